```python
import math
import jax, jax.numpy as jnp
from jax import lax
import numpy as np

D_MODEL = 1024
BATCH = 4
SEQ = 4096
DEPTH = 1

MEM_LEN = 256
SSD_HEADS = 16
SSD_HEAD_DIM = 64
SSD_INNER = SSD_HEADS * SSD_HEAD_DIM
SSD_GROUPS = 4
SSD_STATE = 128
SSD_CONV = 4
SSD_CHUNK = 128
SSD_XBC = SSD_INNER + 2 * SSD_GROUPS * SSD_STATE
ATT_HEADS = 16
ATT_HEAD_DIM = 64
ATT_WIDTH = ATT_HEADS * ATT_HEAD_DIM
KV_RANK = 256
IDX_HEADS = 16
IDX_DIM = 64
TOPK_MAX = 256
Q_BLOCK = 128
MEM_HEADS = 4
MEM_HEAD_DIM = D_MODEL // MEM_HEADS
D_FF = 2816
ALPHA = (2.0 * DEPTH) ** 0.25
BETA = (8.0 * DEPTH) ** -0.25
LN_EPS = 1e-5
RMS_EPS = 1e-6
IN_SPLITS = (D_MODEL, D_MODEL, SSD_INNER, SSD_XBC, SSD_HEADS,
             ATT_WIDTH, KV_RANK, IDX_HEADS * IDX_DIM, IDX_DIM, IDX_HEADS)
D_IN = 2 * D_MODEL + SSD_INNER + SSD_XBC + SSD_HEADS + ATT_WIDTH + KV_RANK + IDX_HEADS * IDX_DIM + IDX_DIM + IDX_HEADS

kernel_name = "hybrid_ssd_dsa_gated_macaron_deepnorm"


def layer_norm(x, g, b):
    xf = x.astype(jnp.float32)
    mu = jnp.mean(xf, -1, keepdims=True)
    var = jnp.mean(jnp.square(xf - mu), -1, keepdims=True)
    return ((xf - mu) * lax.rsqrt(var + LN_EPS) * g + b).astype(x.dtype)


def rms_norm(x, g):
    xf = x.astype(jnp.float32)
    y = xf * lax.rsqrt(jnp.mean(jnp.square(xf), -1, keepdims=True) + RMS_EPS)
    return (y * g).astype(x.dtype)


def swiglu_ffn(x, w_in, w_out):
    gate, up = jnp.split(x @ w_in, 2, axis=-1)
    return (jax.nn.silu(gate) * up) @ w_out


def split_cols(h, sizes):
    out, off = [], 0
    for s in sizes:
        out.append(h[..., off:off + s])
        off += s
    return out


def causal_dwconv(u, w, b):
    width, ch = w.shape
    out = lax.conv_general_dilated(u, w[:, None, :], window_strides=(1,),
                                   padding=[(width - 1, 0)],
                                   dimension_numbers=('NWC', 'WIO', 'NWC'),
                                   feature_group_count=ch)
    return out + b


def segsum(a):
    l = a.shape[-1]
    cs = jnp.cumsum(a, axis=-1)
    diff = cs[..., :, None] - cs[..., None, :]
    mask = jnp.tril(jnp.ones((l, l), dtype=bool))
    return jnp.where(mask, diff, -jnp.inf)


def ssd_chunked_scan(x, dt, a_head, bmat, cmat):
    bsz, seqlen, n_heads, p = x.shape
    g, n = bmat.shape[2], bmat.shape[3]
    r = n_heads // g
    l = SSD_CHUNK
    c = seqlen // l
    x = x.astype(jnp.float32).reshape(bsz, c, l, g, r, p)
    dt = dt.reshape(bsz, c, l, g, r)
    bmat = bmat.astype(jnp.float32).reshape(bsz, c, l, g, n)
    cmat = cmat.astype(jnp.float32).reshape(bsz, c, l, g, n)
    a = (dt * a_head.reshape(g, r)).transpose(0, 3, 4, 1, 2)
    a_cs = jnp.cumsum(a, axis=-1)
    xdt = x * dt[..., None]
    decay_in = jnp.exp(segsum(a))
    cb = jnp.einsum('bclgn,bcsgn->bcgls', cmat, bmat)
    y_diag = jnp.einsum('bcgls,bgrcls,bcsgrp->bclgrp', cb, decay_in, xdt)
    decay_to_end = jnp.exp(a_cs[..., -1:] - a_cs)
    chunk_states = jnp.einsum('bclgn,bgrcl,bclgrp->bcgrpn', bmat, decay_to_end, xdt)
    chunk_decay = jnp.exp(a_cs[..., -1])

    def step(h, inp):
        s_c, d_c = inp
        return d_c[..., None, None] * h + s_c, h

    h0 = jnp.zeros((bsz, g, r, p, n), jnp.float32)
    _, prev = lax.scan(step, h0, (jnp.moveaxis(chunk_states, 1, 0),
                                  jnp.moveaxis(chunk_decay, -1, 0)))
    prev = jnp.moveaxis(prev, 0, 1)
    y_off = jnp.einsum('bclgn,bcgrpn,bgrcl->bclgrp', cmat, prev, jnp.exp(a_cs))
    return (y_diag + y_off).reshape(bsz, seqlen, n_heads, p)


def ssd_branch(z, xbc, dt_raw, conv_w, conv_b, dt_bias, a_log, d_skip, norm_w):
    bsz, seqlen, _ = z.shape
    xbc = jax.nn.silu(causal_dwconv(xbc, conv_w, conv_b))
    gn = SSD_GROUPS * SSD_STATE
    xs, bm, cm = split_cols(xbc, (SSD_INNER, gn, gn))
    xh = xs.reshape(bsz, seqlen, SSD_HEADS, SSD_HEAD_DIM)
    dt = jax.nn.softplus(dt_raw.astype(jnp.float32) + dt_bias)
    a_head = -jnp.exp(a_log.astype(jnp.float32))
    y = ssd_chunked_scan(xh, dt, a_head,
                         bm.reshape(bsz, seqlen, SSD_GROUPS, SSD_STATE),
                         cm.reshape(bsz, seqlen, SSD_GROUPS, SSD_STATE))
    y = y + d_skip[None, None, :, None] * xh.astype(jnp.float32)
    y = y.reshape(bsz, seqlen, SSD_INNER) * jax.nn.silu(z.astype(jnp.float32))
    yg = y.reshape(bsz, seqlen, SSD_GROUPS, SSD_INNER // SSD_GROUPS)
    yg = yg * lax.rsqrt(jnp.mean(jnp.square(yg), -1, keepdims=True) + RMS_EPS)
    return (yg.reshape(bsz, seqlen, SSD_INNER) * norm_w).astype(z.dtype)


def dsa_branch(q, c_kv, q_idx, k_idx, w_idx, kv_norm_w, w_uk, w_uv):
    bsz, seqlen, _ = q.shape
    topk = min(TOPK_MAX, seqlen // 4)
    n_blocks = seqlen // Q_BLOCK
    q = q.reshape(bsz, seqlen, ATT_HEADS, ATT_HEAD_DIM)
    q_lat = jnp.einsum('bthd,hdr->bthr', q, w_uk) * (ATT_HEAD_DIM ** -0.5)
    c_kv = rms_norm(c_kv, kv_norm_w)
    q_idx = q_idx.reshape(bsz, seqlen, IDX_HEADS, IDX_DIM)
    w_idx = w_idx * (IDX_HEADS ** -0.5)
    slopes = 2.0 ** (-8.0 * jnp.arange(1, ATT_HEADS + 1, dtype=jnp.float32) / ATT_HEADS)
    key_pos = jnp.arange(seqlen, dtype=jnp.int32)

    def to_blocks(a):
        return a.reshape(bsz, n_blocks, Q_BLOCK, *a.shape[2:]).swapaxes(0, 1)

    t_blocks = key_pos.reshape(n_blocks, Q_BLOCK)

    def block(args):
        ql, qi, wi, t = args
        logits = jax.nn.relu(jnp.einsum('bqhd,bsd->bqhs', qi, k_idx))
        score = jnp.einsum('bqhs,bqh->bqs', logits, wi).astype(jnp.float32)
        causal = key_pos[None, :] <= t[:, None]
        score = jnp.where(causal[None], score, -jnp.inf)
        _, idx = lax.top_k(score, topk)
        sel = jax.vmap(lambda c, i: c[i])(c_kv, idx)
        s = jnp.einsum('bqhr,bqkr->bqhk', ql, sel).astype(jnp.float32)
        dist = (t[None, :, None] - idx).astype(jnp.float32)
        s = s - slopes[None, None, :, None] * dist[:, :, None, :]
        valid = idx <= t[None, :, None]
        s = jnp.where(valid[:, :, None, :], s, -jnp.inf)
        p = jax.nn.softmax(s, axis=-1).astype(sel.dtype)
        return jnp.einsum('bqhk,bqkr->bqhr', p, sel)

    o = lax.map(block, (to_blocks(q_lat), to_blocks(q_idx), to_blocks(w_idx), t_blocks))
    o = o.swapaxes(0, 1).reshape(bsz, seqlen, ATT_HEADS, KV_RANK)
    return jnp.einsum('bthr,hrd->bthd', o, w_uv).reshape(bsz, seqlen, ATT_WIDTH)


def memory_xattn(x, mem, w_mq, w_mkv, w_mo):
    bsz, seqlen, _ = x.shape
    q = (x @ w_mq).reshape(bsz, seqlen, MEM_HEADS, MEM_HEAD_DIM)
    k, v = jnp.split(mem @ w_mkv, 2, axis=-1)
    k = k.reshape(bsz, -1, MEM_HEADS, MEM_HEAD_DIM)
    v = v.reshape(bsz, -1, MEM_HEADS, MEM_HEAD_DIM)
    s = jnp.einsum('bthd,bmhd->bhtm', q, k).astype(jnp.float32) * (MEM_HEAD_DIM ** -0.5)
    p = jax.nn.softmax(s, axis=-1).astype(v.dtype)
    o = jnp.einsum('bhtm,bmhd->bthd', p, v).reshape(bsz, seqlen, D_MODEL)
    return o @ w_mo


def hybrid_layer(x, mem, ffn1_w_in, ffn1_w_out, ln1_g, ln1_b,
                 w_in, conv_w, conv_b, dt_bias, a_log, d_skip, ssd_norm_w,
                 kv_norm_w, w_uk, w_uv, w_proj_ssd, w_proj_att, w_out, ln2_g, ln2_b,
                 w_mq, w_mkv, w_mo, ln3_g, ln3_b,
                 ffn2_w_in, ffn2_w_out, ln4_g, ln4_b):
    x = layer_norm(ALPHA * x + 0.5 * swiglu_ffn(x, ffn1_w_in, ffn1_w_out), ln1_g, ln1_b)
    h = x @ w_in
    g_ssd, g_att, z, xbc, dt_raw, q, c_kv, q_idx, k_idx, w_idx = split_cols(h, IN_SPLITS)
    y_ssd = ssd_branch(z, xbc, dt_raw, conv_w, conv_b, dt_bias, a_log, d_skip, ssd_norm_w)
    y_att = dsa_branch(q, c_kv, q_idx, k_idx, w_idx, kv_norm_w, w_uk, w_uv)
    merged = jax.nn.sigmoid(g_ssd) * (y_ssd @ w_proj_ssd) + jax.nn.sigmoid(g_att) * (y_att @ w_proj_att)
    x = layer_norm(ALPHA * x + merged @ w_out, ln2_g, ln2_b)
    x = layer_norm(ALPHA * x + memory_xattn(x, mem, w_mq, w_mkv, w_mo), ln3_g, ln3_b)
    x = layer_norm(ALPHA * x + 0.5 * swiglu_ffn(x, ffn2_w_in, ffn2_w_out), ln4_g, ln4_b)
    return x


def setup_inputs(seed: int = 0) -> dict:
    key = jax.random.key(seed)
    keys = iter(jax.random.split(key, 48))
    f32 = jnp.float32
    L = DEPTH

    def nrm(shape, scale):
        return jax.random.normal(next(keys), shape, f32) * scale

    def gain(shape):
        return 1.0 + nrm(shape, 0.02)

    x = nrm((BATCH, SEQ, D_MODEL), 1.0)
    mem = nrm((BATCH, MEM_LEN, D_MODEL), 1.0)
    ffn1_w_in = nrm((L, D_MODEL, 2 * D_FF), D_MODEL ** -0.5)
    ffn1_w_out = nrm((L, D_FF, D_MODEL), BETA * D_FF ** -0.5)
    ln1_g = gain((L, D_MODEL))
    ln1_b = nrm((L, D_MODEL), 0.02)
    w_in = nrm((L, D_MODEL, D_IN), D_MODEL ** -0.5)
    conv_w = nrm((L, SSD_CONV, SSD_XBC), SSD_CONV ** -0.5)
    conv_b = nrm((L, SSD_XBC), 0.02)
    dt0 = jnp.exp(jax.random.uniform(next(keys), (L, SSD_HEADS), f32,
                                     minval=math.log(1e-3), maxval=math.log(1e-1)))
    dt_bias = dt0 + jnp.log(-jnp.expm1(-dt0))
    a_log = jnp.log(jax.random.uniform(next(keys), (L, SSD_HEADS), f32, minval=1.0, maxval=16.0))
    d_skip = gain((L, SSD_HEADS))
    ssd_norm_w = gain((L, SSD_INNER))
    kv_norm_w = gain((L, KV_RANK))
    w_uk = nrm((L, ATT_HEADS, ATT_HEAD_DIM, KV_RANK), KV_RANK ** -0.5)
    w_uv = nrm((L, ATT_HEADS, KV_RANK, ATT_HEAD_DIM), KV_RANK ** -0.5)
    w_proj_ssd = nrm((L, SSD_INNER, D_MODEL), SSD_INNER ** -0.5)
    w_proj_att = nrm((L, ATT_WIDTH, D_MODEL), ATT_WIDTH ** -0.5)
    w_out = nrm((L, D_MODEL, D_MODEL), BETA * D_MODEL ** -0.5)
    ln2_g = gain((L, D_MODEL))
    ln2_b = nrm((L, D_MODEL), 0.02)
    w_mq = nrm((L, D_MODEL, D_MODEL), D_MODEL ** -0.5)
    w_mkv = nrm((L, D_MODEL, 2 * D_MODEL), D_MODEL ** -0.5)
    w_mo = nrm((L, D_MODEL, D_MODEL), BETA * D_MODEL ** -0.5)
    ln3_g = gain((L, D_MODEL))
    ln3_b = nrm((L, D_MODEL), 0.02)
    ffn2_w_in = nrm((L, D_MODEL, 2 * D_FF), D_MODEL ** -0.5)
    ffn2_w_out = nrm((L, D_FF, D_MODEL), BETA * D_FF ** -0.5)
    ln4_g = gain((L, D_MODEL))
    ln4_b = nrm((L, D_MODEL), 0.02)
    return {"x": x, "mem": mem,
            "ffn1_w_in": ffn1_w_in, "ffn1_w_out": ffn1_w_out, "ln1_g": ln1_g, "ln1_b": ln1_b,
            "w_in": w_in, "conv_w": conv_w, "conv_b": conv_b, "dt_bias": dt_bias,
            "a_log": a_log, "d_skip": d_skip, "ssd_norm_w": ssd_norm_w,
            "kv_norm_w": kv_norm_w, "w_uk": w_uk, "w_uv": w_uv,
            "w_proj_ssd": w_proj_ssd, "w_proj_att": w_proj_att, "w_out": w_out,
            "ln2_g": ln2_g, "ln2_b": ln2_b,
            "w_mq": w_mq, "w_mkv": w_mkv, "w_mo": w_mo, "ln3_g": ln3_g, "ln3_b": ln3_b,
            "ffn2_w_in": ffn2_w_in, "ffn2_w_out": ffn2_w_out, "ln4_g": ln4_g, "ln4_b": ln4_b}


def reference(x, mem, ffn1_w_in, ffn1_w_out, ln1_g, ln1_b,
              w_in, conv_w, conv_b, dt_bias, a_log, d_skip, ssd_norm_w,
              kv_norm_w, w_uk, w_uv, w_proj_ssd, w_proj_att, w_out, ln2_g, ln2_b,
              w_mq, w_mkv, w_mo, ln3_g, ln3_b,
              ffn2_w_in, ffn2_w_out, ln4_g, ln4_b):
    for layer in range(DEPTH):
        x = hybrid_layer(x, mem,
                         ffn1_w_in[layer], ffn1_w_out[layer], ln1_g[layer], ln1_b[layer],
                         w_in[layer], conv_w[layer], conv_b[layer], dt_bias[layer],
                         a_log[layer], d_skip[layer], ssd_norm_w[layer],
                         kv_norm_w[layer], w_uk[layer], w_uv[layer],
                         w_proj_ssd[layer], w_proj_att[layer], w_out[layer],
                         ln2_g[layer], ln2_b[layer],
                         w_mq[layer], w_mkv[layer], w_mo[layer], ln3_g[layer], ln3_b[layer],
                         ffn2_w_in[layer], ffn2_w_out[layer], ln4_g[layer], ln4_b[layer])
    return x
```

```python
import functools
import math

import jax
import jax.numpy as jnp
from jax import lax
from jax.experimental import pallas as pl
from jax.experimental.pallas import tpu as pltpu

F32 = jnp.float32
BF16 = jnp.bfloat16

D_MODEL = 1024
DEPTH = 1
SSD_HEADS = 16
SSD_HEAD_DIM = 64
SSD_INNER = SSD_HEADS * SSD_HEAD_DIM
SSD_GROUPS = 4
SSD_STATE = 128
SSD_CONV = 4
SSD_CHUNK = 128
SSD_GN = SSD_GROUPS * SSD_STATE
SSD_XBC = SSD_INNER + 2 * SSD_GN
ATT_HEADS = 16
ATT_HEAD_DIM = 64
ATT_WIDTH = ATT_HEADS * ATT_HEAD_DIM
KV_RANK = 256
IDX_HEADS = 16
IDX_DIM = 64
TOPK_MAX = 256
Q_BLOCK = 128
MEM_HEADS = 4
MEM_HEAD_DIM = D_MODEL // MEM_HEADS
D_FF = 2816
ALPHA = (2.0 * DEPTH) ** 0.25
LN_EPS = 1e-5
RMS_EPS = 1e-6

INT_MIN = -(2 ** 31)
INT_MAX = 2 ** 31 - 1
LOWEST_FINITE_KEY = INT_MIN + 2 ** 23
NEG_BIG = -1e30
LOG2E = math.log2(math.e)

LANES = 128
SUBLANES = 8
TOKEN_TILE = 512
FFN_CHUNK = 1408
KEY_CHUNK = 512
CONV_TAIL = 8
VMEM_LIMIT = 56 * 1024 * 1024

NT_DIMS = (((1,), (1,)), ((), ()))
TN_DIMS = (((0,), (0,)), ((), ()))


def _const_spec(shape):
    return pl.BlockSpec(shape, lambda *_: (0,) * len(shape), pipeline_mode=pl.Buffered(1))


def _params(*sem):
    return pltpu.CompilerParams(dimension_semantics=sem, vmem_limit_bytes=VMEM_LIMIT)


def _layer_norm(y, g, b):
    mu = jnp.mean(y, axis=-1, keepdims=True)
    d = y - mu
    var = jnp.mean(d * d, axis=-1, keepdims=True)
    return d * lax.rsqrt(var + LN_EPS) * g + b


def _dot(a, b):
    return jnp.dot(a, b, preferred_element_type=F32)


def _dot_nt(a, b):
    return lax.dot_general(a, b, NT_DIMS, preferred_element_type=F32)


def _dot_exact(a, b, dims=(((1,), (0,)), ((), ()))):
    return lax.dot_general(a, b, dims, precision=lax.Precision.HIGHEST, preferred_element_type=F32)


def _ffn_ln_body(x_ref, win_ref, wout_ref, g_ref, b_ref, o_ref):
    x = x_ref[...]
    xb = x.astype(BF16)
    acc = jnp.zeros(x.shape, F32)
    for c in range(D_FF // FFN_CHUNK):
        lo = c * FFN_CHUNK
        gate = _dot(xb, win_ref[:, lo:lo + FFN_CHUNK])
        up = _dot(xb, win_ref[:, D_FF + lo:D_FF + lo + FFN_CHUNK])
        act = (gate * jax.nn.sigmoid(gate) * up).astype(BF16)
        acc = acc + _dot(act, wout_ref[lo:lo + FFN_CHUNK, :])
    o_ref[...] = _layer_norm(ALPHA * x + 0.5 * acc, g_ref[...], b_ref[...])


def _ffn_ln(x, w_in, w_out, g, b, tm):
    t = x.shape[0]
    row = lambda i: (i, 0)
    return pl.pallas_call(
        _ffn_ln_body,
        grid=(t // tm,),
        in_specs=[pl.BlockSpec((tm, D_MODEL), row),
                  _const_spec(w_in.shape), _const_spec(w_out.shape),
                  _const_spec(g.shape), _const_spec(b.shape)],
        out_specs=pl.BlockSpec((tm, D_MODEL), row),
        out_shape=jax.ShapeDtypeStruct((t, D_MODEL), F32),
        compiler_params=_params("parallel"),
        name="ffn_ln",
    )(x, w_in, w_out, g, b)


def _in_proj_body(x_ref, wg_ref, wz_ref, wxbc_ref, wq_ref, wckv_ref, wqit_ref, wmisc_ref, wwit_ref, kvn_ref,
                  sg_ref, sz_ref, xbc_ref, q_ref, ckv_ref, qit_ref, ki_ref, dt_ref, wit_ref):
    xb = x_ref[...].astype(BF16)
    sg_ref[...] = jax.nn.sigmoid(_dot(xb, wg_ref[...])).astype(BF16)
    z = _dot(xb, wz_ref[...])
    sz_ref[...] = (z * jax.nn.sigmoid(z)).astype(BF16)
    xbc_ref[...] = _dot(xb, wxbc_ref[...]).astype(BF16)
    q_ref[...] = _dot(xb, wq_ref[...]).astype(BF16)
    c = _dot(xb, wckv_ref[...])
    c = c * lax.rsqrt(jnp.mean(c * c, axis=-1, keepdims=True) + RMS_EPS) * kvn_ref[...]
    ckv_ref[...] = c.astype(BF16)
    qit_ref[...] = _dot_nt(wqit_ref[...], xb).astype(BF16)
    wit_ref[...] = _dot_nt(wwit_ref[...], xb) * (IDX_HEADS ** -0.5)
    misc = _dot(xb, wmisc_ref[...])
    dt_ref[...] = misc[:, 0:SSD_HEADS]
    ki_ref[...] = misc[:, SSD_HEADS:SSD_HEADS + IDX_DIM].astype(BF16)


def _in_proj(x, wg, wz, wxbc, wq, wckv, wqit, wmisc, wwit, kvn, tm):
    t = x.shape[0]
    row = lambda i: (i, 0)
    col = lambda i: (0, i)
    out = [
        ((t, 2 * D_MODEL), BF16, (tm, 2 * D_MODEL), row),
        ((t, SSD_INNER), BF16, (tm, SSD_INNER), row),
        ((t, SSD_XBC), BF16, (tm, SSD_XBC), row),
        ((t, ATT_WIDTH), BF16, (tm, ATT_WIDTH), row),
        ((t, KV_RANK), BF16, (tm, KV_RANK), row),
        ((IDX_HEADS * IDX_DIM, t), BF16, (IDX_HEADS * IDX_DIM, tm), col),
        ((t, IDX_DIM), BF16, (tm, IDX_DIM), row),
        ((t, SSD_HEADS), F32, (tm, SSD_HEADS), row),
        ((IDX_HEADS, t), F32, (IDX_HEADS, tm), col),
    ]
    return pl.pallas_call(
        _in_proj_body,
        grid=(t // tm,),
        in_specs=[pl.BlockSpec((tm, D_MODEL), row)] + [_const_spec(w.shape) for w in
                                                        (wg, wz, wxbc, wq, wckv, wqit, wmisc, wwit, kvn)],
        out_specs=[pl.BlockSpec(blk, im) for _, _, blk, im in out],
        out_shape=[jax.ShapeDtypeStruct(shp, dt) for shp, dt, _, _ in out],
        compiler_params=_params("parallel"),
        name="in_proj",
    )(x, wg, wz, wxbc, wq, wckv, wqit, wmisc, wwit, kvn)


def _ssd_body(xbc_ref, sz_ref, dt_ref, cw_ref, cb_ref, dtb_ref, alog_ref, dskip_ref, nw_ref, expand_ref,
              y_ref, state_ref, tail_ref):
    chunk = xbc_ref.shape[0]
    hp = SSD_HEAD_DIM
    gw = SSD_INNER // SSD_GROUPS

    @pl.when(pl.program_id(1) == 0)
    def _():
        state_ref[...] = jnp.zeros(state_ref.shape, F32)
        tail_ref[...] = jnp.zeros(tail_ref.shape, F32)

    u = xbc_ref[...].astype(F32)
    ext = jnp.concatenate([tail_ref[...], u], axis=0)
    tail_ref[...] = u[chunk - CONV_TAIL:, :]
    conv = cb_ref[...] + jnp.zeros_like(u)
    for k in range(SSD_CONV):
        off = CONV_TAIL - (SSD_CONV - 1) + k
        conv = conv + cw_ref[k:k + 1, :] * ext[off:off + chunk, :]
    act = conv * jax.nn.sigmoid(conv)
    xs = act[:, :SSD_INNER]

    pre = dt_ref[...] + dtb_ref[...]
    dt = jnp.maximum(pre, 0.0) + jnp.log(1.0 + jnp.exp(-jnp.abs(pre)))
    a = dt * (-jnp.exp(alog_ref[...]))
    r_i = lax.broadcasted_iota(jnp.int32, (chunk, chunk), 0)
    c_i = lax.broadcasted_iota(jnp.int32, (chunk, chunk), 1)
    causal = r_i >= c_i
    a_cs = _dot_exact(causal.astype(F32), a)
    eye = (lax.broadcasted_iota(jnp.int32, (SSD_HEADS, SSD_HEADS), 0)
           == lax.broadcasted_iota(jnp.int32, (SSD_HEADS, SSD_HEADS), 1)).astype(F32)
    a_cs_t = _dot_exact(eye, a_cs, NT_DIMS)
    expand = expand_ref[...]
    dt_e = _dot_exact(dt, expand)
    a_cs_e = _dot_exact(a_cs, expand)
    a_end_e = a_cs_e[chunk - 1:chunk, :]

    xdt = xs * dt_e
    xdt_b = xdt.astype(BF16)
    xdt_end_b = (xdt * jnp.exp(a_end_e - a_cs_e)).astype(BF16)
    state = state_ref[...]
    state_b = state.astype(BF16)

    y_diag, y_off, new_state = [], [], []
    for g in range(SSD_GROUPS):
        bm = act[:, SSD_INNER + g * SSD_STATE:SSD_INNER + (g + 1) * SSD_STATE].astype(BF16)
        cm = act[:, SSD_INNER + SSD_GN + g * SSD_STATE:SSD_INNER + SSD_GN + (g + 1) * SSD_STATE].astype(BF16)
        cb = _dot_nt(cm, bm)
        y_off.append(_dot(cm, state_b[:, g * gw:(g + 1) * gw]))
        new_state.append(lax.dot_general(bm, xdt_end_b[:, g * gw:(g + 1) * gw], TN_DIMS,
                                         preferred_element_type=F32))
        for r in range(SSD_HEADS // SSD_GROUPS):
            h = g * (SSD_HEADS // SSD_GROUPS) + r
            seg = a_cs[:, h:h + 1] - a_cs_t[h:h + 1, :]
            decay = jnp.exp(jnp.where(causal, seg, NEG_BIG))
            y_diag.append(_dot((cb * decay).astype(BF16), xdt_b[:, h * hp:(h + 1) * hp]))
    state_ref[...] = jnp.exp(a_end_e) * state + jnp.concatenate(new_state, axis=1)

    y = jnp.concatenate(y_diag, axis=1) + jnp.concatenate(y_off, axis=1) * jnp.exp(a_cs_e)
    y = (y + dskip_ref[...] * xs) * sz_ref[...].astype(F32)
    outs = []
    for g in range(SSD_GROUPS):
        yg = y[:, g * gw:(g + 1) * gw]
        outs.append(yg * lax.rsqrt(jnp.mean(yg * yg, axis=-1, keepdims=True) + RMS_EPS))
    y_ref[...] = (jnp.concatenate(outs, axis=1) * nw_ref[...]).astype(BF16)


def _ssd(xbc, sz, dt, conv_w, conv_b, dt_bias, a_log, d_skip_e, norm_w, expand, bsz, seqlen):
    t = xbc.shape[0]
    nc = seqlen // SSD_CHUNK
    row = lambda b, c: (b * nc + c, 0)
    return pl.pallas_call(
        _ssd_body,
        grid=(bsz, nc),
        in_specs=[pl.BlockSpec((SSD_CHUNK, SSD_XBC), row), pl.BlockSpec((SSD_CHUNK, SSD_INNER), row),
                  pl.BlockSpec((SSD_CHUNK, SSD_HEADS), row)]
                 + [_const_spec(w.shape) for w in (conv_w, conv_b, dt_bias, a_log, d_skip_e, norm_w, expand)],
        out_specs=pl.BlockSpec((SSD_CHUNK, SSD_INNER), row),
        out_shape=jax.ShapeDtypeStruct((t, SSD_INNER), BF16),
        scratch_shapes=[pltpu.VMEM((SSD_STATE, SSD_INNER), F32), pltpu.VMEM((CONV_TAIL, SSD_XBC), F32)],
        compiler_params=_params("arbitrary", "arbitrary"),
        name="ssd",
    )(xbc, sz, dt, conv_w, conv_b, dt_bias, a_log, d_skip_e, norm_w, expand)


def _score_of_key(key):
    return pltpu.bitcast(jnp.where(key >= 0, key, key ^ INT_MAX), F32)


def _dsa_body(slopes_ref, q_ref, qit_ref, wit_ref, ckv_ref, ki_ref, wuk_ref, wuv_ref, y_ref,
              sc_ref, bias_ref, ql_ref, s_ref, p_ref, m_ref, l_ref, alpha_ref, acc_ref, *, topk, pos_bits):
    qb = q_ref.shape[0]
    kc = KEY_CHUNK
    t0 = pl.program_id(1) * qb
    n_chunks = (t0 + qb + kc - 1) // kc
    q_pos = t0 + lax.broadcasted_iota(jnp.int32, (1, qb), 1)

    def key_pos(c):
        return c * kc + lax.broadcasted_iota(jnp.int32, (kc, qb), 0)

    def chunk_ds(c):
        return pl.ds(pl.multiple_of(c * kc, kc), kc)

    def score_chunk(c, carry):
        kblk = ki_ref[chunk_ds(c), :]
        score = jnp.zeros((kc, qb), F32)
        for h in range(0, IDX_HEADS, 2):
            w = jnp.concatenate([qit_ref[h * IDX_DIM:(h + 1) * IDX_DIM, :],
                                 qit_ref[(h + 1) * IDX_DIM:(h + 2) * IDX_DIM, :]], axis=1)
            wi = jnp.concatenate([wit_ref[h:h + 1, :], wit_ref[h + 1:h + 2, :]], axis=1)
            t = jnp.maximum(_dot(kblk, w), 0.0) * wi
            score = score + t[:, :qb] + t[:, qb:]
        sc_ref[chunk_ds(c), :] = jnp.where(key_pos(c) <= q_pos, score, -jnp.inf)
        return carry

    lax.fori_loop(0, n_chunks, score_chunk, 0)

    def count(pred):
        def body(c, part):
            hit = jnp.where(pred(sc_ref[chunk_ds(c), :], key_pos(c)), 1, 0)
            return part + jnp.sum(hit.reshape(kc // SUBLANES, SUBLANES, qb), axis=0)
        part = lax.fori_loop(0, n_chunks, body, jnp.zeros((SUBLANES, qb), jnp.int32))
        return jnp.sum(part, axis=0, keepdims=True)

    def thr_bit(i, thr):
        cand = thr + (jnp.int32(1) << (31 - i))
        cand_f = _score_of_key(cand)
        return jnp.where(count(lambda s, p: s >= cand_f) >= topk, cand, thr)

    thr = lax.fori_loop(0, 32, thr_bit, jnp.full((1, qb), INT_MIN, jnp.int32))
    thr_f = jnp.where(thr >= LOWEST_FINITE_KEY, _score_of_key(jnp.maximum(thr, LOWEST_FINITE_KEY)), -jnp.inf)

    need = topk - count(lambda s, p: s > thr_f)
    n_tie = count(lambda s, p: (s == thr_f) & (p <= q_pos))

    def tie_cut():
        def cut_bit(i, cut):
            cand = cut + (jnp.int32(1) << (pos_bits - 1 - i))
            n_before = count(lambda s, p: (s == thr_f) & (p <= q_pos) & (p < cand))
            return jnp.where(n_before < need, cand, cut)
        return lax.fori_loop(0, pos_bits, cut_bit, jnp.zeros((1, qb), jnp.int32))

    any_tie = jnp.max(jnp.where(n_tie > need, 1, 0)) > 0
    cut = lax.cond(any_tie, tie_cut, lambda: jnp.full((1, qb), INT_MAX, jnp.int32))

    eye = (lax.broadcasted_iota(jnp.int32, (qb, qb), 0)
           == lax.broadcasted_iota(jnp.int32, (qb, qb), 1)).astype(BF16)

    def bias_chunk(c, carry):
        s = sc_ref[chunk_ds(c), :]
        p = key_pos(c)
        sel = (p <= q_pos) & ((s > thr_f) | ((s == thr_f) & (p <= cut)))
        bias_ref[:, chunk_ds(c)] = _dot_nt(eye, jnp.where(sel, 0.0, NEG_BIG).astype(BF16))
        return carry

    lax.fori_loop(0, n_chunks, bias_chunk, 0)

    for h in range(ATT_HEADS):
        ql = _dot(q_ref[:, h * ATT_HEAD_DIM:(h + 1) * ATT_HEAD_DIM], wuk_ref[h])
        ql_ref[h * qb:(h + 1) * qb, :] = (ql * (ATT_HEAD_DIM ** -0.5 * LOG2E)).astype(BF16)
    m_ref[...] = jnp.full(m_ref.shape, NEG_BIG, F32)
    l_ref[...] = jnp.zeros(l_ref.shape, F32)
    acc_ref[...] = jnp.zeros(acc_ref.shape, F32)
    lane_tiles = kc // LANES

    def att_chunk(c, carry):
        cblk = ckv_ref[chunk_ds(c), :]
        s_ref[...] = _dot_nt(ql_ref[...], cblk)
        rel = (c * kc - t0 + lax.broadcasted_iota(jnp.int32, (1, kc), 1)).astype(F32)
        bias = bias_ref[:, chunk_ds(c)]
        for h in range(ATT_HEADS):
            rows = slice(h * qb, (h + 1) * qb)
            s = s_ref[rows, :] + (slopes_ref[h] * LOG2E) * rel + bias
            s_ref[rows, :] = s
            mx = s[:, 0:LANES]
            for i in range(1, lane_tiles):
                mx = jnp.maximum(mx, s[:, i * LANES:(i + 1) * LANES])
            m_old = m_ref[rows, :]
            m_new = jnp.maximum(m_old, jnp.max(mx, axis=-1, keepdims=True))
            m_ref[rows, :] = m_new
            alpha_ref[rows, :] = jnp.exp2(m_old - m_new)
        for h in range(ATT_HEADS):
            rows = slice(h * qb, (h + 1) * qb)
            m = m_ref[rows, :]
            l = alpha_ref[rows, :] * l_ref[rows, :]
            for i in range(lane_tiles):
                p = jnp.exp2(s_ref[rows, i * LANES:(i + 1) * LANES] - m)
                l = l + p
                p_ref[rows, i * LANES:(i + 1) * LANES] = p.astype(BF16)
            l_ref[rows, :] = l
        pv = _dot(p_ref[...], cblk)
        alpha = alpha_ref[...]
        for i in range(KV_RANK // LANES):
            cols = slice(i * LANES, (i + 1) * LANES)
            acc_ref[:, cols] = alpha * acc_ref[:, cols] + pv[:, cols]
        return carry

    lax.fori_loop(0, n_chunks, att_chunk, 0)

    o = (acc_ref[...] / jnp.sum(l_ref[...], axis=-1, keepdims=True)).astype(BF16)
    y_ref[...] = jnp.concatenate([_dot(o[h * qb:(h + 1) * qb, :], wuv_ref[h]) for h in range(ATT_HEADS)],
                                 axis=1).astype(BF16)


def _dsa(q, qit, wit, ckv, ki, w_uk, w_uv, slopes, bsz, seqlen):
    t = q.shape[0]
    qb = Q_BLOCK
    nq = seqlen // qb
    rows = ATT_HEADS * qb
    topk = min(TOPK_MAX, seqlen // 4)
    pos_bits = max(1, (seqlen - 1).bit_length())
    seq_pad = -(-seqlen // KEY_CHUNK) * KEY_CHUNK
    row = lambda b, j: (b * nq + j, 0)
    col = lambda b, j: (0, b * nq + j)
    per_batch = lambda b, j: (b, 0, 0)
    if seq_pad != seqlen:
        ckv = jnp.pad(ckv.reshape(bsz, seqlen, KV_RANK), ((0, 0), (0, seq_pad - seqlen), (0, 0)))
        ki = jnp.pad(ki.reshape(bsz, seqlen, IDX_DIM), ((0, 0), (0, seq_pad - seqlen), (0, 0)))
    body = functools.partial(_dsa_body, topk=topk, pos_bits=pos_bits)
    return pl.pallas_call(
        body,
        grid=(bsz, nq),
        in_specs=[pl.BlockSpec(memory_space=pltpu.SMEM),
                  pl.BlockSpec((qb, ATT_WIDTH), row), pl.BlockSpec((IDX_HEADS * IDX_DIM, qb), col),
                  pl.BlockSpec((IDX_HEADS, qb), col),
                  pl.BlockSpec((None, seq_pad, KV_RANK), per_batch),
                  pl.BlockSpec((None, seq_pad, IDX_DIM), per_batch),
                  _const_spec(w_uk.shape), _const_spec(w_uv.shape)],
        out_specs=pl.BlockSpec((qb, ATT_WIDTH), row),
        out_shape=jax.ShapeDtypeStruct((t, ATT_WIDTH), BF16),
        scratch_shapes=[pltpu.VMEM((seq_pad, qb), F32), pltpu.VMEM((qb, seq_pad), F32),
                        pltpu.VMEM((rows, KV_RANK), BF16),
                        pltpu.VMEM((rows, KEY_CHUNK), F32), pltpu.VMEM((rows, KEY_CHUNK), BF16),
                        pltpu.VMEM((rows, LANES), F32), pltpu.VMEM((rows, LANES), F32),
                        pltpu.VMEM((rows, LANES), F32), pltpu.VMEM((rows, KV_RANK), F32)],
        compiler_params=_params("arbitrary", "arbitrary"),
        name="dsa",
    )(slopes, q, qit, wit, ckv.reshape(bsz, seq_pad, KV_RANK), ki.reshape(bsz, seq_pad, IDX_DIM), w_uk, w_uv)


def _merge_ln_body(x_ref, ys_ref, ya_ref, sg_ref, wps_ref, wpa_ref, wo_ref, g_ref, b_ref, o_ref):
    sg = sg_ref[...].astype(F32)
    merged = (sg[:, :D_MODEL] * _dot(ys_ref[...], wps_ref[...])
              + sg[:, D_MODEL:] * _dot(ya_ref[...], wpa_ref[...]))
    out = _dot(merged.astype(BF16), wo_ref[...])
    o_ref[...] = _layer_norm(ALPHA * x_ref[...] + out, g_ref[...], b_ref[...])


def _merge_ln(x, ys, ya, sg, wps, wpa, wo, g, b, tm):
    t = x.shape[0]
    row = lambda i: (i, 0)
    return pl.pallas_call(
        _merge_ln_body,
        grid=(t // tm,),
        in_specs=[pl.BlockSpec((tm, D_MODEL), row), pl.BlockSpec((tm, SSD_INNER), row),
                  pl.BlockSpec((tm, ATT_WIDTH), row), pl.BlockSpec((tm, 2 * D_MODEL), row)]
                 + [_const_spec(w.shape) for w in (wps, wpa, wo, g, b)],
        out_specs=pl.BlockSpec((tm, D_MODEL), row),
        out_shape=jax.ShapeDtypeStruct((t, D_MODEL), F32),
        compiler_params=_params("parallel"),
        name="merge_ln",
    )(x, ys, ya, sg, wps, wpa, wo, g, b)


def _mem_kv_body(m_ref, w_ref, o_ref):
    o_ref[...] = _dot(m_ref[...].astype(BF16), w_ref[...]).astype(BF16)


def _mem_kv(mem, w_mkv, bsz, mem_len):
    row = lambda b: (b, 0)
    return pl.pallas_call(
        _mem_kv_body,
        grid=(bsz,),
        in_specs=[pl.BlockSpec((mem_len, D_MODEL), row), _const_spec(w_mkv.shape)],
        out_specs=pl.BlockSpec((mem_len, 2 * D_MODEL), row),
        out_shape=jax.ShapeDtypeStruct((bsz * mem_len, 2 * D_MODEL), BF16),
        compiler_params=_params("parallel"),
        name="mem_kv",
    )(mem, w_mkv)


def _xattn_ln_body(x_ref, kv_ref, wq_ref, wo_ref, g_ref, b_ref, o_ref):
    x = x_ref[...]
    q = (_dot(x.astype(BF16), wq_ref[...]) * (MEM_HEAD_DIM ** -0.5)).astype(BF16)
    heads = []
    for h in range(MEM_HEADS):
        lo = h * MEM_HEAD_DIM
        s = _dot_nt(q[:, lo:lo + MEM_HEAD_DIM], kv_ref[:, lo:lo + MEM_HEAD_DIM])
        p = jnp.exp(s - jnp.max(s, axis=-1, keepdims=True))
        o = _dot(p.astype(BF16), kv_ref[:, D_MODEL + lo:D_MODEL + lo + MEM_HEAD_DIM])
        heads.append((o / jnp.sum(p, axis=-1, keepdims=True)).astype(BF16))
    out = _dot(jnp.concatenate(heads, axis=1), wo_ref[...])
    o_ref[...] = _layer_norm(ALPHA * x + out, g_ref[...], b_ref[...])


def _xattn_ln(x, kv, wq, wo, g, b, bsz, seqlen, mem_len, tm):
    t = x.shape[0]
    nt = seqlen // tm
    return pl.pallas_call(
        _xattn_ln_body,
        grid=(bsz, nt),
        in_specs=[pl.BlockSpec((tm, D_MODEL), lambda b_, i: (b_ * nt + i, 0)),
                  pl.BlockSpec((mem_len, 2 * D_MODEL), lambda b_, i: (b_, 0))]
                 + [_const_spec(w.shape) for w in (wq, wo, g, b)],
        out_specs=pl.BlockSpec((tm, D_MODEL), lambda b_, i: (b_ * nt + i, 0)),
        out_shape=jax.ShapeDtypeStruct((t, D_MODEL), F32),
        compiler_params=_params("parallel", "parallel"),
        name="xattn_ln",
    )(x, kv, wq, wo, g, b)


def _split_w_in(w_in):
    sizes = (D_MODEL, D_MODEL, SSD_INNER, SSD_XBC, SSD_HEADS, ATT_WIDTH, KV_RANK, IDX_HEADS * IDX_DIM,
             IDX_DIM, IDX_HEADS)
    parts, off = [], 0
    for s in sizes:
        parts.append(w_in[:, off:off + s])
        off += s
    g_ssd, g_att, z, xbc, dt, q, ckv, qi, ki, wi = parts
    misc = jnp.concatenate([dt, ki, jnp.zeros((w_in.shape[0], LANES - SSD_HEADS - IDX_DIM), w_in.dtype)], axis=1)
    return [w.astype(BF16) for w in (jnp.concatenate([g_ssd, g_att], axis=1), z, xbc, q, ckv, qi.T, misc, wi.T)]


def _layer(x, mem, bsz, seqlen, mem_len, p):
    (ffn1_w_in, ffn1_w_out, ln1_g, ln1_b, w_in, conv_w, conv_b, dt_bias, a_log, d_skip, ssd_norm_w,
     kv_norm_w, w_uk, w_uv, w_proj_ssd, w_proj_att, w_out, ln2_g, ln2_b,
     w_mq, w_mkv, w_mo, ln3_g, ln3_b, ffn2_w_in, ffn2_w_out, ln4_g, ln4_b) = p
    tm = min(TOKEN_TILE, seqlen)
    vec = lambda v: v.reshape(1, -1)
    bf = lambda w: w.astype(BF16)

    x = _ffn_ln(x, bf(ffn1_w_in), bf(ffn1_w_out), vec(ln1_g), vec(ln1_b), tm)

    sg, sz, xbc, q, ckv, qit, ki, dt, wit = _in_proj(x, *_split_w_in(w_in), vec(kv_norm_w), tm)

    expand = jnp.repeat(jnp.eye(SSD_HEADS, dtype=F32), SSD_HEAD_DIM, axis=1)
    y_ssd = _ssd(xbc, sz, dt, conv_w, vec(conv_b), vec(dt_bias), vec(a_log),
                 vec(jnp.repeat(d_skip, SSD_HEAD_DIM)), vec(ssd_norm_w), expand, bsz, seqlen)

    slopes = 2.0 ** (-8.0 * jnp.arange(1, ATT_HEADS + 1, dtype=F32) / ATT_HEADS)
    y_att = _dsa(q, qit, wit, ckv, ki, bf(w_uk), bf(w_uv), slopes, bsz, seqlen)

    x = _merge_ln(x, y_ssd, y_att, sg, bf(w_proj_ssd), bf(w_proj_att), bf(w_out), vec(ln2_g), vec(ln2_b), tm)

    kv = _mem_kv(mem, bf(w_mkv), bsz, mem_len)
    x = _xattn_ln(x, kv, bf(w_mq), bf(w_mo), vec(ln3_g), vec(ln3_b), bsz, seqlen, mem_len, tm)

    return _ffn_ln(x, bf(ffn2_w_in), bf(ffn2_w_out), vec(ln4_g), vec(ln4_b), tm)


def kernel(x, mem, ffn1_w_in, ffn1_w_out, ln1_g, ln1_b, w_in, conv_w, conv_b, dt_bias, a_log, d_skip,
           ssd_norm_w, kv_norm_w, w_uk, w_uv, w_proj_ssd, w_proj_att, w_out, ln2_g, ln2_b,
           w_mq, w_mkv, w_mo, ln3_g, ln3_b, ffn2_w_in, ffn2_w_out, ln4_g, ln4_b):
    bsz, seqlen, d = x.shape
    mem_len = mem.shape[1]
    params = (ffn1_w_in, ffn1_w_out, ln1_g, ln1_b, w_in, conv_w, conv_b, dt_bias, a_log, d_skip, ssd_norm_w,
              kv_norm_w, w_uk, w_uv, w_proj_ssd, w_proj_att, w_out, ln2_g, ln2_b,
              w_mq, w_mkv, w_mo, ln3_g, ln3_b, ffn2_w_in, ffn2_w_out, ln4_g, ln4_b)
    h = x.reshape(bsz * seqlen, d)
    m = mem.reshape(bsz * mem_len, d)
    for layer in range(ffn1_w_in.shape[0]):
        h = _layer(h, m, bsz, seqlen, mem_len, tuple(w[layer] for w in params))
    return h.reshape(bsz, seqlen, d)
```

```python
import functools
import math

import jax
import jax.numpy as jnp
from jax import lax
from jax.experimental import pallas as pl
from jax.experimental.pallas import tpu as pltpu

F32 = jnp.float32
BF16 = jnp.bfloat16

D_MODEL = 1024
DEPTH = 1
SSD_HEADS = 16
SSD_HEAD_DIM = 64
SSD_INNER = SSD_HEADS * SSD_HEAD_DIM
SSD_GROUPS = 4
SSD_STATE = 128
SSD_CONV = 4
SSD_CHUNK = 128
SSD_GN = SSD_GROUPS * SSD_STATE
SSD_XBC = SSD_INNER + 2 * SSD_GN
ATT_HEADS = 16
ATT_HEAD_DIM = 64
ATT_WIDTH = ATT_HEADS * ATT_HEAD_DIM
KV_RANK = 256
IDX_HEADS = 16
IDX_DIM = 64
TOPK_MAX = 256
Q_BLOCK = 128
MEM_HEADS = 4
MEM_HEAD_DIM = D_MODEL // MEM_HEADS
D_FF = 2816
ALPHA = (2.0 * DEPTH) ** 0.25
LN_EPS = 1e-5
RMS_EPS = 1e-6

INT_MIN = -(2 ** 31)
INT_MAX = 2 ** 31 - 1
LOWEST_FINITE_KEY = INT_MIN + 2 ** 23
NEG_BIG = -1e30
LOG2E = math.log2(math.e)

LANES = 128
SUBLANES = 8
TOKEN_TILE = 512
FFN_CHUNK = 1408
KEY_CHUNK = 512
THR_BITS_ALWAYS = 24
THR_BITS_STEP = 2
MXU_SPLIT = 2
CONV_TAIL = 8
VMEM_LIMIT = 56 * 1024 * 1024

NT_DIMS = (((1,), (1,)), ((), ()))
TN_DIMS = (((0,), (0,)), ((), ()))


def _const_spec(shape):
    return pl.BlockSpec(shape, lambda *_: (0,) * len(shape), pipeline_mode=pl.Buffered(1))


def _params(*sem):
    return pltpu.CompilerParams(dimension_semantics=sem, vmem_limit_bytes=VMEM_LIMIT)


def _layer_norm(y, g, b):
    mu = jnp.mean(y, axis=-1, keepdims=True)
    d = y - mu
    var = jnp.mean(d * d, axis=-1, keepdims=True)
    return d * lax.rsqrt(var + LN_EPS) * g + b


def _dot(a, b):
    return jnp.dot(a, b, preferred_element_type=F32)


def _dot_nt(a, b):
    return lax.dot_general(a, b, NT_DIMS, preferred_element_type=F32)


def _dot_exact(a, b, dims=(((1,), (0,)), ((), ()))):
    return lax.dot_general(a, b, dims, precision=lax.Precision.HIGHEST, preferred_element_type=F32)


def _ffn_ln_body(x_ref, win_ref, wout_ref, g_ref, b_ref, o_ref):
    x = x_ref[...]
    xb = x.astype(BF16)
    acc = jnp.zeros(x.shape, F32)
    for c in range(D_FF // FFN_CHUNK):
        lo = c * FFN_CHUNK
        gate = _dot(xb, win_ref[:, lo:lo + FFN_CHUNK])
        up = _dot(xb, win_ref[:, D_FF + lo:D_FF + lo + FFN_CHUNK])
        act = (gate * jax.nn.sigmoid(gate) * up).astype(BF16)
        acc = acc + _dot(act, wout_ref[lo:lo + FFN_CHUNK, :])
    o_ref[...] = _layer_norm(ALPHA * x + 0.5 * acc, g_ref[...], b_ref[...])


def _ffn_ln(x, w_in, w_out, g, b, tm):
    t = x.shape[0]
    row = lambda i: (i, 0)
    return pl.pallas_call(
        _ffn_ln_body,
        grid=(t // tm,),
        in_specs=[pl.BlockSpec((tm, D_MODEL), row),
                  _const_spec(w_in.shape), _const_spec(w_out.shape),
                  _const_spec(g.shape), _const_spec(b.shape)],
        out_specs=pl.BlockSpec((tm, D_MODEL), row),
        out_shape=jax.ShapeDtypeStruct((t, D_MODEL), F32),
        compiler_params=_params("parallel"),
        name="ffn_ln",
    )(x, w_in, w_out, g, b)


def _in_proj_body(x_ref, wg_ref, wz_ref, wxbc_ref, wq_ref, wckv_ref, wqit_ref, wmisc_ref, wwit_ref, kvn_ref,
                  sg_ref, sz_ref, xbc_ref, q_ref, ckv_ref, qit_ref, ki_ref, dt_ref, wit_ref):
    xb = x_ref[...].astype(BF16)
    sg_ref[...] = jax.nn.sigmoid(_dot(xb, wg_ref[...])).astype(BF16)
    z = _dot(xb, wz_ref[...])
    sz_ref[...] = (z * jax.nn.sigmoid(z)).astype(BF16)
    xbc_ref[...] = _dot(xb, wxbc_ref[...]).astype(BF16)
    q_ref[...] = _dot(xb, wq_ref[...]).astype(BF16)
    c = _dot(xb, wckv_ref[...])
    c = c * lax.rsqrt(jnp.mean(c * c, axis=-1, keepdims=True) + RMS_EPS) * kvn_ref[...]
    ckv_ref[...] = c.astype(BF16)
    qit_ref[...] = _dot_nt(wqit_ref[...], xb).astype(BF16)
    wit_ref[...] = _dot_nt(wwit_ref[...], xb) * (IDX_HEADS ** -0.5)
    misc = _dot(xb, wmisc_ref[...])
    dt_ref[...] = misc[:, 0:SSD_HEADS]
    ki_ref[...] = misc[:, SSD_HEADS:SSD_HEADS + IDX_DIM].astype(BF16)


def _in_proj(x, wg, wz, wxbc, wq, wckv, wqit, wmisc, wwit, kvn, tm):
    t = x.shape[0]
    row = lambda i: (i, 0)
    col = lambda i: (0, i)
    out = [
        ((t, 2 * D_MODEL), BF16, (tm, 2 * D_MODEL), row),
        ((t, SSD_INNER), BF16, (tm, SSD_INNER), row),
        ((t, SSD_XBC), BF16, (tm, SSD_XBC), row),
        ((t, ATT_WIDTH), BF16, (tm, ATT_WIDTH), row),
        ((t, KV_RANK), BF16, (tm, KV_RANK), row),
        ((IDX_HEADS * IDX_DIM, t), BF16, (IDX_HEADS * IDX_DIM, tm), col),
        ((t, IDX_DIM), BF16, (tm, IDX_DIM), row),
        ((t, SSD_HEADS), F32, (tm, SSD_HEADS), row),
        ((IDX_HEADS, t), F32, (IDX_HEADS, tm), col),
    ]
    return pl.pallas_call(
        _in_proj_body,
        grid=(t // tm,),
        in_specs=[pl.BlockSpec((tm, D_MODEL), row)] + [_const_spec(w.shape) for w in
                                                        (wg, wz, wxbc, wq, wckv, wqit, wmisc, wwit, kvn)],
        out_specs=[pl.BlockSpec(blk, im) for _, _, blk, im in out],
        out_shape=[jax.ShapeDtypeStruct(shp, dt) for shp, dt, _, _ in out],
        compiler_params=_params("parallel"),
        name="in_proj",
    )(x, wg, wz, wxbc, wq, wckv, wqit, wmisc, wwit, kvn)


def _ssd_body(xbc_ref, sz_ref, dt_ref, cw_ref, cb_ref, dtb_ref, alog_ref, dskip_ref, nw_ref, expand_ref,
              y_ref, state_ref, tail_ref):
    chunk = xbc_ref.shape[0]
    hp = SSD_HEAD_DIM
    gw = SSD_INNER // SSD_GROUPS

    @pl.when(pl.program_id(1) == 0)
    def _():
        state_ref[...] = jnp.zeros(state_ref.shape, F32)
        tail_ref[...] = jnp.zeros(tail_ref.shape, F32)

    u = xbc_ref[...].astype(F32)
    ext = jnp.concatenate([tail_ref[...], u], axis=0)
    tail_ref[...] = u[chunk - CONV_TAIL:, :]
    conv = cb_ref[...] + jnp.zeros_like(u)
    for k in range(SSD_CONV):
        off = CONV_TAIL - (SSD_CONV - 1) + k
        conv = conv + cw_ref[k:k + 1, :] * ext[off:off + chunk, :]
    act = conv * jax.nn.sigmoid(conv)
    xs = act[:, :SSD_INNER]

    pre = dt_ref[...] + dtb_ref[...]
    dt = jnp.maximum(pre, 0.0) + jnp.log(1.0 + jnp.exp(-jnp.abs(pre)))
    a = dt * (-jnp.exp(alog_ref[...]))
    r_i = lax.broadcasted_iota(jnp.int32, (chunk, chunk), 0)
    c_i = lax.broadcasted_iota(jnp.int32, (chunk, chunk), 1)
    causal = r_i >= c_i
    a_cs = _dot_exact(causal.astype(F32), a)
    eye = (lax.broadcasted_iota(jnp.int32, (SSD_HEADS, SSD_HEADS), 0)
           == lax.broadcasted_iota(jnp.int32, (SSD_HEADS, SSD_HEADS), 1)).astype(F32)
    a_cs_t = _dot_exact(eye, a_cs, NT_DIMS)
    expand = expand_ref[...]
    dt_e = _dot_exact(dt, expand)
    a_cs_e = _dot_exact(a_cs, expand)
    a_end_e = a_cs_e[chunk - 1:chunk, :]

    xdt = xs * dt_e
    xdt_b = xdt.astype(BF16)
    xdt_end_b = (xdt * jnp.exp(a_end_e - a_cs_e)).astype(BF16)
    state = state_ref[...]
    state_b = state.astype(BF16)

    y_diag, y_off, new_state = [], [], []
    for g in range(SSD_GROUPS):
        bm = act[:, SSD_INNER + g * SSD_STATE:SSD_INNER + (g + 1) * SSD_STATE].astype(BF16)
        cm = act[:, SSD_INNER + SSD_GN + g * SSD_STATE:SSD_INNER + SSD_GN + (g + 1) * SSD_STATE].astype(BF16)
        cb = _dot_nt(cm, bm)
        y_off.append(_dot(cm, state_b[:, g * gw:(g + 1) * gw]))
        new_state.append(lax.dot_general(bm, xdt_end_b[:, g * gw:(g + 1) * gw], TN_DIMS,
                                         preferred_element_type=F32))
        for r in range(SSD_HEADS // SSD_GROUPS):
            h = g * (SSD_HEADS // SSD_GROUPS) + r
            seg = a_cs[:, h:h + 1] - a_cs_t[h:h + 1, :]
            decay = jnp.exp(jnp.where(causal, seg, NEG_BIG))
            y_diag.append(_dot((cb * decay).astype(BF16), xdt_b[:, h * hp:(h + 1) * hp]))
    state_ref[...] = jnp.exp(a_end_e) * state + jnp.concatenate(new_state, axis=1)

    y = jnp.concatenate(y_diag, axis=1) + jnp.concatenate(y_off, axis=1) * jnp.exp(a_cs_e)
    y = (y + dskip_ref[...] * xs) * sz_ref[...].astype(F32)
    outs = []
    for g in range(SSD_GROUPS):
        yg = y[:, g * gw:(g + 1) * gw]
        outs.append(yg * lax.rsqrt(jnp.mean(yg * yg, axis=-1, keepdims=True) + RMS_EPS))
    y_ref[...] = (jnp.concatenate(outs, axis=1) * nw_ref[...]).astype(BF16)


def _ssd(xbc, sz, dt, conv_w, conv_b, dt_bias, a_log, d_skip_e, norm_w, expand, bsz, seqlen):
    t = xbc.shape[0]
    nc = seqlen // SSD_CHUNK
    row = lambda b, c: (b * nc + c, 0)
    return pl.pallas_call(
        _ssd_body,
        grid=(bsz, nc),
        in_specs=[pl.BlockSpec((SSD_CHUNK, SSD_XBC), row), pl.BlockSpec((SSD_CHUNK, SSD_INNER), row),
                  pl.BlockSpec((SSD_CHUNK, SSD_HEADS), row)]
                 + [_const_spec(w.shape) for w in (conv_w, conv_b, dt_bias, a_log, d_skip_e, norm_w, expand)],
        out_specs=pl.BlockSpec((SSD_CHUNK, SSD_INNER), row),
        out_shape=jax.ShapeDtypeStruct((t, SSD_INNER), BF16),
        scratch_shapes=[pltpu.VMEM((SSD_STATE, SSD_INNER), F32), pltpu.VMEM((CONV_TAIL, SSD_XBC), F32)],
        compiler_params=_params("arbitrary", "arbitrary"),
        name="ssd",
    )(xbc, sz, dt, conv_w, conv_b, dt_bias, a_log, d_skip_e, norm_w, expand)


def _score_of_key(key):
    return pltpu.bitcast(jnp.where(key >= 0, key, key ^ INT_MAX), F32)


def _dsa_body(slopes_ref, q_ref, qit_ref, wit_ref, ckv_ref, ki_ref, wuk_ref, wuv_ref, y_ref,
              sc_ref, bias_ref, ql_ref, s_ref, p_ref, m_ref, l_ref, alpha_ref, acc_ref, *, topk, pos_bits):
    qb = q_ref.shape[0]
    kc = KEY_CHUNK
    t0 = pl.program_id(1) * qb
    n_chunks = (t0 + qb + kc - 1) // kc
    q_pos = t0 + lax.broadcasted_iota(jnp.int32, (1, qb), 1)

    def key_pos(c):
        return c * kc + lax.broadcasted_iota(jnp.int32, (kc, qb), 0)

    def chunk_ds(c):
        return pl.ds(pl.multiple_of(c * kc, kc), kc)

    def score_chunk(c, carry):
        kblk = ki_ref[chunk_ds(c), :]
        score = jnp.zeros((kc, qb), F32)
        for h in range(0, IDX_HEADS, 2):
            w = jnp.concatenate([qit_ref[h * IDX_DIM:(h + 1) * IDX_DIM, :],
                                 qit_ref[(h + 1) * IDX_DIM:(h + 2) * IDX_DIM, :]], axis=1)
            wi = jnp.concatenate([wit_ref[h:h + 1, :], wit_ref[h + 1:h + 2, :]], axis=1)
            t = jnp.maximum(_dot(kblk, w), 0.0) * wi
            score = score + t[:, :qb] + t[:, qb:]
        sc_ref[chunk_ds(c), :] = jnp.where(key_pos(c) <= q_pos, score, -jnp.inf)
        return carry

    lax.fori_loop(0, n_chunks, score_chunk, 0)

    def count(pred):
        def body(c, part):
            hit = jnp.where(pred(sc_ref[chunk_ds(c), :], key_pos(c)), 1, 0)
            return part + jnp.sum(hit.reshape(kc // SUBLANES, SUBLANES, qb), axis=0)
        part = lax.fori_loop(0, n_chunks, body, jnp.zeros((SUBLANES, qb), jnp.int32))
        return jnp.sum(part, axis=0, keepdims=True)

    def thr_bit(i, state):
        thr, n_ge = state
        cand = thr + (jnp.int32(1) << (31 - i))
        cand_f = _score_of_key(cand)
        n = count(lambda s, p: s >= cand_f)
        return jnp.where(n >= topk, cand, thr), jnp.where(n >= topk, n, n_ge)

    def pending(state):
        thr, n_ge = state
        settled = (n_ge < topk) | ((n_ge == topk) & (thr > INT_MIN))
        return jnp.max(jnp.where(settled, 0, 1))

    state = (jnp.full((1, qb), INT_MIN, jnp.int32), count(lambda s, p: s > -jnp.inf))
    state = lax.fori_loop(0, THR_BITS_ALWAYS, thr_bit, state)

    def more_bits(carry):
        i, state, _ = carry
        for b in range(THR_BITS_STEP):
            state = thr_bit(i + b, state)
        return i + THR_BITS_STEP, state, pending(state)

    _, (thr, _), _ = lax.while_loop(lambda carry: (carry[0] < 32) & (carry[2] > 0), more_bits,
                                    (jnp.int32(THR_BITS_ALWAYS), state, pending(state)))
    thr_f = jnp.where(thr >= LOWEST_FINITE_KEY, _score_of_key(jnp.maximum(thr, LOWEST_FINITE_KEY)), -jnp.inf)

    need = topk - count(lambda s, p: s > thr_f)
    n_tie = count(lambda s, p: (s == thr_f) & (p <= q_pos))

    def tie_cut():
        def cut_bit(i, cut):
            cand = cut + (jnp.int32(1) << (pos_bits - 1 - i))
            n_before = count(lambda s, p: (s == thr_f) & (p <= q_pos) & (p < cand))
            return jnp.where(n_before < need, cand, cut)
        return lax.fori_loop(0, pos_bits, cut_bit, jnp.zeros((1, qb), jnp.int32))

    eye = (lax.broadcasted_iota(jnp.int32, (qb, qb), 0)
           == lax.broadcasted_iota(jnp.int32, (qb, qb), 1)).astype(BF16)

    def write_bias(selected):
        def bias_chunk(c, carry):
            p = key_pos(c)
            sel = (p <= q_pos) & selected(sc_ref[chunk_ds(c), :], p)
            bias_ref[:, chunk_ds(c)] = _dot_nt(eye, jnp.where(sel, 0.0, NEG_BIG).astype(BF16))
            return carry
        lax.fori_loop(0, n_chunks, bias_chunk, 0)

    def bias_with_ties():
        cut = tie_cut()
        write_bias(lambda s, p: (s > thr_f) | ((s == thr_f) & (p <= cut)))

    def bias_no_ties():
        write_bias(lambda s, p: s >= thr_f)

    lax.cond(jnp.max(jnp.where(n_tie > need, 1, 0)) > 0, bias_with_ties, bias_no_ties)

    for h in range(ATT_HEADS):
        ql = _dot(q_ref[:, h * ATT_HEAD_DIM:(h + 1) * ATT_HEAD_DIM], wuk_ref[h])
        ql_ref[h * qb:(h + 1) * qb, :] = (ql * (ATT_HEAD_DIM ** -0.5 * LOG2E)).astype(BF16)
    m_ref[...] = jnp.full(m_ref.shape, NEG_BIG, F32)
    l_ref[...] = jnp.zeros(l_ref.shape, F32)
    acc_ref[...] = jnp.zeros(acc_ref.shape, F32)
    lane_tiles = kc // LANES
    heads_per_part = ATT_HEADS // MXU_SPLIT

    part_rows = [slice(r * heads_per_part * qb, (r + 1) * heads_per_part * qb) for r in range(MXU_SPLIT)]

    def logits(part, c):
        cblk = ckv_ref[chunk_ds(c), :]
        rel = (c * kc - t0 + lax.broadcasted_iota(jnp.int32, (1, kc), 1)).astype(F32)
        s_part = _dot_nt(ql_ref[part_rows[part], :], cblk)
        for i_h in range(heads_per_part):
            h = part * heads_per_part + i_h
            rows = slice(h * qb, (h + 1) * qb)
            s = s_part[i_h * qb:(i_h + 1) * qb, :] + (slopes_ref[h] * LOG2E) * rel + bias_ref[:, chunk_ds(c)]
            s_ref[rows, :] = s
            mx = s[:, 0:LANES]
            for i in range(1, lane_tiles):
                mx = jnp.maximum(mx, s[:, i * LANES:(i + 1) * LANES])
            m_old = m_ref[rows, :]
            m_new = jnp.maximum(m_old, jnp.max(mx, axis=-1, keepdims=True))
            m_ref[rows, :] = m_new
            alpha_ref[rows, :] = jnp.exp2(m_old - m_new)

    def values(part, c):
        cblk = ckv_ref[chunk_ds(c), :]
        for i_h in range(heads_per_part):
            rows = slice((part * heads_per_part + i_h) * qb, (part * heads_per_part + i_h + 1) * qb)
            m = m_ref[rows, :]
            l = alpha_ref[rows, :] * l_ref[rows, :]
            for i in range(lane_tiles):
                p = jnp.exp2(s_ref[rows, i * LANES:(i + 1) * LANES] - m)
                l = l + p
                p_ref[rows, i * LANES:(i + 1) * LANES] = p.astype(BF16)
            l_ref[rows, :] = l
        upd = _dot(p_ref[part_rows[part], :], cblk)
        for i_h in range(heads_per_part):
            rows = slice((part * heads_per_part + i_h) * qb, (part * heads_per_part + i_h + 1) * qb)
            alpha = alpha_ref[rows, :]
            for i in range(KV_RANK // LANES):
                cols = slice(i * LANES, (i + 1) * LANES)
                acc_ref[rows, cols] = alpha * acc_ref[rows, cols] + upd[i_h * qb:(i_h + 1) * qb, cols]

    last = n_chunks - 1
    logits(0, 0)
    logits(1, 0)

    def att_chunk(c, carry):
        values(0, c)
        logits(0, c + 1)
        values(1, c)
        logits(1, c + 1)
        return carry

    lax.fori_loop(0, last, att_chunk, 0)
    values(0, last)
    values(1, last)

    o = (acc_ref[...] / jnp.sum(l_ref[...], axis=-1, keepdims=True)).astype(BF16)
    y_ref[...] = jnp.concatenate([_dot(o[h * qb:(h + 1) * qb, :], wuv_ref[h]) for h in range(ATT_HEADS)],
                                 axis=1).astype(BF16)


def _dsa(q, qit, wit, ckv, ki, w_uk, w_uv, slopes, bsz, seqlen):
    t = q.shape[0]
    qb = Q_BLOCK
    nq = seqlen // qb
    rows = ATT_HEADS * qb
    topk = min(TOPK_MAX, seqlen // 4)
    pos_bits = max(1, (seqlen - 1).bit_length())
    seq_pad = -(-seqlen // KEY_CHUNK) * KEY_CHUNK
    row = lambda b, j: (b * nq + j, 0)
    col = lambda b, j: (0, b * nq + j)
    per_batch = lambda b, j: (b, 0, 0)
    if seq_pad != seqlen:
        ckv = jnp.pad(ckv.reshape(bsz, seqlen, KV_RANK), ((0, 0), (0, seq_pad - seqlen), (0, 0)))
        ki = jnp.pad(ki.reshape(bsz, seqlen, IDX_DIM), ((0, 0), (0, seq_pad - seqlen), (0, 0)))
    body = functools.partial(_dsa_body, topk=topk, pos_bits=pos_bits)
    return pl.pallas_call(
        body,
        grid=(bsz, nq),
        in_specs=[pl.BlockSpec(memory_space=pltpu.SMEM),
                  pl.BlockSpec((qb, ATT_WIDTH), row), pl.BlockSpec((IDX_HEADS * IDX_DIM, qb), col),
                  pl.BlockSpec((IDX_HEADS, qb), col),
                  pl.BlockSpec((None, seq_pad, KV_RANK), per_batch),
                  pl.BlockSpec((None, seq_pad, IDX_DIM), per_batch),
                  _const_spec(w_uk.shape), _const_spec(w_uv.shape)],
        out_specs=pl.BlockSpec((qb, ATT_WIDTH), row),
        out_shape=jax.ShapeDtypeStruct((t, ATT_WIDTH), BF16),
        scratch_shapes=[pltpu.VMEM((seq_pad, qb), F32), pltpu.VMEM((qb, seq_pad), F32),
                        pltpu.VMEM((rows, KV_RANK), BF16),
                        pltpu.VMEM((rows, KEY_CHUNK), F32), pltpu.VMEM((rows, KEY_CHUNK), BF16),
                        pltpu.VMEM((rows, LANES), F32), pltpu.VMEM((rows, LANES), F32),
                        pltpu.VMEM((rows, LANES), F32), pltpu.VMEM((rows, KV_RANK), F32)],
        compiler_params=_params("arbitrary", "arbitrary"),
        name="dsa",
    )(slopes, q, qit, wit, ckv.reshape(bsz, seq_pad, KV_RANK), ki.reshape(bsz, seq_pad, IDX_DIM), w_uk, w_uv)


def _merge_ln_body(x_ref, ys_ref, ya_ref, sg_ref, wps_ref, wpa_ref, wo_ref, g_ref, b_ref, o_ref):
    sg = sg_ref[...].astype(F32)
    merged = (sg[:, :D_MODEL] * _dot(ys_ref[...], wps_ref[...])
              + sg[:, D_MODEL:] * _dot(ya_ref[...], wpa_ref[...]))
    out = _dot(merged.astype(BF16), wo_ref[...])
    o_ref[...] = _layer_norm(ALPHA * x_ref[...] + out, g_ref[...], b_ref[...])


def _merge_ln(x, ys, ya, sg, wps, wpa, wo, g, b, tm):
    t = x.shape[0]
    row = lambda i: (i, 0)
    return pl.pallas_call(
        _merge_ln_body,
        grid=(t // tm,),
        in_specs=[pl.BlockSpec((tm, D_MODEL), row), pl.BlockSpec((tm, SSD_INNER), row),
                  pl.BlockSpec((tm, ATT_WIDTH), row), pl.BlockSpec((tm, 2 * D_MODEL), row)]
                 + [_const_spec(w.shape) for w in (wps, wpa, wo, g, b)],
        out_specs=pl.BlockSpec((tm, D_MODEL), row),
        out_shape=jax.ShapeDtypeStruct((t, D_MODEL), F32),
        compiler_params=_params("parallel"),
        name="merge_ln",
    )(x, ys, ya, sg, wps, wpa, wo, g, b)


def _mem_kv_body(m_ref, w_ref, o_ref):
    o_ref[...] = _dot(m_ref[...].astype(BF16), w_ref[...]).astype(BF16)


def _mem_kv(mem, w_mkv, bsz, mem_len):
    row = lambda b: (b, 0)
    return pl.pallas_call(
        _mem_kv_body,
        grid=(bsz,),
        in_specs=[pl.BlockSpec((mem_len, D_MODEL), row), _const_spec(w_mkv.shape)],
        out_specs=pl.BlockSpec((mem_len, 2 * D_MODEL), row),
        out_shape=jax.ShapeDtypeStruct((bsz * mem_len, 2 * D_MODEL), BF16),
        compiler_params=_params("parallel"),
        name="mem_kv",
    )(mem, w_mkv)


def _xattn_ln_body(x_ref, kv_ref, wq_ref, wo_ref, g_ref, b_ref, o_ref):
    x = x_ref[...]
    q = (_dot(x.astype(BF16), wq_ref[...]) * (MEM_HEAD_DIM ** -0.5)).astype(BF16)
    heads = []
    for h in range(MEM_HEADS):
        lo = h * MEM_HEAD_DIM
        s = _dot_nt(q[:, lo:lo + MEM_HEAD_DIM], kv_ref[:, lo:lo + MEM_HEAD_DIM])
        p = jnp.exp(s - jnp.max(s, axis=-1, keepdims=True))
        o = _dot(p.astype(BF16), kv_ref[:, D_MODEL + lo:D_MODEL + lo + MEM_HEAD_DIM])
        heads.append((o / jnp.sum(p, axis=-1, keepdims=True)).astype(BF16))
    out = _dot(jnp.concatenate(heads, axis=1), wo_ref[...])
    o_ref[...] = _layer_norm(ALPHA * x + out, g_ref[...], b_ref[...])


def _xattn_ln(x, kv, wq, wo, g, b, bsz, seqlen, mem_len, tm):
    t = x.shape[0]
    nt = seqlen // tm
    return pl.pallas_call(
        _xattn_ln_body,
        grid=(bsz, nt),
        in_specs=[pl.BlockSpec((tm, D_MODEL), lambda b_, i: (b_ * nt + i, 0)),
                  pl.BlockSpec((mem_len, 2 * D_MODEL), lambda b_, i: (b_, 0))]
                 + [_const_spec(w.shape) for w in (wq, wo, g, b)],
        out_specs=pl.BlockSpec((tm, D_MODEL), lambda b_, i: (b_ * nt + i, 0)),
        out_shape=jax.ShapeDtypeStruct((t, D_MODEL), F32),
        compiler_params=_params("parallel", "parallel"),
        name="xattn_ln",
    )(x, kv, wq, wo, g, b)


def _split_w_in(w_in):
    sizes = (D_MODEL, D_MODEL, SSD_INNER, SSD_XBC, SSD_HEADS, ATT_WIDTH, KV_RANK, IDX_HEADS * IDX_DIM,
             IDX_DIM, IDX_HEADS)
    parts, off = [], 0
    for s in sizes:
        parts.append(w_in[:, off:off + s])
        off += s
    g_ssd, g_att, z, xbc, dt, q, ckv, qi, ki, wi = parts
    misc = jnp.concatenate([dt, ki, jnp.zeros((w_in.shape[0], LANES - SSD_HEADS - IDX_DIM), w_in.dtype)], axis=1)
    return [w.astype(BF16) for w in (jnp.concatenate([g_ssd, g_att], axis=1), z, xbc, q, ckv, qi.T, misc, wi.T)]


def _layer(x, mem, bsz, seqlen, mem_len, p):
    (ffn1_w_in, ffn1_w_out, ln1_g, ln1_b, w_in, conv_w, conv_b, dt_bias, a_log, d_skip, ssd_norm_w,
     kv_norm_w, w_uk, w_uv, w_proj_ssd, w_proj_att, w_out, ln2_g, ln2_b,
     w_mq, w_mkv, w_mo, ln3_g, ln3_b, ffn2_w_in, ffn2_w_out, ln4_g, ln4_b) = p
    tm = min(TOKEN_TILE, seqlen)
    vec = lambda v: v.reshape(1, -1)
    bf = lambda w: w.astype(BF16)

    x = _ffn_ln(x, bf(ffn1_w_in), bf(ffn1_w_out), vec(ln1_g), vec(ln1_b), tm)

    sg, sz, xbc, q, ckv, qit, ki, dt, wit = _in_proj(x, *_split_w_in(w_in), vec(kv_norm_w), tm)

    expand = jnp.repeat(jnp.eye(SSD_HEADS, dtype=F32), SSD_HEAD_DIM, axis=1)
    y_ssd = _ssd(xbc, sz, dt, conv_w, vec(conv_b), vec(dt_bias), vec(a_log),
                 vec(jnp.repeat(d_skip, SSD_HEAD_DIM)), vec(ssd_norm_w), expand, bsz, seqlen)

    slopes = 2.0 ** (-8.0 * jnp.arange(1, ATT_HEADS + 1, dtype=F32) / ATT_HEADS)
    y_att = _dsa(q, qit, wit, ckv, ki, bf(w_uk), bf(w_uv), slopes, bsz, seqlen)

    x = _merge_ln(x, y_ssd, y_att, sg, bf(w_proj_ssd), bf(w_proj_att), bf(w_out), vec(ln2_g), vec(ln2_b), tm)

    kv = _mem_kv(mem, bf(w_mkv), bsz, mem_len)
    x = _xattn_ln(x, kv, bf(w_mq), bf(w_mo), vec(ln3_g), vec(ln3_b), bsz, seqlen, mem_len, tm)

    return _ffn_ln(x, bf(ffn2_w_in), bf(ffn2_w_out), vec(ln4_g), vec(ln4_b), tm)


def kernel(x, mem, ffn1_w_in, ffn1_w_out, ln1_g, ln1_b, w_in, conv_w, conv_b, dt_bias, a_log, d_skip,
           ssd_norm_w, kv_norm_w, w_uk, w_uv, w_proj_ssd, w_proj_att, w_out, ln2_g, ln2_b,
           w_mq, w_mkv, w_mo, ln3_g, ln3_b, ffn2_w_in, ffn2_w_out, ln4_g, ln4_b):
    bsz, seqlen, d = x.shape
    mem_len = mem.shape[1]
    params = (ffn1_w_in, ffn1_w_out, ln1_g, ln1_b, w_in, conv_w, conv_b, dt_bias, a_log, d_skip, ssd_norm_w,
              kv_norm_w, w_uk, w_uv, w_proj_ssd, w_proj_att, w_out, ln2_g, ln2_b,
              w_mq, w_mkv, w_mo, ln3_g, ln3_b, ffn2_w_in, ffn2_w_out, ln4_g, ln4_b)
    h = x.reshape(bsz * seqlen, d)
    m = mem.reshape(bsz * mem_len, d)
    for layer in range(ffn1_w_in.shape[0]):
        h = _layer(h, m, bsz, seqlen, mem_len, tuple(w[layer] for w in params))
    return h.reshape(bsz, seqlen, d)
```

```python
import functools
import math

import jax
import jax.numpy as jnp
from jax import lax
from jax.experimental import pallas as pl
from jax.experimental.pallas import tpu as pltpu

F32 = jnp.float32
BF16 = jnp.bfloat16

D_MODEL = 1024
DEPTH = 1
SSD_HEADS = 16
SSD_HEAD_DIM = 64
SSD_INNER = SSD_HEADS * SSD_HEAD_DIM
SSD_GROUPS = 4
SSD_STATE = 128
SSD_CONV = 4
SSD_CHUNK = 128
SSD_GN = SSD_GROUPS * SSD_STATE
SSD_XBC = SSD_INNER + 2 * SSD_GN
ATT_HEADS = 16
ATT_HEAD_DIM = 64
ATT_WIDTH = ATT_HEADS * ATT_HEAD_DIM
KV_RANK = 256
IDX_HEADS = 16
IDX_DIM = 64
TOPK_MAX = 256
Q_BLOCK = 256
MEM_HEADS = 4
MEM_HEAD_DIM = D_MODEL // MEM_HEADS
D_FF = 2816
ALPHA = (2.0 * DEPTH) ** 0.25
LN_EPS = 1e-5
RMS_EPS = 1e-6

INT_MIN = -(2 ** 31)
INT_MAX = 2 ** 31 - 1
LOWEST_FINITE_KEY = INT_MIN + 2 ** 23
NEG_BIG = -1e30
LOG2E = math.log2(math.e)

LANES = 128
SUBLANES = 8
TOKEN_TILE = 512
FFN_CHUNK = 1408
KEY_CHUNK = 512
THR_BITS_ALWAYS = 24
THR_BITS_STEP = 2
MXU_SPLIT = 2
CONV_TAIL = 8
VMEM_LIMIT = 56 * 1024 * 1024

NT_DIMS = (((1,), (1,)), ((), ()))
TN_DIMS = (((0,), (0,)), ((), ()))


def _const_spec(shape):
    return pl.BlockSpec(shape, lambda *_: (0,) * len(shape), pipeline_mode=pl.Buffered(1))


def _params(*sem):
    return pltpu.CompilerParams(dimension_semantics=sem, vmem_limit_bytes=VMEM_LIMIT)


def _layer_norm(y, g, b):
    mu = jnp.mean(y, axis=-1, keepdims=True)
    d = y - mu
    var = jnp.mean(d * d, axis=-1, keepdims=True)
    return d * lax.rsqrt(var + LN_EPS) * g + b


def _dot(a, b):
    return jnp.dot(a, b, preferred_element_type=F32)


def _dot_nt(a, b):
    return lax.dot_general(a, b, NT_DIMS, preferred_element_type=F32)


def _split3(x):
    hi = x.astype(BF16)
    rest = x - hi.astype(F32)
    mid = rest.astype(BF16)
    return hi, mid, (rest - mid.astype(F32)).astype(BF16)


def _select_sum(sel01, pieces, lhs_is_01):
    if lhs_is_01:
        return sum(_dot(sel01, p) for p in pieces)
    return sum(_dot(p, sel01) for p in pieces)


def _ffn_ln_body(x_ref, win_ref, wout_ref, g_ref, b_ref, o_ref):
    x = x_ref[...]
    xb = x.astype(BF16)
    acc = jnp.zeros(x.shape, F32)
    for c in range(D_FF // FFN_CHUNK):
        lo = c * FFN_CHUNK
        gate = _dot(xb, win_ref[:, lo:lo + FFN_CHUNK])
        up = _dot(xb, win_ref[:, D_FF + lo:D_FF + lo + FFN_CHUNK])
        act = (gate * jax.nn.sigmoid(gate) * up).astype(BF16)
        acc = acc + _dot(act, wout_ref[lo:lo + FFN_CHUNK, :])
    o_ref[...] = _layer_norm(ALPHA * x + 0.5 * acc, g_ref[...], b_ref[...])


def _ffn_ln(x, w_in, w_out, g, b, tm):
    t = x.shape[0]
    row = lambda i: (i, 0)
    return pl.pallas_call(
        _ffn_ln_body,
        grid=(t // tm,),
        in_specs=[pl.BlockSpec((tm, D_MODEL), row),
                  _const_spec(w_in.shape), _const_spec(w_out.shape),
                  _const_spec(g.shape), _const_spec(b.shape)],
        out_specs=pl.BlockSpec((tm, D_MODEL), row),
        out_shape=jax.ShapeDtypeStruct((t, D_MODEL), F32),
        compiler_params=_params("parallel"),
        name="ffn_ln",
    )(x, w_in, w_out, g, b)


def _in_proj_body(x_ref, wg_ref, wz_ref, wxbc_ref, wq_ref, wckv_ref, wqit_ref, wmisc_ref, wwit_ref, kvn_ref,
                  sg_ref, sz_ref, xbc_ref, q_ref, ckv_ref, qit_ref, ki_ref, dt_ref, wit_ref):
    xb = x_ref[...].astype(BF16)
    sg_ref[...] = jax.nn.sigmoid(_dot(xb, wg_ref[...])).astype(BF16)
    z = _dot(xb, wz_ref[...])
    sz_ref[...] = (z * jax.nn.sigmoid(z)).astype(BF16)
    xbc_ref[...] = _dot(xb, wxbc_ref[...]).astype(BF16)
    q_ref[...] = _dot(xb, wq_ref[...]).astype(BF16)
    c = _dot(xb, wckv_ref[...])
    c = c * lax.rsqrt(jnp.mean(c * c, axis=-1, keepdims=True) + RMS_EPS) * kvn_ref[...]
    ckv_ref[...] = c.astype(BF16)
    qit_ref[...] = _dot_nt(wqit_ref[...], xb).astype(BF16)
    wit_ref[...] = _dot_nt(wwit_ref[...], xb) * (IDX_HEADS ** -0.5)
    misc = _dot(xb, wmisc_ref[...])
    dt_ref[...] = misc[:, 0:SSD_HEADS]
    ki_ref[...] = misc[:, SSD_HEADS:SSD_HEADS + IDX_DIM].astype(BF16)


def _in_proj(x, wg, wz, wxbc, wq, wckv, wqit, wmisc, wwit, kvn, tm):
    t = x.shape[0]
    row = lambda i: (i, 0)
    col = lambda i: (0, i)
    out = [
        ((t, 2 * D_MODEL), BF16, (tm, 2 * D_MODEL), row),
        ((t, SSD_INNER), BF16, (tm, SSD_INNER), row),
        ((t, SSD_XBC), BF16, (tm, SSD_XBC), row),
        ((t, ATT_WIDTH), BF16, (tm, ATT_WIDTH), row),
        ((t, KV_RANK), BF16, (tm, KV_RANK), row),
        ((IDX_HEADS * IDX_DIM, t), BF16, (IDX_HEADS * IDX_DIM, tm), col),
        ((t, IDX_DIM), BF16, (tm, IDX_DIM), row),
        ((t, SSD_HEADS), F32, (tm, SSD_HEADS), row),
        ((IDX_HEADS, t), F32, (IDX_HEADS, tm), col),
    ]
    return pl.pallas_call(
        _in_proj_body,
        grid=(t // tm,),
        in_specs=[pl.BlockSpec((tm, D_MODEL), row)] + [_const_spec(w.shape) for w in
                                                        (wg, wz, wxbc, wq, wckv, wqit, wmisc, wwit, kvn)],
        out_specs=[pl.BlockSpec(blk, im) for _, _, blk, im in out],
        out_shape=[jax.ShapeDtypeStruct(shp, dt) for shp, dt, _, _ in out],
        compiler_params=_params("parallel"),
        name="in_proj",
    )(x, wg, wz, wxbc, wq, wckv, wqit, wmisc, wwit, kvn)


def _ssd_body(xbc_ref, sz_ref, dt_ref, cw_ref, cb_ref, dtb_ref, alog_ref, dskip_ref, nw_ref, expand_ref,
              y_ref, state_ref, ext_ref):
    chunk = xbc_ref.shape[0]
    hp = SSD_HEAD_DIM
    gw = SSD_INNER // SSD_GROUPS

    @pl.when(pl.program_id(1) == 0)
    def _():
        state_ref[...] = jnp.zeros(state_ref.shape, F32)
        ext_ref[0:CONV_TAIL, :] = jnp.zeros((CONV_TAIL, SSD_XBC), F32)

    ext_ref[CONV_TAIL:, :] = xbc_ref[...].astype(F32)
    conv = cb_ref[...] + cw_ref[SSD_CONV - 1:SSD_CONV, :] * ext_ref[CONV_TAIL:, :]
    for k in range(SSD_CONV - 1):
        off = CONV_TAIL - (SSD_CONV - 1) + k
        conv = conv + cw_ref[k:k + 1, :] * ext_ref[off:off + chunk, :]
    ext_ref[0:CONV_TAIL, :] = ext_ref[chunk:chunk + CONV_TAIL, :]
    act = conv * jax.nn.sigmoid(conv)
    xs = act[:, :SSD_INNER]

    pre = dt_ref[...] + dtb_ref[...]
    dt = jnp.maximum(pre, 0.0) + jnp.log(1.0 + jnp.exp(-jnp.abs(pre)))
    a = dt * (-jnp.exp(alog_ref[...]))
    r_i = lax.broadcasted_iota(jnp.int32, (chunk, chunk), 0)
    c_i = lax.broadcasted_iota(jnp.int32, (chunk, chunk), 1)
    causal = r_i >= c_i
    a_cs = _select_sum(causal.astype(BF16), _split3(a), lhs_is_01=True)
    eye = (lax.broadcasted_iota(jnp.int32, (SSD_HEADS, SSD_HEADS), 0)
           == lax.broadcasted_iota(jnp.int32, (SSD_HEADS, SSD_HEADS), 1)).astype(BF16)
    a_cs_3 = _split3(a_cs)
    a_cs_t = sum(_dot_nt(eye, piece) for piece in a_cs_3)
    expand = expand_ref[...]
    dt_e = _select_sum(expand, _split3(dt), lhs_is_01=False)
    a_cs_e = _select_sum(expand, a_cs_3, lhs_is_01=False)
    a_end_e = a_cs_e[chunk - 1:chunk, :]

    xdt = xs * dt_e
    xdt_b = xdt.astype(BF16)
    xdt_end_b = (xdt * jnp.exp(a_end_e - a_cs_e)).astype(BF16)
    state = state_ref[...]
    state_b = state.astype(BF16)

    y_diag, y_off, new_state = [], [], []
    for g in range(SSD_GROUPS):
        bm = act[:, SSD_INNER + g * SSD_STATE:SSD_INNER + (g + 1) * SSD_STATE].astype(BF16)
        cm = act[:, SSD_INNER + SSD_GN + g * SSD_STATE:SSD_INNER + SSD_GN + (g + 1) * SSD_STATE].astype(BF16)
        cb = _dot_nt(cm, bm)
        y_off.append(_dot(cm, state_b[:, g * gw:(g + 1) * gw]))
        new_state.append(lax.dot_general(bm, xdt_end_b[:, g * gw:(g + 1) * gw], TN_DIMS,
                                         preferred_element_type=F32))
        for r in range(SSD_HEADS // SSD_GROUPS):
            h = g * (SSD_HEADS // SSD_GROUPS) + r
            seg = a_cs[:, h:h + 1] - a_cs_t[h:h + 1, :]
            decay = jnp.exp(jnp.where(causal, seg, NEG_BIG))
            y_diag.append(_dot((cb * decay).astype(BF16), xdt_b[:, h * hp:(h + 1) * hp]))
    state_ref[...] = jnp.exp(a_end_e) * state + jnp.concatenate(new_state, axis=1)

    y = jnp.concatenate(y_diag, axis=1) + jnp.concatenate(y_off, axis=1) * jnp.exp(a_cs_e)
    y = (y + dskip_ref[...] * xs) * sz_ref[...].astype(F32)
    outs = []
    for g in range(SSD_GROUPS):
        yg = y[:, g * gw:(g + 1) * gw]
        outs.append(yg * lax.rsqrt(jnp.mean(yg * yg, axis=-1, keepdims=True) + RMS_EPS))
    y_ref[...] = (jnp.concatenate(outs, axis=1) * nw_ref[...]).astype(BF16)


def _ssd(xbc, sz, dt, conv_w, conv_b, dt_bias, a_log, d_skip_e, norm_w, expand, bsz, seqlen):
    t = xbc.shape[0]
    nc = seqlen // SSD_CHUNK
    row = lambda b, c: (b * nc + c, 0)
    return pl.pallas_call(
        _ssd_body,
        grid=(bsz, nc),
        in_specs=[pl.BlockSpec((SSD_CHUNK, SSD_XBC), row), pl.BlockSpec((SSD_CHUNK, SSD_INNER), row),
                  pl.BlockSpec((SSD_CHUNK, SSD_HEADS), row)]
                 + [_const_spec(w.shape) for w in (conv_w, conv_b, dt_bias, a_log, d_skip_e, norm_w, expand)],
        out_specs=pl.BlockSpec((SSD_CHUNK, SSD_INNER), row),
        out_shape=jax.ShapeDtypeStruct((t, SSD_INNER), BF16),
        scratch_shapes=[pltpu.VMEM((SSD_STATE, SSD_INNER), F32),
                        pltpu.VMEM((CONV_TAIL + SSD_CHUNK, SSD_XBC), F32)],
        compiler_params=_params("arbitrary", "arbitrary"),
        name="ssd",
    )(xbc, sz, dt, conv_w, conv_b, dt_bias, a_log, d_skip_e, norm_w, expand)


def _score_of_key(key):
    return pltpu.bitcast(jnp.where(key >= 0, key, key ^ INT_MAX), F32)


def _dsa_body(slopes_ref, q_ref, qit_ref, wit_ref, ckv_ref, ki_ref, wuk_ref, wuv_ref, y_ref,
              sc_ref, bias_ref, ql_ref, s_ref, p_ref, m_ref, l_ref, alpha_ref, acc_ref, *, topk, pos_bits):
    qb = q_ref.shape[0]
    kc = KEY_CHUNK
    t0 = pl.program_id(1) * qb
    n_chunks = (t0 + qb + kc - 1) // kc
    q_pos = t0 + lax.broadcasted_iota(jnp.int32, (1, qb), 1)

    def key_pos(c):
        return c * kc + lax.broadcasted_iota(jnp.int32, (kc, qb), 0)

    def chunk_ds(c):
        return pl.ds(pl.multiple_of(c * kc, kc), kc)

    def score_chunk(c, carry):
        kblk = ki_ref[chunk_ds(c), :]
        score = jnp.zeros((kc, qb), F32)
        for h in range(0, IDX_HEADS, 2):
            w = jnp.concatenate([qit_ref[h * IDX_DIM:(h + 1) * IDX_DIM, :],
                                 qit_ref[(h + 1) * IDX_DIM:(h + 2) * IDX_DIM, :]], axis=1)
            wi = jnp.concatenate([wit_ref[h:h + 1, :], wit_ref[h + 1:h + 2, :]], axis=1)
            t = jnp.maximum(_dot(kblk, w), 0.0) * wi
            score = score + t[:, :qb] + t[:, qb:]
        sc_ref[chunk_ds(c), :] = jnp.where(key_pos(c) <= q_pos, score, -jnp.inf)
        return carry

    lax.fori_loop(0, n_chunks, score_chunk, 0)

    def count(pred):
        n_acc = 8

        def body(c, parts):
            hit = pred(sc_ref[chunk_ds(c), :], key_pos(c))
            parts = list(parts)
            for i in range(kc // SUBLANES):
                tile = hit[i * SUBLANES:(i + 1) * SUBLANES, :]
                parts[i % n_acc] = jnp.where(tile, parts[i % n_acc] + 1, parts[i % n_acc])
            return tuple(parts)
        parts = lax.fori_loop(0, n_chunks, body, (jnp.zeros((SUBLANES, qb), jnp.int32),) * n_acc)
        return jnp.sum(sum(parts), axis=0, keepdims=True)

    def thr_bit(i, state):
        thr, n_ge = state
        cand = thr + (jnp.int32(1) << (31 - i))
        cand_f = _score_of_key(cand)
        n = count(lambda s, p: s >= cand_f)
        return jnp.where(n >= topk, cand, thr), jnp.where(n >= topk, n, n_ge)

    def pending(state):
        thr, n_ge = state
        settled = (n_ge < topk) | ((n_ge == topk) & (thr > INT_MIN))
        return jnp.max(jnp.where(settled, 0, 1))

    state = (jnp.full((1, qb), INT_MIN, jnp.int32), count(lambda s, p: s > -jnp.inf))
    state = lax.fori_loop(0, THR_BITS_ALWAYS, thr_bit, state)

    def more_bits(carry):
        i, state, _ = carry
        for b in range(THR_BITS_STEP):
            state = thr_bit(i + b, state)
        return i + THR_BITS_STEP, state, pending(state)

    _, (thr, _), _ = lax.while_loop(lambda carry: (carry[0] < 32) & (carry[2] > 0), more_bits,
                                    (jnp.int32(THR_BITS_ALWAYS), state, pending(state)))
    thr_f = jnp.where(thr >= LOWEST_FINITE_KEY, _score_of_key(jnp.maximum(thr, LOWEST_FINITE_KEY)), -jnp.inf)

    need = topk - count(lambda s, p: s > thr_f)
    n_tie = count(lambda s, p: (s == thr_f) & (p <= q_pos))

    def tie_cut():
        def cut_bit(i, cut):
            cand = cut + (jnp.int32(1) << (pos_bits - 1 - i))
            n_before = count(lambda s, p: (s == thr_f) & (p <= q_pos) & (p < cand))
            return jnp.where(n_before < need, cand, cut)
        return lax.fori_loop(0, pos_bits, cut_bit, jnp.zeros((1, qb), jnp.int32))

    eye = (lax.broadcasted_iota(jnp.int32, (qb, qb), 0)
           == lax.broadcasted_iota(jnp.int32, (qb, qb), 1)).astype(BF16)

    def write_bias(selected):
        def bias_chunk(c, carry):
            p = key_pos(c)
            sel = (p <= q_pos) & selected(sc_ref[chunk_ds(c), :], p)
            bias_ref[:, chunk_ds(c)] = _dot_nt(eye, jnp.where(sel, 0.0, NEG_BIG).astype(BF16))
            return carry
        lax.fori_loop(0, n_chunks, bias_chunk, 0)

    def bias_with_ties():
        cut = tie_cut()
        write_bias(lambda s, p: (s > thr_f) | ((s == thr_f) & (p <= cut)))

    def bias_no_ties():
        write_bias(lambda s, p: s >= thr_f)

    lax.cond(jnp.max(jnp.where(n_tie > need, 1, 0)) > 0, bias_with_ties, bias_no_ties)

    for h in range(ATT_HEADS):
        rows = slice(h * qb, (h + 1) * qb)
        ql = _dot(q_ref[:, h * ATT_HEAD_DIM:(h + 1) * ATT_HEAD_DIM], wuk_ref[h])
        ql_ref[rows, :] = (ql * (ATT_HEAD_DIM ** -0.5 * LOG2E)).astype(BF16)
        m_ref[rows, :] = jnp.full((qb, LANES), NEG_BIG, F32)
        l_ref[rows, :] = jnp.zeros((qb, LANES), F32)
        acc_ref[rows, :] = jnp.zeros((qb, KV_RANK), F32)
    lane_tiles = kc // LANES
    heads_per_part = ATT_HEADS // MXU_SPLIT

    part_rows = [slice(r * heads_per_part * qb, (r + 1) * heads_per_part * qb) for r in range(MXU_SPLIT)]

    def logits(part, c):
        cblk = ckv_ref[chunk_ds(c), :]
        rel = (c * kc - t0 + lax.broadcasted_iota(jnp.int32, (1, kc), 1)).astype(F32)
        s_part = _dot_nt(ql_ref[part_rows[part], :], cblk)
        for i_h in range(heads_per_part):
            h = part * heads_per_part + i_h
            rows = slice(h * qb, (h + 1) * qb)
            s = s_part[i_h * qb:(i_h + 1) * qb, :] + (slopes_ref[h] * LOG2E) * rel + bias_ref[:, chunk_ds(c)]
            s_ref[rows, :] = s
            mx = s[:, 0:LANES]
            for i in range(1, lane_tiles):
                mx = jnp.maximum(mx, s[:, i * LANES:(i + 1) * LANES])
            m_old = m_ref[rows, :]
            m_new = jnp.maximum(m_old, jnp.max(mx, axis=-1, keepdims=True))
            m_ref[rows, :] = m_new
            alpha_ref[rows, :] = jnp.exp2(m_old - m_new)

    def values(part, c):
        cblk = ckv_ref[chunk_ds(c), :]
        for i_h in range(heads_per_part):
            rows = slice((part * heads_per_part + i_h) * qb, (part * heads_per_part + i_h + 1) * qb)
            m = m_ref[rows, :]
            l = alpha_ref[rows, :] * l_ref[rows, :]
            for i in range(lane_tiles):
                p = jnp.exp2(s_ref[rows, i * LANES:(i + 1) * LANES] - m)
                l = l + p
                p_ref[rows, i * LANES:(i + 1) * LANES] = p.astype(BF16)
            l_ref[rows, :] = l
        upd = _dot(p_ref[part_rows[part], :], cblk)
        for i_h in range(heads_per_part):
            rows = slice((part * heads_per_part + i_h) * qb, (part * heads_per_part + i_h + 1) * qb)
            alpha = alpha_ref[rows, :]
            for i in range(KV_RANK // LANES):
                cols = slice(i * LANES, (i + 1) * LANES)
                acc_ref[rows, cols] = alpha * acc_ref[rows, cols] + upd[i_h * qb:(i_h + 1) * qb, cols]

    last = n_chunks - 1
    logits(0, 0)
    logits(1, 0)

    def att_chunk(c, carry):
        values(0, c)
        logits(0, c + 1)
        values(1, c)
        logits(1, c + 1)
        return carry

    lax.fori_loop(0, last, att_chunk, 0)
    values(0, last)
    values(1, last)

    o = (acc_ref[...] / jnp.sum(l_ref[...], axis=-1, keepdims=True)).astype(BF16)
    y_ref[...] = jnp.concatenate([_dot(o[h * qb:(h + 1) * qb, :], wuv_ref[h]) for h in range(ATT_HEADS)],
                                 axis=1).astype(BF16)


def _dsa(q, qit, wit, ckv, ki, w_uk, w_uv, slopes, bsz, seqlen):
    t = q.shape[0]
    qb = Q_BLOCK
    nq = seqlen // qb
    rows = ATT_HEADS * qb
    topk = min(TOPK_MAX, seqlen // 4)
    pos_bits = max(1, (seqlen - 1).bit_length())
    seq_pad = -(-seqlen // KEY_CHUNK) * KEY_CHUNK
    row = lambda b, j: (b * nq + j, 0)
    col = lambda b, j: (0, b * nq + j)
    per_batch = lambda b, j: (b, 0, 0)
    if seq_pad != seqlen:
        ckv = jnp.pad(ckv.reshape(bsz, seqlen, KV_RANK), ((0, 0), (0, seq_pad - seqlen), (0, 0)))
        ki = jnp.pad(ki.reshape(bsz, seqlen, IDX_DIM), ((0, 0), (0, seq_pad - seqlen), (0, 0)))
    body = functools.partial(_dsa_body, topk=topk, pos_bits=pos_bits)
    return pl.pallas_call(
        body,
        grid=(bsz, nq),
        in_specs=[pl.BlockSpec(memory_space=pltpu.SMEM),
                  pl.BlockSpec((qb, ATT_WIDTH), row), pl.BlockSpec((IDX_HEADS * IDX_DIM, qb), col),
                  pl.BlockSpec((IDX_HEADS, qb), col),
                  pl.BlockSpec((None, seq_pad, KV_RANK), per_batch),
                  pl.BlockSpec((None, seq_pad, IDX_DIM), per_batch),
                  _const_spec(w_uk.shape), _const_spec(w_uv.shape)],
        out_specs=pl.BlockSpec((qb, ATT_WIDTH), row),
        out_shape=jax.ShapeDtypeStruct((t, ATT_WIDTH), BF16),
        scratch_shapes=[pltpu.VMEM((seq_pad, qb), F32), pltpu.VMEM((qb, seq_pad), F32),
                        pltpu.VMEM((rows, KV_RANK), BF16),
                        pltpu.VMEM((rows, KEY_CHUNK), F32), pltpu.VMEM((rows, KEY_CHUNK), BF16),
                        pltpu.VMEM((rows, LANES), F32), pltpu.VMEM((rows, LANES), F32),
                        pltpu.VMEM((rows, LANES), F32), pltpu.VMEM((rows, KV_RANK), F32)],
        compiler_params=_params("arbitrary", "arbitrary"),
        name="dsa",
    )(slopes, q, qit, wit, ckv.reshape(bsz, seq_pad, KV_RANK), ki.reshape(bsz, seq_pad, IDX_DIM), w_uk, w_uv)


def _merge_ln_body(x_ref, ys_ref, ya_ref, sg_ref, wps_ref, wpa_ref, wo_ref, g_ref, b_ref, o_ref):
    sg = sg_ref[...].astype(F32)
    merged = (sg[:, :D_MODEL] * _dot(ys_ref[...], wps_ref[...])
              + sg[:, D_MODEL:] * _dot(ya_ref[...], wpa_ref[...]))
    out = _dot(merged.astype(BF16), wo_ref[...])
    o_ref[...] = _layer_norm(ALPHA * x_ref[...] + out, g_ref[...], b_ref[...])


def _merge_ln(x, ys, ya, sg, wps, wpa, wo, g, b, tm):
    t = x.shape[0]
    row = lambda i: (i, 0)
    return pl.pallas_call(
        _merge_ln_body,
        grid=(t // tm,),
        in_specs=[pl.BlockSpec((tm, D_MODEL), row), pl.BlockSpec((tm, SSD_INNER), row),
                  pl.BlockSpec((tm, ATT_WIDTH), row), pl.BlockSpec((tm, 2 * D_MODEL), row)]
                 + [_const_spec(w.shape) for w in (wps, wpa, wo, g, b)],
        out_specs=pl.BlockSpec((tm, D_MODEL), row),
        out_shape=jax.ShapeDtypeStruct((t, D_MODEL), F32),
        compiler_params=_params("parallel"),
        name="merge_ln",
    )(x, ys, ya, sg, wps, wpa, wo, g, b)


def _mem_kv_body(m_ref, w_ref, o_ref):
    o_ref[...] = _dot(m_ref[...].astype(BF16), w_ref[...]).astype(BF16)


def _mem_kv(mem, w_mkv, bsz, mem_len):
    row = lambda b: (b, 0)
    return pl.pallas_call(
        _mem_kv_body,
        grid=(bsz,),
        in_specs=[pl.BlockSpec((mem_len, D_MODEL), row), _const_spec(w_mkv.shape)],
        out_specs=pl.BlockSpec((mem_len, 2 * D_MODEL), row),
        out_shape=jax.ShapeDtypeStruct((bsz * mem_len, 2 * D_MODEL), BF16),
        compiler_params=_params("parallel"),
        name="mem_kv",
    )(mem, w_mkv)


def _xattn_ln_body(x_ref, kv_ref, wq_ref, wo_ref, g_ref, b_ref, o_ref):
    x = x_ref[...]
    q = (_dot(x.astype(BF16), wq_ref[...]) * (MEM_HEAD_DIM ** -0.5)).astype(BF16)
    heads = []
    for h in range(MEM_HEADS):
        lo = h * MEM_HEAD_DIM
        s = _dot_nt(q[:, lo:lo + MEM_HEAD_DIM], kv_ref[:, lo:lo + MEM_HEAD_DIM])
        p = jnp.exp(s - jnp.max(s, axis=-1, keepdims=True))
        o = _dot(p.astype(BF16), kv_ref[:, D_MODEL + lo:D_MODEL + lo + MEM_HEAD_DIM])
        heads.append((o / jnp.sum(p, axis=-1, keepdims=True)).astype(BF16))
    out = _dot(jnp.concatenate(heads, axis=1), wo_ref[...])
    o_ref[...] = _layer_norm(ALPHA * x + out, g_ref[...], b_ref[...])


def _xattn_ln(x, kv, wq, wo, g, b, bsz, seqlen, mem_len, tm):
    t = x.shape[0]
    nt = seqlen // tm
    return pl.pallas_call(
        _xattn_ln_body,
        grid=(bsz, nt),
        in_specs=[pl.BlockSpec((tm, D_MODEL), lambda b_, i: (b_ * nt + i, 0)),
                  pl.BlockSpec((mem_len, 2 * D_MODEL), lambda b_, i: (b_, 0))]
                 + [_const_spec(w.shape) for w in (wq, wo, g, b)],
        out_specs=pl.BlockSpec((tm, D_MODEL), lambda b_, i: (b_ * nt + i, 0)),
        out_shape=jax.ShapeDtypeStruct((t, D_MODEL), F32),
        compiler_params=_params("parallel", "parallel"),
        name="xattn_ln",
    )(x, kv, wq, wo, g, b)


def _split_w_in(w_in):
    sizes = (D_MODEL, D_MODEL, SSD_INNER, SSD_XBC, SSD_HEADS, ATT_WIDTH, KV_RANK, IDX_HEADS * IDX_DIM,
             IDX_DIM, IDX_HEADS)
    parts, off = [], 0
    for s in sizes:
        parts.append(w_in[:, off:off + s])
        off += s
    g_ssd, g_att, z, xbc, dt, q, ckv, qi, ki, wi = parts
    misc = jnp.concatenate([dt, ki, jnp.zeros((w_in.shape[0], LANES - SSD_HEADS - IDX_DIM), w_in.dtype)], axis=1)
    return [w.astype(BF16) for w in (jnp.concatenate([g_ssd, g_att], axis=1), z, xbc, q, ckv, qi.T, misc, wi.T)]


def _layer(x, mem, bsz, seqlen, mem_len, p):
    (ffn1_w_in, ffn1_w_out, ln1_g, ln1_b, w_in, conv_w, conv_b, dt_bias, a_log, d_skip, ssd_norm_w,
     kv_norm_w, w_uk, w_uv, w_proj_ssd, w_proj_att, w_out, ln2_g, ln2_b,
     w_mq, w_mkv, w_mo, ln3_g, ln3_b, ffn2_w_in, ffn2_w_out, ln4_g, ln4_b) = p
    tm = min(TOKEN_TILE, seqlen)
    vec = lambda v: v.reshape(1, -1)
    bf = lambda w: w.astype(BF16)

    x = _ffn_ln(x, bf(ffn1_w_in), bf(ffn1_w_out), vec(ln1_g), vec(ln1_b), tm)

    sg, sz, xbc, q, ckv, qit, ki, dt, wit = _in_proj(x, *_split_w_in(w_in), vec(kv_norm_w), tm)

    expand = jnp.repeat(jnp.eye(SSD_HEADS, dtype=BF16), SSD_HEAD_DIM, axis=1)
    y_ssd = _ssd(xbc, sz, dt, conv_w, vec(conv_b), vec(dt_bias), vec(a_log),
                 vec(jnp.repeat(d_skip, SSD_HEAD_DIM)), vec(ssd_norm_w), expand, bsz, seqlen)

    slopes = 2.0 ** (-8.0 * jnp.arange(1, ATT_HEADS + 1, dtype=F32) / ATT_HEADS)
    y_att = _dsa(q, qit, wit, ckv, ki, bf(w_uk), bf(w_uv), slopes, bsz, seqlen)

    x = _merge_ln(x, y_ssd, y_att, sg, bf(w_proj_ssd), bf(w_proj_att), bf(w_out), vec(ln2_g), vec(ln2_b), tm)

    kv = _mem_kv(mem, bf(w_mkv), bsz, mem_len)
    x = _xattn_ln(x, kv, bf(w_mq), bf(w_mo), vec(ln3_g), vec(ln3_b), bsz, seqlen, mem_len, tm)

    return _ffn_ln(x, bf(ffn2_w_in), bf(ffn2_w_out), vec(ln4_g), vec(ln4_b), tm)


def kernel(x, mem, ffn1_w_in, ffn1_w_out, ln1_g, ln1_b, w_in, conv_w, conv_b, dt_bias, a_log, d_skip,
           ssd_norm_w, kv_norm_w, w_uk, w_uv, w_proj_ssd, w_proj_att, w_out, ln2_g, ln2_b,
           w_mq, w_mkv, w_mo, ln3_g, ln3_b, ffn2_w_in, ffn2_w_out, ln4_g, ln4_b):
    bsz, seqlen, d = x.shape
    mem_len = mem.shape[1]
    params = (ffn1_w_in, ffn1_w_out, ln1_g, ln1_b, w_in, conv_w, conv_b, dt_bias, a_log, d_skip, ssd_norm_w,
              kv_norm_w, w_uk, w_uv, w_proj_ssd, w_proj_att, w_out, ln2_g, ln2_b,
              w_mq, w_mkv, w_mo, ln3_g, ln3_b, ffn2_w_in, ffn2_w_out, ln4_g, ln4_b)
    h = x.reshape(bsz * seqlen, d)
    m = mem.reshape(bsz * mem_len, d)
    for layer in range(ffn1_w_in.shape[0]):
        h = _layer(h, m, bsz, seqlen, mem_len, tuple(w[layer] for w in params))
    return h.reshape(bsz, seqlen, d)
```

```python
import functools
import math

import jax
import jax.numpy as jnp
from jax import lax
from jax.experimental import pallas as pl
from jax.experimental.pallas import tpu as pltpu

F32 = jnp.float32
BF16 = jnp.bfloat16

D_MODEL = 1024
DEPTH = 1
SSD_HEADS = 16
SSD_HEAD_DIM = 64
SSD_INNER = SSD_HEADS * SSD_HEAD_DIM
SSD_GROUPS = 4
SSD_STATE = 128
SSD_CONV = 4
SSD_CHUNK = 128
SSD_GN = SSD_GROUPS * SSD_STATE
SSD_XBC = SSD_INNER + 2 * SSD_GN
ATT_HEADS = 16
ATT_HEAD_DIM = 64
ATT_WIDTH = ATT_HEADS * ATT_HEAD_DIM
KV_RANK = 256
IDX_HEADS = 16
IDX_DIM = 64
TOPK_MAX = 256
Q_BLOCK = 256
MEM_HEADS = 4
MEM_HEAD_DIM = D_MODEL // MEM_HEADS
D_FF = 2816
ALPHA = (2.0 * DEPTH) ** 0.25
LN_EPS = 1e-5
RMS_EPS = 1e-6

INT_MIN = -(2 ** 31)
INT_MAX = 2 ** 31 - 1
LOWEST_FINITE_KEY = INT_MIN + 2 ** 23
NEG_BIG = -1e30
LOG2E = math.log2(math.e)

LANES = 128
SUBLANES = 8
TOKEN_TILE = 512
FFN_CHUNK = 1408
KEY_CHUNK = 512
THR_BITS_ALWAYS = 24
THR_BITS_STEP = 2
MXU_SPLIT = 8
CONV_TAIL = 8
VMEM_LIMIT = 56 * 1024 * 1024

NT_DIMS = (((1,), (1,)), ((), ()))
TN_DIMS = (((0,), (0,)), ((), ()))


def _const_spec(shape):
    return pl.BlockSpec(shape, lambda *_: (0,) * len(shape), pipeline_mode=pl.Buffered(1))


def _params(*sem):
    return pltpu.CompilerParams(dimension_semantics=sem, vmem_limit_bytes=VMEM_LIMIT)


def _layer_norm(y, g, b):
    mu = jnp.mean(y, axis=-1, keepdims=True)
    d = y - mu
    var = jnp.mean(d * d, axis=-1, keepdims=True)
    return d * lax.rsqrt(var + LN_EPS) * g + b


def _dot(a, b):
    return jnp.dot(a, b, preferred_element_type=F32)


def _dot_nt(a, b):
    return lax.dot_general(a, b, NT_DIMS, preferred_element_type=F32)


def _split3(x):
    hi = x.astype(BF16)
    rest = x - hi.astype(F32)
    mid = rest.astype(BF16)
    return hi, mid, (rest - mid.astype(F32)).astype(BF16)


def _select_sum(sel01, pieces, lhs_is_01):
    if lhs_is_01:
        return sum(_dot(sel01, p) for p in pieces)
    return sum(_dot(p, sel01) for p in pieces)


def _ffn_ln_body(x_ref, win_ref, wout_ref, g_ref, b_ref, o_ref):
    x = x_ref[...]
    xb = x.astype(BF16)
    acc = jnp.zeros(x.shape, F32)
    for c in range(D_FF // FFN_CHUNK):
        lo = c * FFN_CHUNK
        gate = _dot(xb, win_ref[:, lo:lo + FFN_CHUNK])
        up = _dot(xb, win_ref[:, D_FF + lo:D_FF + lo + FFN_CHUNK])
        act = (gate * jax.nn.sigmoid(gate) * up).astype(BF16)
        acc = acc + _dot(act, wout_ref[lo:lo + FFN_CHUNK, :])
    o_ref[...] = _layer_norm(ALPHA * x + 0.5 * acc, g_ref[...], b_ref[...])


def _ffn_ln(x, w_in, w_out, g, b, tm):
    t = x.shape[0]
    row = lambda i: (i, 0)
    return pl.pallas_call(
        _ffn_ln_body,
        grid=(t // tm,),
        in_specs=[pl.BlockSpec((tm, D_MODEL), row),
                  _const_spec(w_in.shape), _const_spec(w_out.shape),
                  _const_spec(g.shape), _const_spec(b.shape)],
        out_specs=pl.BlockSpec((tm, D_MODEL), row),
        out_shape=jax.ShapeDtypeStruct((t, D_MODEL), F32),
        compiler_params=_params("parallel"),
        name="ffn_ln",
    )(x, w_in, w_out, g, b)


def _in_proj_body(x_ref, wg_ref, wz_ref, wxbc_ref, wq_ref, wckv_ref, wqit_ref, wmisc_ref, wwit_ref, kvn_ref,
                  sg_ref, sz_ref, xbc_ref, q_ref, ckv_ref, qit_ref, ki_ref, dt_ref, wit_ref):
    xb = x_ref[...].astype(BF16)
    sg_ref[...] = jax.nn.sigmoid(_dot(xb, wg_ref[...])).astype(BF16)
    z = _dot(xb, wz_ref[...])
    sz_ref[...] = (z * jax.nn.sigmoid(z)).astype(BF16)
    xbc_ref[...] = _dot(xb, wxbc_ref[...]).astype(BF16)
    q_ref[...] = _dot(xb, wq_ref[...]).astype(BF16)
    c = _dot(xb, wckv_ref[...])
    c = c * lax.rsqrt(jnp.mean(c * c, axis=-1, keepdims=True) + RMS_EPS) * kvn_ref[...]
    ckv_ref[...] = c.astype(BF16)
    qit_ref[...] = _dot_nt(wqit_ref[...], xb).astype(BF16)
    wit_ref[...] = _dot_nt(wwit_ref[...], xb) * (IDX_HEADS ** -0.5)
    misc = _dot(xb, wmisc_ref[...])
    dt_ref[...] = misc[:, 0:SSD_HEADS]
    ki_ref[...] = misc[:, SSD_HEADS:SSD_HEADS + IDX_DIM].astype(BF16)


def _in_proj(x, wg, wz, wxbc, wq, wckv, wqit, wmisc, wwit, kvn, tm):
    t = x.shape[0]
    row = lambda i: (i, 0)
    col = lambda i: (0, i)
    out = [
        ((t, 2 * D_MODEL), BF16, (tm, 2 * D_MODEL), row),
        ((t, SSD_INNER), BF16, (tm, SSD_INNER), row),
        ((t, SSD_XBC), BF16, (tm, SSD_XBC), row),
        ((t, ATT_WIDTH), BF16, (tm, ATT_WIDTH), row),
        ((t, KV_RANK), BF16, (tm, KV_RANK), row),
        ((IDX_HEADS * IDX_DIM, t), BF16, (IDX_HEADS * IDX_DIM, tm), col),
        ((t, IDX_DIM), BF16, (tm, IDX_DIM), row),
        ((t, SSD_HEADS), F32, (tm, SSD_HEADS), row),
        ((IDX_HEADS, t), F32, (IDX_HEADS, tm), col),
    ]
    return pl.pallas_call(
        _in_proj_body,
        grid=(t // tm,),
        in_specs=[pl.BlockSpec((tm, D_MODEL), row)] + [_const_spec(w.shape) for w in
                                                        (wg, wz, wxbc, wq, wckv, wqit, wmisc, wwit, kvn)],
        out_specs=[pl.BlockSpec(blk, im) for _, _, blk, im in out],
        out_shape=[jax.ShapeDtypeStruct(shp, dt) for shp, dt, _, _ in out],
        compiler_params=_params("parallel"),
        name="in_proj",
    )(x, wg, wz, wxbc, wq, wckv, wqit, wmisc, wwit, kvn)


def _ssd_body(xbc_ref, sz_ref, dt_ref, cw_ref, cb_ref, dtb_ref, alog_ref, dskip_ref, nw_ref, expand_ref,
              y_ref, state_ref, ext_ref):
    chunk = xbc_ref.shape[0]
    hp = SSD_HEAD_DIM
    gw = SSD_INNER // SSD_GROUPS

    @pl.when(pl.program_id(1) == 0)
    def _():
        state_ref[...] = jnp.zeros(state_ref.shape, F32)
        ext_ref[0:CONV_TAIL, :] = jnp.zeros((CONV_TAIL, SSD_XBC), F32)

    ext_ref[CONV_TAIL:, :] = xbc_ref[...].astype(F32)
    conv = cb_ref[...] + cw_ref[SSD_CONV - 1:SSD_CONV, :] * ext_ref[CONV_TAIL:, :]
    for k in range(SSD_CONV - 1):
        off = CONV_TAIL - (SSD_CONV - 1) + k
        conv = conv + cw_ref[k:k + 1, :] * ext_ref[off:off + chunk, :]
    ext_ref[0:CONV_TAIL, :] = ext_ref[chunk:chunk + CONV_TAIL, :]
    act = conv * jax.nn.sigmoid(conv)
    xs = act[:, :SSD_INNER]

    pre = dt_ref[...] + dtb_ref[...]
    dt = jnp.maximum(pre, 0.0) + jnp.log(1.0 + jnp.exp(-jnp.abs(pre)))
    a = dt * (-jnp.exp(alog_ref[...]))
    r_i = lax.broadcasted_iota(jnp.int32, (chunk, chunk), 0)
    c_i = lax.broadcasted_iota(jnp.int32, (chunk, chunk), 1)
    causal = r_i >= c_i
    a_cs = _select_sum(causal.astype(BF16), _split3(a), lhs_is_01=True)
    eye = (lax.broadcasted_iota(jnp.int32, (SSD_HEADS, SSD_HEADS), 0)
           == lax.broadcasted_iota(jnp.int32, (SSD_HEADS, SSD_HEADS), 1)).astype(BF16)
    a_cs_3 = _split3(a_cs)
    a_cs_t = sum(_dot_nt(eye, piece) for piece in a_cs_3)
    expand = expand_ref[...]
    dt_e = _select_sum(expand, _split3(dt), lhs_is_01=False)
    a_cs_e = _select_sum(expand, a_cs_3, lhs_is_01=False)
    a_end_e = a_cs_e[chunk - 1:chunk, :]

    xdt = xs * dt_e
    xdt_b = xdt.astype(BF16)
    xdt_end_b = (xdt * jnp.exp(a_end_e - a_cs_e)).astype(BF16)
    state = state_ref[...]
    state_b = state.astype(BF16)

    y_diag, y_off, new_state = [], [], []
    for g in range(SSD_GROUPS):
        bm = act[:, SSD_INNER + g * SSD_STATE:SSD_INNER + (g + 1) * SSD_STATE].astype(BF16)
        cm = act[:, SSD_INNER + SSD_GN + g * SSD_STATE:SSD_INNER + SSD_GN + (g + 1) * SSD_STATE].astype(BF16)
        cb = _dot_nt(cm, bm)
        y_off.append(_dot(cm, state_b[:, g * gw:(g + 1) * gw]))
        new_state.append(lax.dot_general(bm, xdt_end_b[:, g * gw:(g + 1) * gw], TN_DIMS,
                                         preferred_element_type=F32))
        for r in range(SSD_HEADS // SSD_GROUPS):
            h = g * (SSD_HEADS // SSD_GROUPS) + r
            seg = a_cs[:, h:h + 1] - a_cs_t[h:h + 1, :]
            decay = jnp.exp(jnp.where(causal, seg, NEG_BIG))
            y_diag.append(_dot((cb * decay).astype(BF16), xdt_b[:, h * hp:(h + 1) * hp]))
    state_ref[...] = jnp.exp(a_end_e) * state + jnp.concatenate(new_state, axis=1)

    y = jnp.concatenate(y_diag, axis=1) + jnp.concatenate(y_off, axis=1) * jnp.exp(a_cs_e)
    y = (y + dskip_ref[...] * xs) * sz_ref[...].astype(F32)
    outs = []
    for g in range(SSD_GROUPS):
        yg = y[:, g * gw:(g + 1) * gw]
        outs.append(yg * lax.rsqrt(jnp.mean(yg * yg, axis=-1, keepdims=True) + RMS_EPS))
    y_ref[...] = (jnp.concatenate(outs, axis=1) * nw_ref[...]).astype(BF16)


def _ssd(xbc, sz, dt, conv_w, conv_b, dt_bias, a_log, d_skip_e, norm_w, expand, bsz, seqlen):
    t = xbc.shape[0]
    nc = seqlen // SSD_CHUNK
    row = lambda b, c: (b * nc + c, 0)
    return pl.pallas_call(
        _ssd_body,
        grid=(bsz, nc),
        in_specs=[pl.BlockSpec((SSD_CHUNK, SSD_XBC), row), pl.BlockSpec((SSD_CHUNK, SSD_INNER), row),
                  pl.BlockSpec((SSD_CHUNK, SSD_HEADS), row)]
                 + [_const_spec(w.shape) for w in (conv_w, conv_b, dt_bias, a_log, d_skip_e, norm_w, expand)],
        out_specs=pl.BlockSpec((SSD_CHUNK, SSD_INNER), row),
        out_shape=jax.ShapeDtypeStruct((t, SSD_INNER), BF16),
        scratch_shapes=[pltpu.VMEM((SSD_STATE, SSD_INNER), F32),
                        pltpu.VMEM((CONV_TAIL + SSD_CHUNK, SSD_XBC), F32)],
        compiler_params=_params("arbitrary", "arbitrary"),
        name="ssd",
    )(xbc, sz, dt, conv_w, conv_b, dt_bias, a_log, d_skip_e, norm_w, expand)


def _score_of_key(key):
    return pltpu.bitcast(jnp.where(key >= 0, key, key ^ INT_MAX), F32)


def _dsa_body(slopes_ref, q_ref, qit_ref, wit_ref, ckv_ref, ki_ref, wuk_ref, wuv_ref, y_ref,
              sc_ref, bias_ref, ql_ref, s_ref, p_ref, m_ref, l_ref, alpha_ref, acc_ref, *, topk, pos_bits):
    qb = q_ref.shape[0]
    kc = KEY_CHUNK
    t0 = pl.program_id(1) * qb
    n_chunks = (t0 + qb + kc - 1) // kc
    q_pos = t0 + lax.broadcasted_iota(jnp.int32, (1, qb), 1)

    def key_pos(c):
        return c * kc + lax.broadcasted_iota(jnp.int32, (kc, qb), 0)

    def chunk_ds(c):
        return pl.ds(pl.multiple_of(c * kc, kc), kc)

    def score_chunk(c, carry):
        kblk = ki_ref[chunk_ds(c), :]
        score = jnp.zeros((kc, qb), F32)
        for h in range(0, IDX_HEADS, 2):
            w = jnp.concatenate([qit_ref[h * IDX_DIM:(h + 1) * IDX_DIM, :],
                                 qit_ref[(h + 1) * IDX_DIM:(h + 2) * IDX_DIM, :]], axis=1)
            wi = jnp.concatenate([wit_ref[h:h + 1, :], wit_ref[h + 1:h + 2, :]], axis=1)
            t = jnp.maximum(_dot(kblk, w), 0.0) * wi
            score = score + t[:, :qb] + t[:, qb:]
        sc_ref[chunk_ds(c), :] = jnp.where(key_pos(c) <= q_pos, score, -jnp.inf)
        return carry

    lax.fori_loop(0, n_chunks, score_chunk, 0)

    def count(pred):
        n_acc = 8

        def body(c, parts):
            hit = pred(sc_ref[chunk_ds(c), :], key_pos(c))
            parts = list(parts)
            for i in range(kc // SUBLANES):
                tile = hit[i * SUBLANES:(i + 1) * SUBLANES, :]
                parts[i % n_acc] = jnp.where(tile, parts[i % n_acc] + 1, parts[i % n_acc])
            return tuple(parts)
        parts = lax.fori_loop(0, n_chunks, body, (jnp.zeros((SUBLANES, qb), jnp.int32),) * n_acc)
        return jnp.sum(sum(parts), axis=0, keepdims=True)

    def thr_bit(i, state):
        thr, n_ge = state
        cand = thr + (jnp.int32(1) << (31 - i))
        cand_f = _score_of_key(cand)
        n = count(lambda s, p: s >= cand_f)
        return jnp.where(n >= topk, cand, thr), jnp.where(n >= topk, n, n_ge)

    def pending(state):
        thr, n_ge = state
        settled = (n_ge < topk) | ((n_ge == topk) & (thr > INT_MIN))
        return jnp.max(jnp.where(settled, 0, 1))

    state = (jnp.full((1, qb), INT_MIN, jnp.int32), count(lambda s, p: s > -jnp.inf))
    state = lax.fori_loop(0, THR_BITS_ALWAYS, thr_bit, state)

    def more_bits(carry):
        i, state, _ = carry
        for b in range(THR_BITS_STEP):
            state = thr_bit(i + b, state)
        return i + THR_BITS_STEP, state, pending(state)

    _, (thr, _), _ = lax.while_loop(lambda carry: (carry[0] < 32) & (carry[2] > 0), more_bits,
                                    (jnp.int32(THR_BITS_ALWAYS), state, pending(state)))
    thr_f = jnp.where(thr >= LOWEST_FINITE_KEY, _score_of_key(jnp.maximum(thr, LOWEST_FINITE_KEY)), -jnp.inf)

    need = topk - count(lambda s, p: s > thr_f)
    n_tie = count(lambda s, p: (s == thr_f) & (p <= q_pos))

    def tie_cut():
        def cut_bit(i, cut):
            cand = cut + (jnp.int32(1) << (pos_bits - 1 - i))
            n_before = count(lambda s, p: (s == thr_f) & (p <= q_pos) & (p < cand))
            return jnp.where(n_before < need, cand, cut)
        return lax.fori_loop(0, pos_bits, cut_bit, jnp.zeros((1, qb), jnp.int32))

    eye = (lax.broadcasted_iota(jnp.int32, (qb, qb), 0)
           == lax.broadcasted_iota(jnp.int32, (qb, qb), 1)).astype(BF16)

    def write_bias(selected):
        def bias_chunk(c, carry):
            p = key_pos(c)
            sel = (p <= q_pos) & selected(sc_ref[chunk_ds(c), :], p)
            bias_ref[:, chunk_ds(c)] = _dot_nt(eye, jnp.where(sel, 0.0, NEG_BIG).astype(BF16))
            return carry
        lax.fori_loop(0, n_chunks, bias_chunk, 0)

    def bias_with_ties():
        cut = tie_cut()
        write_bias(lambda s, p: (s > thr_f) | ((s == thr_f) & (p <= cut)))

    def bias_no_ties():
        write_bias(lambda s, p: s >= thr_f)

    lax.cond(jnp.max(jnp.where(n_tie > need, 1, 0)) > 0, bias_with_ties, bias_no_ties)

    for h in range(ATT_HEADS):
        rows = slice(h * qb, (h + 1) * qb)
        ql = _dot(q_ref[:, h * ATT_HEAD_DIM:(h + 1) * ATT_HEAD_DIM], wuk_ref[h])
        ql_ref[rows, :] = (ql * (ATT_HEAD_DIM ** -0.5 * LOG2E)).astype(BF16)
        m_ref[rows, :] = jnp.full((qb, LANES), NEG_BIG, F32)
        l_ref[rows, :] = jnp.zeros((qb, LANES), F32)
        acc_ref[rows, :] = jnp.zeros((qb, KV_RANK), F32)
    lane_tiles = kc // LANES
    heads_per_part = ATT_HEADS // MXU_SPLIT

    part_rows = [slice(r * heads_per_part * qb, (r + 1) * heads_per_part * qb) for r in range(MXU_SPLIT)]

    def logits(part, c):
        cblk = ckv_ref[chunk_ds(c), :]
        rel = (c * kc - t0 + lax.broadcasted_iota(jnp.int32, (1, kc), 1)).astype(F32)
        s_part = _dot_nt(ql_ref[part_rows[part], :], cblk)
        for i_h in range(heads_per_part):
            h = part * heads_per_part + i_h
            rows = slice(h * qb, (h + 1) * qb)
            s = s_part[i_h * qb:(i_h + 1) * qb, :] + (slopes_ref[h] * LOG2E) * rel + bias_ref[:, chunk_ds(c)]
            s_ref[rows, :] = s
            mx = s[:, 0:LANES]
            for i in range(1, lane_tiles):
                mx = jnp.maximum(mx, s[:, i * LANES:(i + 1) * LANES])
            m_old = m_ref[rows, :]
            m_new = jnp.maximum(m_old, jnp.max(mx, axis=-1, keepdims=True))
            m_ref[rows, :] = m_new
            alpha_ref[rows, :] = jnp.exp2(m_old - m_new)

    def values(part, c):
        cblk = ckv_ref[chunk_ds(c), :]
        for i_h in range(heads_per_part):
            rows = slice((part * heads_per_part + i_h) * qb, (part * heads_per_part + i_h + 1) * qb)
            m = m_ref[rows, :]
            l = alpha_ref[rows, :] * l_ref[rows, :]
            for i in range(lane_tiles):
                p = jnp.exp2(s_ref[rows, i * LANES:(i + 1) * LANES] - m)
                l = l + p
                p_ref[rows, i * LANES:(i + 1) * LANES] = p.astype(BF16)
            l_ref[rows, :] = l
        upd = _dot(p_ref[part_rows[part], :], cblk)
        for i_h in range(heads_per_part):
            rows = slice((part * heads_per_part + i_h) * qb, (part * heads_per_part + i_h + 1) * qb)
            alpha = alpha_ref[rows, :]
            for i in range(KV_RANK // LANES):
                cols = slice(i * LANES, (i + 1) * LANES)
                acc_ref[rows, cols] = alpha * acc_ref[rows, cols] + upd[i_h * qb:(i_h + 1) * qb, cols]

    last = n_chunks - 1
    for part in range(MXU_SPLIT):
        logits(part, 0)

    def att_chunk(c, carry):
        for part in range(MXU_SPLIT):
            values(part, c)
            logits(part, c + 1)
        return carry

    lax.fori_loop(0, last, att_chunk, 0)
    for part in range(MXU_SPLIT):
        values(part, last)

    o = (acc_ref[...] / jnp.sum(l_ref[...], axis=-1, keepdims=True)).astype(BF16)
    y_ref[...] = jnp.concatenate([_dot(o[h * qb:(h + 1) * qb, :], wuv_ref[h]) for h in range(ATT_HEADS)],
                                 axis=1).astype(BF16)


def _dsa(q, qit, wit, ckv, ki, w_uk, w_uv, slopes, bsz, seqlen):
    t = q.shape[0]
    qb = Q_BLOCK
    nq = seqlen // qb
    rows = ATT_HEADS * qb
    topk = min(TOPK_MAX, seqlen // 4)
    pos_bits = max(1, (seqlen - 1).bit_length())
    seq_pad = -(-seqlen // KEY_CHUNK) * KEY_CHUNK
    row = lambda b, j: (b * nq + j, 0)
    col = lambda b, j: (0, b * nq + j)
    per_batch = lambda b, j: (b, 0, 0)
    if seq_pad != seqlen:
        ckv = jnp.pad(ckv.reshape(bsz, seqlen, KV_RANK), ((0, 0), (0, seq_pad - seqlen), (0, 0)))
        ki = jnp.pad(ki.reshape(bsz, seqlen, IDX_DIM), ((0, 0), (0, seq_pad - seqlen), (0, 0)))
    body = functools.partial(_dsa_body, topk=topk, pos_bits=pos_bits)
    return pl.pallas_call(
        body,
        grid=(bsz, nq),
        in_specs=[pl.BlockSpec(memory_space=pltpu.SMEM),
                  pl.BlockSpec((qb, ATT_WIDTH), row), pl.BlockSpec((IDX_HEADS * IDX_DIM, qb), col),
                  pl.BlockSpec((IDX_HEADS, qb), col),
                  pl.BlockSpec((None, seq_pad, KV_RANK), per_batch),
                  pl.BlockSpec((None, seq_pad, IDX_DIM), per_batch),
                  _const_spec(w_uk.shape), _const_spec(w_uv.shape)],
        out_specs=pl.BlockSpec((qb, ATT_WIDTH), row),
        out_shape=jax.ShapeDtypeStruct((t, ATT_WIDTH), BF16),
        scratch_shapes=[pltpu.VMEM((seq_pad, qb), F32), pltpu.VMEM((qb, seq_pad), F32),
                        pltpu.VMEM((rows, KV_RANK), BF16),
                        pltpu.VMEM((rows, KEY_CHUNK), F32), pltpu.VMEM((rows, KEY_CHUNK), BF16),
                        pltpu.VMEM((rows, LANES), F32), pltpu.VMEM((rows, LANES), F32),
                        pltpu.VMEM((rows, LANES), F32), pltpu.VMEM((rows, KV_RANK), F32)],
        compiler_params=_params("arbitrary", "arbitrary"),
        name="dsa",
    )(slopes, q, qit, wit, ckv.reshape(bsz, seq_pad, KV_RANK), ki.reshape(bsz, seq_pad, IDX_DIM), w_uk, w_uv)


def _merge_ln_body(x_ref, ys_ref, ya_ref, sg_ref, wps_ref, wpa_ref, wo_ref, g_ref, b_ref, o_ref):
    sg = sg_ref[...].astype(F32)
    merged = (sg[:, :D_MODEL] * _dot(ys_ref[...], wps_ref[...])
              + sg[:, D_MODEL:] * _dot(ya_ref[...], wpa_ref[...]))
    out = _dot(merged.astype(BF16), wo_ref[...])
    o_ref[...] = _layer_norm(ALPHA * x_ref[...] + out, g_ref[...], b_ref[...])


def _merge_ln(x, ys, ya, sg, wps, wpa, wo, g, b, tm):
    t = x.shape[0]
    row = lambda i: (i, 0)
    return pl.pallas_call(
        _merge_ln_body,
        grid=(t // tm,),
        in_specs=[pl.BlockSpec((tm, D_MODEL), row), pl.BlockSpec((tm, SSD_INNER), row),
                  pl.BlockSpec((tm, ATT_WIDTH), row), pl.BlockSpec((tm, 2 * D_MODEL), row)]
                 + [_const_spec(w.shape) for w in (wps, wpa, wo, g, b)],
        out_specs=pl.BlockSpec((tm, D_MODEL), row),
        out_shape=jax.ShapeDtypeStruct((t, D_MODEL), F32),
        compiler_params=_params("parallel"),
        name="merge_ln",
    )(x, ys, ya, sg, wps, wpa, wo, g, b)


def _mem_kv_body(m_ref, w_ref, o_ref):
    o_ref[...] = _dot(m_ref[...].astype(BF16), w_ref[...]).astype(BF16)


def _mem_kv(mem, w_mkv, bsz, mem_len):
    row = lambda b: (b, 0)
    return pl.pallas_call(
        _mem_kv_body,
        grid=(bsz,),
        in_specs=[pl.BlockSpec((mem_len, D_MODEL), row), _const_spec(w_mkv.shape)],
        out_specs=pl.BlockSpec((mem_len, 2 * D_MODEL), row),
        out_shape=jax.ShapeDtypeStruct((bsz * mem_len, 2 * D_MODEL), BF16),
        compiler_params=_params("parallel"),
        name="mem_kv",
    )(mem, w_mkv)


def _xattn_ln_body(x_ref, kv_ref, wq_ref, wo_ref, g_ref, b_ref, o_ref):
    x = x_ref[...]
    q = (_dot(x.astype(BF16), wq_ref[...]) * (MEM_HEAD_DIM ** -0.5)).astype(BF16)
    heads = []
    for h in range(MEM_HEADS):
        lo = h * MEM_HEAD_DIM
        s = _dot_nt(q[:, lo:lo + MEM_HEAD_DIM], kv_ref[:, lo:lo + MEM_HEAD_DIM])
        p = jnp.exp(s - jnp.max(s, axis=-1, keepdims=True))
        o = _dot(p.astype(BF16), kv_ref[:, D_MODEL + lo:D_MODEL + lo + MEM_HEAD_DIM])
        heads.append((o / jnp.sum(p, axis=-1, keepdims=True)).astype(BF16))
    out = _dot(jnp.concatenate(heads, axis=1), wo_ref[...])
    o_ref[...] = _layer_norm(ALPHA * x + out, g_ref[...], b_ref[...])


def _xattn_ln(x, kv, wq, wo, g, b, bsz, seqlen, mem_len, tm):
    t = x.shape[0]
    nt = seqlen // tm
    return pl.pallas_call(
        _xattn_ln_body,
        grid=(bsz, nt),
        in_specs=[pl.BlockSpec((tm, D_MODEL), lambda b_, i: (b_ * nt + i, 0)),
                  pl.BlockSpec((mem_len, 2 * D_MODEL), lambda b_, i: (b_, 0))]
                 + [_const_spec(w.shape) for w in (wq, wo, g, b)],
        out_specs=pl.BlockSpec((tm, D_MODEL), lambda b_, i: (b_ * nt + i, 0)),
        out_shape=jax.ShapeDtypeStruct((t, D_MODEL), F32),
        compiler_params=_params("parallel", "parallel"),
        name="xattn_ln",
    )(x, kv, wq, wo, g, b)


def _split_w_in(w_in):
    sizes = (D_MODEL, D_MODEL, SSD_INNER, SSD_XBC, SSD_HEADS, ATT_WIDTH, KV_RANK, IDX_HEADS * IDX_DIM,
             IDX_DIM, IDX_HEADS)
    parts, off = [], 0
    for s in sizes:
        parts.append(w_in[:, off:off + s])
        off += s
    g_ssd, g_att, z, xbc, dt, q, ckv, qi, ki, wi = parts
    misc = jnp.concatenate([dt, ki, jnp.zeros((w_in.shape[0], LANES - SSD_HEADS - IDX_DIM), w_in.dtype)], axis=1)
    return [w.astype(BF16) for w in (jnp.concatenate([g_ssd, g_att], axis=1), z, xbc, q, ckv, qi.T, misc, wi.T)]


def _layer(x, mem, bsz, seqlen, mem_len, p):
    (ffn1_w_in, ffn1_w_out, ln1_g, ln1_b, w_in, conv_w, conv_b, dt_bias, a_log, d_skip, ssd_norm_w,
     kv_norm_w, w_uk, w_uv, w_proj_ssd, w_proj_att, w_out, ln2_g, ln2_b,
     w_mq, w_mkv, w_mo, ln3_g, ln3_b, ffn2_w_in, ffn2_w_out, ln4_g, ln4_b) = p
    tm = min(TOKEN_TILE, seqlen)
    vec = lambda v: v.reshape(1, -1)
    bf = lambda w: w.astype(BF16)

    x = _ffn_ln(x, bf(ffn1_w_in), bf(ffn1_w_out), vec(ln1_g), vec(ln1_b), tm)

    sg, sz, xbc, q, ckv, qit, ki, dt, wit = _in_proj(x, *_split_w_in(w_in), vec(kv_norm_w), tm)

    expand = jnp.repeat(jnp.eye(SSD_HEADS, dtype=BF16), SSD_HEAD_DIM, axis=1)
    y_ssd = _ssd(xbc, sz, dt, conv_w, vec(conv_b), vec(dt_bias), vec(a_log),
                 vec(jnp.repeat(d_skip, SSD_HEAD_DIM)), vec(ssd_norm_w), expand, bsz, seqlen)

    slopes = 2.0 ** (-8.0 * jnp.arange(1, ATT_HEADS + 1, dtype=F32) / ATT_HEADS)
    y_att = _dsa(q, qit, wit, ckv, ki, bf(w_uk), bf(w_uv), slopes, bsz, seqlen)

    x = _merge_ln(x, y_ssd, y_att, sg, bf(w_proj_ssd), bf(w_proj_att), bf(w_out), vec(ln2_g), vec(ln2_b), tm)

    kv = _mem_kv(mem, bf(w_mkv), bsz, mem_len)
    x = _xattn_ln(x, kv, bf(w_mq), bf(w_mo), vec(ln3_g), vec(ln3_b), bsz, seqlen, mem_len, tm)

    return _ffn_ln(x, bf(ffn2_w_in), bf(ffn2_w_out), vec(ln4_g), vec(ln4_b), tm)


def kernel(x, mem, ffn1_w_in, ffn1_w_out, ln1_g, ln1_b, w_in, conv_w, conv_b, dt_bias, a_log, d_skip,
           ssd_norm_w, kv_norm_w, w_uk, w_uv, w_proj_ssd, w_proj_att, w_out, ln2_g, ln2_b,
           w_mq, w_mkv, w_mo, ln3_g, ln3_b, ffn2_w_in, ffn2_w_out, ln4_g, ln4_b):
    bsz, seqlen, d = x.shape
    mem_len = mem.shape[1]
    params = (ffn1_w_in, ffn1_w_out, ln1_g, ln1_b, w_in, conv_w, conv_b, dt_bias, a_log, d_skip, ssd_norm_w,
              kv_norm_w, w_uk, w_uv, w_proj_ssd, w_proj_att, w_out, ln2_g, ln2_b,
              w_mq, w_mkv, w_mo, ln3_g, ln3_b, ffn2_w_in, ffn2_w_out, ln4_g, ln4_b)
    h = x.reshape(bsz * seqlen, d)
    m = mem.reshape(bsz * mem_len, d)
    for layer in range(ffn1_w_in.shape[0]):
        h = _layer(h, m, bsz, seqlen, mem_len, tuple(w[layer] for w in params))
    return h.reshape(bsz, seqlen, d)
```

```python
import functools
import math

import jax
import jax.numpy as jnp
from jax import lax
from jax.experimental import pallas as pl
from jax.experimental.pallas import tpu as pltpu

F32 = jnp.float32
BF16 = jnp.bfloat16

D_MODEL = 1024
DEPTH = 1
SSD_HEADS = 16
SSD_HEAD_DIM = 64
SSD_INNER = SSD_HEADS * SSD_HEAD_DIM
SSD_GROUPS = 4
SSD_STATE = 128
SSD_CONV = 4
SSD_CHUNK = 128
SSD_STEP_CHUNKS = 2
SSD_GN = SSD_GROUPS * SSD_STATE
SSD_XBC = SSD_INNER + 2 * SSD_GN
ATT_HEADS = 16
ATT_HEAD_DIM = 64
ATT_WIDTH = ATT_HEADS * ATT_HEAD_DIM
KV_RANK = 256
IDX_HEADS = 16
IDX_DIM = 64
TOPK_MAX = 256
Q_BLOCK = 256
MEM_HEADS = 4
MEM_HEAD_DIM = D_MODEL // MEM_HEADS
D_FF = 2816
ALPHA = (2.0 * DEPTH) ** 0.25
LN_EPS = 1e-5
RMS_EPS = 1e-6

INT_MIN = -(2 ** 31)
INT_MAX = 2 ** 31 - 1
LOWEST_FINITE_KEY = INT_MIN + 2 ** 23
NEG_BIG = -1e30
LOG2E = math.log2(math.e)

LANES = 128
SUBLANES = 8
BF16_ROWS = 16
TOKEN_TILE = 512
FFN_CHUNK = 1408
KEY_CHUNK = 512
THR_BITS_ALWAYS = 9
THR_BITS_STEP = 2
MXU_SPLIT = 8
CONV_TAIL = 16
VMEM_LIMIT = 56 * 1024 * 1024

NT_DIMS = (((1,), (1,)), ((), ()))
TN_DIMS = (((0,), (0,)), ((), ()))


def _const_spec(shape):
    return pl.BlockSpec(shape, lambda *_: (0,) * len(shape), pipeline_mode=pl.Buffered(1))


def _params(*sem):
    return pltpu.CompilerParams(dimension_semantics=sem, vmem_limit_bytes=VMEM_LIMIT)


def _layer_norm(y, g, b):
    mu = jnp.mean(y, axis=-1, keepdims=True)
    d = y - mu
    var = jnp.mean(d * d, axis=-1, keepdims=True)
    return d * lax.rsqrt(var + LN_EPS) * g + b


def _dot(a, b):
    return jnp.dot(a, b, preferred_element_type=F32)


def _dot_nt(a, b):
    return lax.dot_general(a, b, NT_DIMS, preferred_element_type=F32)


def _split3(x):
    hi = x.astype(BF16)
    rest = x - hi.astype(F32)
    mid = rest.astype(BF16)
    return hi, mid, (rest - mid.astype(F32)).astype(BF16)


def _select_sum(sel01, pieces, lhs_is_01):
    if lhs_is_01:
        return sum(_dot(sel01, p) for p in pieces)
    return sum(_dot(p, sel01) for p in pieces)


def _ffn_ln_body(x_ref, win_ref, wout_ref, g_ref, b_ref, o_ref):
    x = x_ref[...]
    xb = x.astype(BF16)
    acc = jnp.zeros(x.shape, F32)
    for c in range(D_FF // FFN_CHUNK):
        lo = c * FFN_CHUNK
        gate = _dot(xb, win_ref[:, lo:lo + FFN_CHUNK])
        up = _dot(xb, win_ref[:, D_FF + lo:D_FF + lo + FFN_CHUNK])
        act = (gate * jax.nn.sigmoid(gate) * up).astype(BF16)
        acc = acc + _dot(act, wout_ref[lo:lo + FFN_CHUNK, :])
    o_ref[...] = _layer_norm(ALPHA * x + 0.5 * acc, g_ref[...], b_ref[...])


def _ffn_ln(x, w_in, w_out, g, b, tm):
    t = x.shape[0]
    row = lambda i: (i, 0)
    return pl.pallas_call(
        _ffn_ln_body,
        grid=(t // tm,),
        in_specs=[pl.BlockSpec((tm, D_MODEL), row),
                  _const_spec(w_in.shape), _const_spec(w_out.shape),
                  _const_spec(g.shape), _const_spec(b.shape)],
        out_specs=pl.BlockSpec((tm, D_MODEL), row),
        out_shape=jax.ShapeDtypeStruct((t, D_MODEL), F32),
        compiler_params=_params("parallel"),
        name="ffn_ln",
    )(x, w_in, w_out, g, b)


def _in_proj_body(x_ref, wg_ref, wz_ref, wxbc_ref, wq_ref, wckv_ref, wqit_ref, wmisc_ref, wwit_ref, kvn_ref,
                  sg_ref, sz_ref, xbc_ref, q_ref, ckv_ref, qit_ref, ki_ref, dt_ref, wit_ref):
    xb = x_ref[...].astype(BF16)
    sg_ref[...] = jax.nn.sigmoid(_dot(xb, wg_ref[...])).astype(BF16)
    z = _dot(xb, wz_ref[...])
    sz_ref[...] = (z * jax.nn.sigmoid(z)).astype(BF16)
    xbc_ref[...] = _dot(xb, wxbc_ref[...]).astype(BF16)
    q_ref[...] = _dot(xb, wq_ref[...]).astype(BF16)
    c = _dot(xb, wckv_ref[...])
    c = c * lax.rsqrt(jnp.mean(c * c, axis=-1, keepdims=True) + RMS_EPS) * kvn_ref[...]
    ckv_ref[...] = c.astype(BF16)
    qit_ref[...] = _dot_nt(wqit_ref[...], xb).astype(BF16)
    wit_ref[...] = _dot_nt(wwit_ref[...], xb) * (IDX_HEADS ** -0.5)
    misc = _dot(xb, wmisc_ref[...])
    dt_ref[...] = misc[:, 0:SSD_HEADS]
    ki_ref[...] = misc[:, SSD_HEADS:SSD_HEADS + IDX_DIM].astype(BF16)


def _in_proj(x, wg, wz, wxbc, wq, wckv, wqit, wmisc, wwit, kvn, tm):
    t = x.shape[0]
    row = lambda i: (i, 0)
    col = lambda i: (0, i)
    out = [
        ((t, 2 * D_MODEL), BF16, (tm, 2 * D_MODEL), row),
        ((t, SSD_INNER), BF16, (tm, SSD_INNER), row),
        ((t, SSD_XBC), BF16, (tm, SSD_XBC), row),
        ((t, ATT_WIDTH), BF16, (tm, ATT_WIDTH), row),
        ((t, KV_RANK), BF16, (tm, KV_RANK), row),
        ((IDX_HEADS * IDX_DIM, t), BF16, (IDX_HEADS * IDX_DIM, tm), col),
        ((t, IDX_DIM), BF16, (tm, IDX_DIM), row),
        ((t, SSD_HEADS), F32, (tm, SSD_HEADS), row),
        ((IDX_HEADS, t), F32, (IDX_HEADS, tm), col),
    ]
    return pl.pallas_call(
        _in_proj_body,
        grid=(t // tm,),
        in_specs=[pl.BlockSpec((tm, D_MODEL), row)] + [_const_spec(w.shape) for w in
                                                        (wg, wz, wxbc, wq, wckv, wqit, wmisc, wwit, kvn)],
        out_specs=[pl.BlockSpec(blk, im) for _, _, blk, im in out],
        out_shape=[jax.ShapeDtypeStruct(shp, dt) for shp, dt, _, _ in out],
        compiler_params=_params("parallel"),
        name="in_proj",
    )(x, wg, wz, wxbc, wq, wckv, wqit, wmisc, wwit, kvn)


def _ssd_body(xbc_ref, sz_ref, dt_ref, cw_ref, cb_ref, dtb_ref, alog_ref, dskip_ref, nw_ref, expand_ref,
              y_ref, state_ref, ext_ref):
    chunk = SSD_CHUNK
    n_sub = xbc_ref.shape[0] // chunk
    hp = SSD_HEAD_DIM
    gw = SSD_INNER // SSD_GROUPS

    @pl.when(pl.program_id(1) == 0)
    def _():
        state_ref[...] = jnp.zeros(state_ref.shape, F32)
        ext_ref[0:CONV_TAIL, :] = jnp.zeros((CONV_TAIL, SSD_XBC), BF16)

    ext_ref[CONV_TAIL:, :] = xbc_ref[...]
    t_i = lax.broadcasted_iota(jnp.int32, (chunk, CONV_TAIL + chunk), 0)
    j_i = lax.broadcasted_iota(jnp.int32, (chunk, CONV_TAIL + chunk), 1)
    r_i = lax.broadcasted_iota(jnp.int32, (chunk, chunk), 0)
    c_i = lax.broadcasted_iota(jnp.int32, (chunk, chunk), 1)
    causal = r_i >= c_i
    causal_b = causal.astype(BF16)
    eye = (lax.broadcasted_iota(jnp.int32, (SSD_HEADS, SSD_HEADS), 0)
           == lax.broadcasted_iota(jnp.int32, (SSD_HEADS, SSD_HEADS), 1)).astype(BF16)
    expand = expand_ref[...]
    neg_a = -jnp.exp(alog_ref[...])
    state = state_ref[...]

    for sub in range(n_sub):
        rows = slice(sub * chunk, (sub + 1) * chunk)
        ext = ext_ref[sub * chunk:sub * chunk + CONV_TAIL + chunk, :]
        conv = cb_ref[...] + cw_ref[SSD_CONV - 1:SSD_CONV, :] * xbc_ref[rows, :].astype(F32)
        for k in range(SSD_CONV - 1):
            shift = (j_i == t_i + (CONV_TAIL - (SSD_CONV - 1) + k)).astype(BF16)
            conv = conv + cw_ref[k:k + 1, :] * _dot(shift, ext)
        act = conv * jax.nn.sigmoid(conv)
        xs = act[:, :SSD_INNER]

        pre = dt_ref[rows, :] + dtb_ref[...]
        dt = jnp.maximum(pre, 0.0) + jnp.log(1.0 + jnp.exp(-jnp.abs(pre)))
        a = dt * neg_a
        a_cs = _select_sum(causal_b, _split3(a), lhs_is_01=True)
        a_cs_3 = _split3(a_cs)
        a_cs_t = sum(_dot_nt(eye, piece) for piece in a_cs_3)
        dt_e = _select_sum(expand, _split3(dt), lhs_is_01=False)
        a_cs_e = _select_sum(expand, a_cs_3, lhs_is_01=False)
        a_end_e = a_cs_e[chunk - 1:chunk, :]

        xdt = xs * dt_e
        xdt_b = xdt.astype(BF16)
        xdt_end_b = (xdt * jnp.exp(a_end_e - a_cs_e)).astype(BF16)
        state_b = state.astype(BF16)

        y_diag, y_off, new_state = [], [], []
        for g in range(SSD_GROUPS):
            bm = act[:, SSD_INNER + g * SSD_STATE:SSD_INNER + (g + 1) * SSD_STATE].astype(BF16)
            cm = act[:, SSD_INNER + SSD_GN + g * SSD_STATE:SSD_INNER + SSD_GN + (g + 1) * SSD_STATE].astype(BF16)
            cb = _dot_nt(cm, bm)
            y_off.append(_dot(cm, state_b[:, g * gw:(g + 1) * gw]))
            new_state.append(lax.dot_general(bm, xdt_end_b[:, g * gw:(g + 1) * gw], TN_DIMS,
                                             preferred_element_type=F32))
            for r in range(SSD_HEADS // SSD_GROUPS):
                h = g * (SSD_HEADS // SSD_GROUPS) + r
                seg = a_cs[:, h:h + 1] - a_cs_t[h:h + 1, :]
                decay = jnp.exp(jnp.where(causal, seg, NEG_BIG))
                y_diag.append(_dot((cb * decay).astype(BF16), xdt_b[:, h * hp:(h + 1) * hp]))
        state = jnp.exp(a_end_e) * state + jnp.concatenate(new_state, axis=1)

        y = jnp.concatenate(y_diag, axis=1) + jnp.concatenate(y_off, axis=1) * jnp.exp(a_cs_e)
        y = (y + dskip_ref[...] * xs) * sz_ref[rows, :].astype(F32)
        outs = []
        for g in range(SSD_GROUPS):
            yg = y[:, g * gw:(g + 1) * gw]
            outs.append(yg * lax.rsqrt(jnp.mean(yg * yg, axis=-1, keepdims=True) + RMS_EPS))
        y_ref[rows, :] = (jnp.concatenate(outs, axis=1) * nw_ref[...]).astype(BF16)

    state_ref[...] = state
    ext_ref[0:CONV_TAIL, :] = ext_ref[n_sub * chunk:n_sub * chunk + CONV_TAIL, :]


def _ssd(xbc, sz, dt, conv_w, conv_b, dt_bias, a_log, d_skip_e, norm_w, expand, bsz, seqlen):
    t = xbc.shape[0]
    n_sub = SSD_STEP_CHUNKS if (seqlen // SSD_CHUNK) % SSD_STEP_CHUNKS == 0 else 1
    rows = n_sub * SSD_CHUNK
    ns = seqlen // rows
    row = lambda b, c: (b * ns + c, 0)
    return pl.pallas_call(
        _ssd_body,
        grid=(bsz, ns),
        in_specs=[pl.BlockSpec((rows, SSD_XBC), row), pl.BlockSpec((rows, SSD_INNER), row),
                  pl.BlockSpec((rows, SSD_HEADS), row)]
                 + [_const_spec(w.shape) for w in (conv_w, conv_b, dt_bias, a_log, d_skip_e, norm_w, expand)],
        out_specs=pl.BlockSpec((rows, SSD_INNER), row),
        out_shape=jax.ShapeDtypeStruct((t, SSD_INNER), BF16),
        scratch_shapes=[pltpu.VMEM((SSD_STATE, SSD_INNER), F32),
                        pltpu.VMEM((CONV_TAIL + rows, SSD_XBC), BF16)],
        compiler_params=_params("arbitrary", "arbitrary"),
        name="ssd",
    )(xbc, sz, dt, conv_w, conv_b, dt_bias, a_log, d_skip_e, norm_w, expand)


def _score_of_key(key):
    return pltpu.bitcast(jnp.where(key >= 0, key, key ^ INT_MAX), F32)


def _dsa_body(slopes_ref, q_ref, qit_ref, wit_ref, ckv_ref, ki_ref, wuk_ref, wuv_ref, y_ref,
              sc_ref, sb_ref, bias_ref, ql_ref, s_ref, p_ref, m_ref, l_ref, alpha_ref, acc_ref, *, topk, pos_bits):
    qb = q_ref.shape[0]
    kc = KEY_CHUNK
    t0 = pl.program_id(1) * qb
    n_chunks = (t0 + qb + kc - 1) // kc
    q_pos = t0 + lax.broadcasted_iota(jnp.int32, (1, qb), 1)

    def key_pos(c):
        return c * kc + lax.broadcasted_iota(jnp.int32, (kc, qb), 0)

    def chunk_ds(c):
        return pl.ds(pl.multiple_of(c * kc, kc), kc)

    def score_chunk(c, carry):
        kblk = ki_ref[chunk_ds(c), :]
        score = jnp.zeros((kc, qb), F32)
        for h in range(0, IDX_HEADS, 2):
            w = jnp.concatenate([qit_ref[h * IDX_DIM:(h + 1) * IDX_DIM, :],
                                 qit_ref[(h + 1) * IDX_DIM:(h + 2) * IDX_DIM, :]], axis=1)
            wi = jnp.concatenate([wit_ref[h:h + 1, :], wit_ref[h + 1:h + 2, :]], axis=1)
            t = jnp.maximum(_dot(kblk, w), 0.0) * wi
            score = score + t[:, :qb] + t[:, qb:]
        score = jnp.where(key_pos(c) <= q_pos, score, -jnp.inf)
        sc_ref[chunk_ds(c), :] = score
        sb_ref[chunk_ds(c), :] = score.astype(BF16)
        return carry

    lax.fori_loop(0, n_chunks, score_chunk, 0)

    def count(pred):
        n_acc = 8

        def body(c, parts):
            hit = pred(sc_ref[chunk_ds(c), :], key_pos(c))
            parts = list(parts)
            for i in range(kc // SUBLANES):
                tile = hit[i * SUBLANES:(i + 1) * SUBLANES, :]
                parts[i % n_acc] = jnp.where(tile, parts[i % n_acc] + 1, parts[i % n_acc])
            return tuple(parts)
        parts = lax.fori_loop(0, n_chunks, body, (jnp.zeros((SUBLANES, qb), jnp.int32),) * n_acc)
        return jnp.sum(sum(parts), axis=0, keepdims=True)

    def count_rounded(cand_b):
        n_acc = 4

        def body(c, parts):
            hit = jnp.where(sb_ref[chunk_ds(c), :] >= cand_b, jnp.ones((kc, qb), BF16), jnp.zeros((kc, qb), BF16))
            parts = list(parts)
            for i in range(kc // BF16_ROWS):
                parts[i % n_acc] = parts[i % n_acc] + hit[i * BF16_ROWS:(i + 1) * BF16_ROWS, :]
            return tuple(parts)
        parts = lax.fori_loop(0, n_chunks, body, (jnp.zeros((BF16_ROWS, qb), BF16),) * n_acc)
        total = sum(p.astype(F32) for p in parts)
        return jnp.sum(total, axis=0, keepdims=True).astype(jnp.int32)

    def coarse_bit(i, key16):
        cand = key16 + (jnp.int32(1) << (15 - i))
        pattern = jnp.where(cand >= 0, cand, cand ^ 0x7FFF)
        cand_b = pltpu.bitcast(pattern << 16, F32).astype(BF16)
        return jnp.where(count_rounded(cand_b) >= topk, cand, key16)

    key16 = lax.fori_loop(0, 16, coarse_bit, jnp.full((1, qb), -(2 ** 15), jnp.int32))
    coarse = (key16 << 16) + jnp.where(key16 < 0, 0xFFFF, 0)
    window_lo = jnp.where(coarse < INT_MIN + 2 ** 16, INT_MIN, coarse - 2 ** 16)

    def thr_bit(i, state):
        thr, n_ge = state
        cand = thr + (jnp.int32(1) << (16 - i))
        cand_f = _score_of_key(cand)
        n = count(lambda s, p: s >= cand_f)
        return jnp.where(n >= topk, cand, thr), jnp.where(n >= topk, n, n_ge)

    def pending(state):
        return jnp.max(jnp.where(state[1] <= topk, 0, 1))

    window_lo_f = _score_of_key(window_lo)
    state = (window_lo, count(lambda s, p: s >= window_lo_f))
    state = lax.fori_loop(0, THR_BITS_ALWAYS, thr_bit, state)

    def more_bits(carry):
        i, state, _ = carry
        for b in range(THR_BITS_STEP):
            state = thr_bit(i + b, state)
        return i + THR_BITS_STEP, state, pending(state)

    _, (thr, _), _ = lax.while_loop(lambda carry: (carry[0] < 17) & (carry[2] > 0), more_bits,
                                    (jnp.int32(THR_BITS_ALWAYS), state, pending(state)))
    thr_f = jnp.where(thr >= LOWEST_FINITE_KEY, _score_of_key(jnp.maximum(thr, LOWEST_FINITE_KEY)), -jnp.inf)

    need = topk - count(lambda s, p: s > thr_f)
    n_tie = count(lambda s, p: (s == thr_f) & (p <= q_pos))

    def tie_cut():
        def cut_bit(i, cut):
            cand = cut + (jnp.int32(1) << (pos_bits - 1 - i))
            n_before = count(lambda s, p: (s == thr_f) & (p <= q_pos) & (p < cand))
            return jnp.where(n_before < need, cand, cut)
        return lax.fori_loop(0, pos_bits, cut_bit, jnp.zeros((1, qb), jnp.int32))

    eye = (lax.broadcasted_iota(jnp.int32, (qb, qb), 0)
           == lax.broadcasted_iota(jnp.int32, (qb, qb), 1)).astype(BF16)

    def write_bias(selected):
        def bias_chunk(c, carry):
            p = key_pos(c)
            sel = (p <= q_pos) & selected(sc_ref[chunk_ds(c), :], p)
            bias_ref[:, chunk_ds(c)] = _dot_nt(eye, jnp.where(sel, 0.0, NEG_BIG).astype(BF16))
            return carry
        lax.fori_loop(0, n_chunks, bias_chunk, 0)

    def bias_with_ties():
        cut = tie_cut()
        write_bias(lambda s, p: (s > thr_f) | ((s == thr_f) & (p <= cut)))

    def bias_no_ties():
        write_bias(lambda s, p: s >= thr_f)

    lax.cond(jnp.max(jnp.where(n_tie > need, 1, 0)) > 0, bias_with_ties, bias_no_ties)

    for h in range(ATT_HEADS):
        rows = slice(h * qb, (h + 1) * qb)
        ql = _dot(q_ref[:, h * ATT_HEAD_DIM:(h + 1) * ATT_HEAD_DIM], wuk_ref[h])
        ql_ref[rows, :] = (ql * (ATT_HEAD_DIM ** -0.5 * LOG2E)).astype(BF16)
        m_ref[rows, :] = jnp.full((qb, LANES), NEG_BIG, F32)
        l_ref[rows, :] = jnp.zeros((qb, LANES), F32)
        acc_ref[rows, :] = jnp.zeros((qb, KV_RANK), F32)
    lane_tiles = kc // LANES
    heads_per_part = ATT_HEADS // MXU_SPLIT

    part_rows = [slice(r * heads_per_part * qb, (r + 1) * heads_per_part * qb) for r in range(MXU_SPLIT)]

    def logits(part, c):
        cblk = ckv_ref[chunk_ds(c), :]
        rel = (c * kc - t0 + lax.broadcasted_iota(jnp.int32, (1, kc), 1)).astype(F32)
        s_part = _dot_nt(ql_ref[part_rows[part], :], cblk)
        for i_h in range(heads_per_part):
            h = part * heads_per_part + i_h
            rows = slice(h * qb, (h + 1) * qb)
            s = s_part[i_h * qb:(i_h + 1) * qb, :] + (slopes_ref[h] * LOG2E) * rel + bias_ref[:, chunk_ds(c)]
            s_ref[rows, :] = s
            mx = s[:, 0:LANES]
            for i in range(1, lane_tiles):
                mx = jnp.maximum(mx, s[:, i * LANES:(i + 1) * LANES])
            m_old = m_ref[rows, :]
            m_new = jnp.maximum(m_old, jnp.max(mx, axis=-1, keepdims=True))
            m_ref[rows, :] = m_new
            alpha_ref[rows, :] = jnp.exp2(m_old - m_new)

    def values(part, c):
        cblk = ckv_ref[chunk_ds(c), :]
        for i_h in range(heads_per_part):
            rows = slice((part * heads_per_part + i_h) * qb, (part * heads_per_part + i_h + 1) * qb)
            m = m_ref[rows, :]
            l = alpha_ref[rows, :] * l_ref[rows, :]
            for i in range(lane_tiles):
                p = jnp.exp2(s_ref[rows, i * LANES:(i + 1) * LANES] - m)
                l = l + p
                p_ref[rows, i * LANES:(i + 1) * LANES] = p.astype(BF16)
            l_ref[rows, :] = l
        upd = _dot(p_ref[part_rows[part], :], cblk)
        for i_h in range(heads_per_part):
            rows = slice((part * heads_per_part + i_h) * qb, (part * heads_per_part + i_h + 1) * qb)
            alpha = alpha_ref[rows, :]
            for i in range(KV_RANK // LANES):
                cols = slice(i * LANES, (i + 1) * LANES)
                acc_ref[rows, cols] = alpha * acc_ref[rows, cols] + upd[i_h * qb:(i_h + 1) * qb, cols]

    last = n_chunks - 1
    for part in range(MXU_SPLIT):
        logits(part, 0)

    def att_chunk(c, carry):
        for part in range(MXU_SPLIT):
            values(part, c)
            logits(part, c + 1)
        return carry

    lax.fori_loop(0, last, att_chunk, 0)
    for part in range(MXU_SPLIT):
        values(part, last)

    o = (acc_ref[...] / jnp.sum(l_ref[...], axis=-1, keepdims=True)).astype(BF16)
    y_ref[...] = jnp.concatenate([_dot(o[h * qb:(h + 1) * qb, :], wuv_ref[h]) for h in range(ATT_HEADS)],
                                 axis=1).astype(BF16)


def _dsa(q, qit, wit, ckv, ki, w_uk, w_uv, slopes, bsz, seqlen):
    t = q.shape[0]
    qb = Q_BLOCK
    nq = seqlen // qb
    rows = ATT_HEADS * qb
    topk = min(TOPK_MAX, seqlen // 4)
    pos_bits = max(1, (seqlen - 1).bit_length())
    seq_pad = -(-seqlen // KEY_CHUNK) * KEY_CHUNK
    row = lambda b, j: (b * nq + j, 0)
    col = lambda b, j: (0, b * nq + j)
    per_batch = lambda b, j: (b, 0, 0)
    if seq_pad != seqlen:
        ckv = jnp.pad(ckv.reshape(bsz, seqlen, KV_RANK), ((0, 0), (0, seq_pad - seqlen), (0, 0)))
        ki = jnp.pad(ki.reshape(bsz, seqlen, IDX_DIM), ((0, 0), (0, seq_pad - seqlen), (0, 0)))
    body = functools.partial(_dsa_body, topk=topk, pos_bits=pos_bits)
    return pl.pallas_call(
        body,
        grid=(bsz, nq),
        in_specs=[pl.BlockSpec(memory_space=pltpu.SMEM),
                  pl.BlockSpec((qb, ATT_WIDTH), row), pl.BlockSpec((IDX_HEADS * IDX_DIM, qb), col),
                  pl.BlockSpec((IDX_HEADS, qb), col),
                  pl.BlockSpec((None, seq_pad, KV_RANK), per_batch, pipeline_mode=pl.Buffered(1)),
                  pl.BlockSpec((None, seq_pad, IDX_DIM), per_batch, pipeline_mode=pl.Buffered(1)),
                  _const_spec(w_uk.shape), _const_spec(w_uv.shape)],
        out_specs=pl.BlockSpec((qb, ATT_WIDTH), row),
        out_shape=jax.ShapeDtypeStruct((t, ATT_WIDTH), BF16),
        scratch_shapes=[pltpu.VMEM((seq_pad, qb), F32), pltpu.VMEM((seq_pad, qb), BF16),
                        pltpu.VMEM((qb, seq_pad), F32),
                        pltpu.VMEM((rows, KV_RANK), BF16),
                        pltpu.VMEM((rows, KEY_CHUNK), F32), pltpu.VMEM((rows, KEY_CHUNK), BF16),
                        pltpu.VMEM((rows, LANES), F32), pltpu.VMEM((rows, LANES), F32),
                        pltpu.VMEM((rows, LANES), F32), pltpu.VMEM((rows, KV_RANK), F32)],
        compiler_params=_params("arbitrary", "arbitrary"),
        name="dsa",
    )(slopes, q, qit, wit, ckv.reshape(bsz, seq_pad, KV_RANK), ki.reshape(bsz, seq_pad, IDX_DIM), w_uk, w_uv)


def _merge_ln_body(x_ref, ys_ref, ya_ref, sg_ref, wps_ref, wpa_ref, wo_ref, g_ref, b_ref, o_ref):
    sg = sg_ref[...].astype(F32)
    merged = (sg[:, :D_MODEL] * _dot(ys_ref[...], wps_ref[...])
              + sg[:, D_MODEL:] * _dot(ya_ref[...], wpa_ref[...]))
    out = _dot(merged.astype(BF16), wo_ref[...])
    o_ref[...] = _layer_norm(ALPHA * x_ref[...] + out, g_ref[...], b_ref[...])


def _merge_ln(x, ys, ya, sg, wps, wpa, wo, g, b, tm):
    t = x.shape[0]
    row = lambda i: (i, 0)
    return pl.pallas_call(
        _merge_ln_body,
        grid=(t // tm,),
        in_specs=[pl.BlockSpec((tm, D_MODEL), row), pl.BlockSpec((tm, SSD_INNER), row),
                  pl.BlockSpec((tm, ATT_WIDTH), row), pl.BlockSpec((tm, 2 * D_MODEL), row)]
                 + [_const_spec(w.shape) for w in (wps, wpa, wo, g, b)],
        out_specs=pl.BlockSpec((tm, D_MODEL), row),
        out_shape=jax.ShapeDtypeStruct((t, D_MODEL), F32),
        compiler_params=_params("parallel"),
        name="merge_ln",
    )(x, ys, ya, sg, wps, wpa, wo, g, b)


def _mem_kv_body(m_ref, w_ref, o_ref):
    o_ref[...] = _dot(m_ref[...].astype(BF16), w_ref[...]).astype(BF16)


def _mem_kv(mem, w_mkv, bsz, mem_len):
    row = lambda b: (b, 0)
    return pl.pallas_call(
        _mem_kv_body,
        grid=(bsz,),
        in_specs=[pl.BlockSpec((mem_len, D_MODEL), row), _const_spec(w_mkv.shape)],
        out_specs=pl.BlockSpec((mem_len, 2 * D_MODEL), row),
        out_shape=jax.ShapeDtypeStruct((bsz * mem_len, 2 * D_MODEL), BF16),
        compiler_params=_params("parallel"),
        name="mem_kv",
    )(mem, w_mkv)


def _xattn_ln_body(x_ref, kv_ref, wq_ref, wo_ref, g_ref, b_ref, o_ref):
    x = x_ref[...]
    q = (_dot(x.astype(BF16), wq_ref[...]) * (MEM_HEAD_DIM ** -0.5)).astype(BF16)
    heads = []
    for h in range(MEM_HEADS):
        lo = h * MEM_HEAD_DIM
        s = _dot_nt(q[:, lo:lo + MEM_HEAD_DIM], kv_ref[:, lo:lo + MEM_HEAD_DIM])
        p = jnp.exp(s - jnp.max(s, axis=-1, keepdims=True))
        o = _dot(p.astype(BF16), kv_ref[:, D_MODEL + lo:D_MODEL + lo + MEM_HEAD_DIM])
        heads.append((o / jnp.sum(p, axis=-1, keepdims=True)).astype(BF16))
    out = _dot(jnp.concatenate(heads, axis=1), wo_ref[...])
    o_ref[...] = _layer_norm(ALPHA * x + out, g_ref[...], b_ref[...])


def _xattn_ln(x, kv, wq, wo, g, b, bsz, seqlen, mem_len, tm):
    t = x.shape[0]
    nt = seqlen // tm
    return pl.pallas_call(
        _xattn_ln_body,
        grid=(bsz, nt),
        in_specs=[pl.BlockSpec((tm, D_MODEL), lambda b_, i: (b_ * nt + i, 0)),
                  pl.BlockSpec((mem_len, 2 * D_MODEL), lambda b_, i: (b_, 0))]
                 + [_const_spec(w.shape) for w in (wq, wo, g, b)],
        out_specs=pl.BlockSpec((tm, D_MODEL), lambda b_, i: (b_ * nt + i, 0)),
        out_shape=jax.ShapeDtypeStruct((t, D_MODEL), F32),
        compiler_params=_params("parallel", "parallel"),
        name="xattn_ln",
    )(x, kv, wq, wo, g, b)


def _split_w_in(w_in):
    sizes = (D_MODEL, D_MODEL, SSD_INNER, SSD_XBC, SSD_HEADS, ATT_WIDTH, KV_RANK, IDX_HEADS * IDX_DIM,
             IDX_DIM, IDX_HEADS)
    parts, off = [], 0
    for s in sizes:
        parts.append(w_in[:, off:off + s])
        off += s
    g_ssd, g_att, z, xbc, dt, q, ckv, qi, ki, wi = parts
    misc = jnp.concatenate([dt, ki, jnp.zeros((w_in.shape[0], LANES - SSD_HEADS - IDX_DIM), w_in.dtype)], axis=1)
    return [w.astype(BF16) for w in (jnp.concatenate([g_ssd, g_att], axis=1), z, xbc, q, ckv, qi.T, misc, wi.T)]


def _layer(x, mem, bsz, seqlen, mem_len, p):
    (ffn1_w_in, ffn1_w_out, ln1_g, ln1_b, w_in, conv_w, conv_b, dt_bias, a_log, d_skip, ssd_norm_w,
     kv_norm_w, w_uk, w_uv, w_proj_ssd, w_proj_att, w_out, ln2_g, ln2_b,
     w_mq, w_mkv, w_mo, ln3_g, ln3_b, ffn2_w_in, ffn2_w_out, ln4_g, ln4_b) = p
    tm = min(TOKEN_TILE, seqlen)
    vec = lambda v: v.reshape(1, -1)
    bf = lambda w: w.astype(BF16)

    x = _ffn_ln(x, bf(ffn1_w_in), bf(ffn1_w_out), vec(ln1_g), vec(ln1_b), tm)

    sg, sz, xbc, q, ckv, qit, ki, dt, wit = _in_proj(x, *_split_w_in(w_in), vec(kv_norm_w), tm)

    expand = jnp.repeat(jnp.eye(SSD_HEADS, dtype=BF16), SSD_HEAD_DIM, axis=1)
    y_ssd = _ssd(xbc, sz, dt, conv_w, vec(conv_b), vec(dt_bias), vec(a_log),
                 vec(jnp.repeat(d_skip, SSD_HEAD_DIM)), vec(ssd_norm_w), expand, bsz, seqlen)

    slopes = 2.0 ** (-8.0 * jnp.arange(1, ATT_HEADS + 1, dtype=F32) / ATT_HEADS)
    y_att = _dsa(q, qit, wit, ckv, ki, bf(w_uk), bf(w_uv), slopes, bsz, seqlen)

    x = _merge_ln(x, y_ssd, y_att, sg, bf(w_proj_ssd), bf(w_proj_att), bf(w_out), vec(ln2_g), vec(ln2_b), tm)

    kv = _mem_kv(mem, bf(w_mkv), bsz, mem_len)
    x = _xattn_ln(x, kv, bf(w_mq), bf(w_mo), vec(ln3_g), vec(ln3_b), bsz, seqlen, mem_len, tm)

    return _ffn_ln(x, bf(ffn2_w_in), bf(ffn2_w_out), vec(ln4_g), vec(ln4_b), tm)


def kernel(x, mem, ffn1_w_in, ffn1_w_out, ln1_g, ln1_b, w_in, conv_w, conv_b, dt_bias, a_log, d_skip,
           ssd_norm_w, kv_norm_w, w_uk, w_uv, w_proj_ssd, w_proj_att, w_out, ln2_g, ln2_b,
           w_mq, w_mkv, w_mo, ln3_g, ln3_b, ffn2_w_in, ffn2_w_out, ln4_g, ln4_b):
    bsz, seqlen, d = x.shape
    mem_len = mem.shape[1]
    params = (ffn1_w_in, ffn1_w_out, ln1_g, ln1_b, w_in, conv_w, conv_b, dt_bias, a_log, d_skip, ssd_norm_w,
              kv_norm_w, w_uk, w_uv, w_proj_ssd, w_proj_att, w_out, ln2_g, ln2_b,
              w_mq, w_mkv, w_mo, ln3_g, ln3_b, ffn2_w_in, ffn2_w_out, ln4_g, ln4_b)
    h = x.reshape(bsz * seqlen, d)
    m = mem.reshape(bsz * mem_len, d)
    for layer in range(ffn1_w_in.shape[0]):
        h = _layer(h, m, bsz, seqlen, mem_len, tuple(w[layer] for w in params))
    return h.reshape(bsz, seqlen, d)
```

```python
import functools
import math

import jax
import jax.numpy as jnp
from jax import lax
from jax.experimental import pallas as pl
from jax.experimental.pallas import tpu as pltpu

F32 = jnp.float32
BF16 = jnp.bfloat16

D_MODEL = 1024
DEPTH = 1
SSD_HEADS = 16
SSD_HEAD_DIM = 64
SSD_INNER = SSD_HEADS * SSD_HEAD_DIM
SSD_GROUPS = 4
SSD_STATE = 128
SSD_CONV = 4
SSD_CHUNK = 128
SSD_STEP_CHUNKS = 2
SSD_GN = SSD_GROUPS * SSD_STATE
SSD_XBC = SSD_INNER + 2 * SSD_GN
ATT_HEADS = 16
ATT_HEAD_DIM = 64
ATT_WIDTH = ATT_HEADS * ATT_HEAD_DIM
KV_RANK = 256
IDX_HEADS = 16
IDX_DIM = 64
TOPK_MAX = 256
Q_BLOCK = 256
MEM_HEADS = 4
MEM_HEAD_DIM = D_MODEL // MEM_HEADS
D_FF = 2816
ALPHA = (2.0 * DEPTH) ** 0.25
LN_EPS = 1e-5
RMS_EPS = 1e-6

INT_MIN = -(2 ** 31)
INT_MAX = 2 ** 31 - 1
LOWEST_FINITE_KEY = INT_MIN + 2 ** 23
NEG_BIG = -1e30
LOG2E = math.log2(math.e)

LANES = 128
SUBLANES = 8
BF16_ROWS = 16
TOKEN_TILE = 512
FFN_CHUNK = 1408
KEY_CHUNK = 512
THR_BITS_ALWAYS = 9
THR_BITS_STEP = 2
MXU_SPLIT = 4
CONV_TAIL = 16
VMEM_LIMIT = 56 * 1024 * 1024

NT_DIMS = (((1,), (1,)), ((), ()))
TN_DIMS = (((0,), (0,)), ((), ()))


def _const_spec(shape):
    return pl.BlockSpec(shape, lambda *_: (0,) * len(shape), pipeline_mode=pl.Buffered(1))


def _params(*sem):
    return pltpu.CompilerParams(dimension_semantics=sem, vmem_limit_bytes=VMEM_LIMIT)


def _layer_norm(y, g, b):
    mu = jnp.mean(y, axis=-1, keepdims=True)
    d = y - mu
    var = jnp.mean(d * d, axis=-1, keepdims=True)
    return d * lax.rsqrt(var + LN_EPS) * g + b


def _dot(a, b):
    return jnp.dot(a, b, preferred_element_type=F32)


def _dot_nt(a, b):
    return lax.dot_general(a, b, NT_DIMS, preferred_element_type=F32)


def _split3(x):
    hi = x.astype(BF16)
    rest = x - hi.astype(F32)
    mid = rest.astype(BF16)
    return hi, mid, (rest - mid.astype(F32)).astype(BF16)


def _select_sum(sel01, pieces, lhs_is_01):
    if lhs_is_01:
        return sum(_dot(sel01, p) for p in pieces)
    return sum(_dot(p, sel01) for p in pieces)


def _ffn_ln_body(x_ref, win_ref, wout_ref, g_ref, b_ref, o_ref):
    x = x_ref[...]
    xb = x.astype(BF16)
    acc = jnp.zeros(x.shape, F32)
    for c in range(D_FF // FFN_CHUNK):
        lo = c * FFN_CHUNK
        gate = _dot(xb, win_ref[:, lo:lo + FFN_CHUNK])
        up = _dot(xb, win_ref[:, D_FF + lo:D_FF + lo + FFN_CHUNK])
        act = (gate * jax.nn.sigmoid(gate) * up).astype(BF16)
        acc = acc + _dot(act, wout_ref[lo:lo + FFN_CHUNK, :])
    o_ref[...] = _layer_norm(ALPHA * x + 0.5 * acc, g_ref[...], b_ref[...])


def _ffn_ln(x, w_in, w_out, g, b, tm):
    t = x.shape[0]
    row = lambda i: (i, 0)
    return pl.pallas_call(
        _ffn_ln_body,
        grid=(t // tm,),
        in_specs=[pl.BlockSpec((tm, D_MODEL), row),
                  _const_spec(w_in.shape), _const_spec(w_out.shape),
                  _const_spec(g.shape), _const_spec(b.shape)],
        out_specs=pl.BlockSpec((tm, D_MODEL), row),
        out_shape=jax.ShapeDtypeStruct((t, D_MODEL), F32),
        compiler_params=_params("parallel"),
        name="ffn_ln",
    )(x, w_in, w_out, g, b)


def _in_proj_body(x_ref, wg_ref, wz_ref, wxbc_ref, wq_ref, wckv_ref, wqit_ref, wmisc_ref, wwit_ref, kvn_ref,
                  sg_ref, sz_ref, xbc_ref, q_ref, ckv_ref, qit_ref, ki_ref, dt_ref, wit_ref):
    xb = x_ref[...].astype(BF16)
    sg_ref[...] = jax.nn.sigmoid(_dot(xb, wg_ref[...])).astype(BF16)
    z = _dot(xb, wz_ref[...])
    sz_ref[...] = (z * jax.nn.sigmoid(z)).astype(BF16)
    xbc_ref[...] = _dot(xb, wxbc_ref[...]).astype(BF16)
    q_ref[...] = _dot(xb, wq_ref[...]).astype(BF16)
    c = _dot(xb, wckv_ref[...])
    c = c * lax.rsqrt(jnp.mean(c * c, axis=-1, keepdims=True) + RMS_EPS) * kvn_ref[...]
    ckv_ref[...] = c.astype(BF16)
    qit_ref[...] = _dot_nt(wqit_ref[...], xb).astype(BF16)
    wit_ref[...] = _dot_nt(wwit_ref[...], xb) * (IDX_HEADS ** -0.5)
    misc = _dot(xb, wmisc_ref[...])
    dt_ref[...] = misc[:, 0:SSD_HEADS]
    ki_ref[...] = misc[:, SSD_HEADS:SSD_HEADS + IDX_DIM].astype(BF16)


def _in_proj(x, wg, wz, wxbc, wq, wckv, wqit, wmisc, wwit, kvn, tm):
    t = x.shape[0]
    row = lambda i: (i, 0)
    col = lambda i: (0, i)
    out = [
        ((t, 2 * D_MODEL), BF16, (tm, 2 * D_MODEL), row),
        ((t, SSD_INNER), BF16, (tm, SSD_INNER), row),
        ((t, SSD_XBC), BF16, (tm, SSD_XBC), row),
        ((t, ATT_WIDTH), BF16, (tm, ATT_WIDTH), row),
        ((t, KV_RANK), BF16, (tm, KV_RANK), row),
        ((IDX_HEADS * IDX_DIM, t), BF16, (IDX_HEADS * IDX_DIM, tm), col),
        ((t, IDX_DIM), BF16, (tm, IDX_DIM), row),
        ((t, SSD_HEADS), F32, (tm, SSD_HEADS), row),
        ((IDX_HEADS, t), F32, (IDX_HEADS, tm), col),
    ]
    return pl.pallas_call(
        _in_proj_body,
        grid=(t // tm,),
        in_specs=[pl.BlockSpec((tm, D_MODEL), row)] + [_const_spec(w.shape) for w in
                                                        (wg, wz, wxbc, wq, wckv, wqit, wmisc, wwit, kvn)],
        out_specs=[pl.BlockSpec(blk, im) for _, _, blk, im in out],
        out_shape=[jax.ShapeDtypeStruct(shp, dt) for shp, dt, _, _ in out],
        compiler_params=_params("parallel"),
        name="in_proj",
    )(x, wg, wz, wxbc, wq, wckv, wqit, wmisc, wwit, kvn)


def _ssd_body(xbc_ref, sz_ref, dt_ref, cw_ref, cb_ref, dtb_ref, alog_ref, dskip_ref, nw_ref, expand_ref,
              y_ref, state_ref, ext_ref):
    chunk = SSD_CHUNK
    n_sub = xbc_ref.shape[0] // chunk
    hp = SSD_HEAD_DIM
    gw = SSD_INNER // SSD_GROUPS

    @pl.when(pl.program_id(1) == 0)
    def _():
        state_ref[...] = jnp.zeros(state_ref.shape, F32)
        ext_ref[0:CONV_TAIL, :] = jnp.zeros((CONV_TAIL, SSD_XBC), BF16)

    ext_ref[CONV_TAIL:, :] = xbc_ref[...]
    t_i = lax.broadcasted_iota(jnp.int32, (chunk, CONV_TAIL + chunk), 0)
    j_i = lax.broadcasted_iota(jnp.int32, (chunk, CONV_TAIL + chunk), 1)
    r_i = lax.broadcasted_iota(jnp.int32, (chunk, chunk), 0)
    c_i = lax.broadcasted_iota(jnp.int32, (chunk, chunk), 1)
    causal = r_i >= c_i
    causal_b = causal.astype(BF16)
    eye = (lax.broadcasted_iota(jnp.int32, (SSD_HEADS, SSD_HEADS), 0)
           == lax.broadcasted_iota(jnp.int32, (SSD_HEADS, SSD_HEADS), 1)).astype(BF16)
    expand = expand_ref[...]
    neg_a = -jnp.exp(alog_ref[...])
    state = state_ref[...]

    for sub in range(n_sub):
        rows = slice(sub * chunk, (sub + 1) * chunk)
        ext = ext_ref[sub * chunk:sub * chunk + CONV_TAIL + chunk, :]
        conv = cb_ref[...] + cw_ref[SSD_CONV - 1:SSD_CONV, :] * xbc_ref[rows, :].astype(F32)
        for k in range(SSD_CONV - 1):
            shift = (j_i == t_i + (CONV_TAIL - (SSD_CONV - 1) + k)).astype(BF16)
            conv = conv + cw_ref[k:k + 1, :] * _dot(shift, ext)
        act = conv * jax.nn.sigmoid(conv)
        xs = act[:, :SSD_INNER]

        pre = dt_ref[rows, :] + dtb_ref[...]
        dt = jnp.maximum(pre, 0.0) + jnp.log(1.0 + jnp.exp(-jnp.abs(pre)))
        a = dt * neg_a
        a_cs = _select_sum(causal_b, _split3(a), lhs_is_01=True)
        a_cs_3 = _split3(a_cs)
        a_cs_t = sum(_dot_nt(eye, piece) for piece in a_cs_3)
        dt_e = _select_sum(expand, _split3(dt), lhs_is_01=False)
        a_cs_e = _select_sum(expand, a_cs_3, lhs_is_01=False)
        a_end_e = a_cs_e[chunk - 1:chunk, :]

        xdt = xs * dt_e
        xdt_b = xdt.astype(BF16)
        xdt_end_b = (xdt * jnp.exp(a_end_e - a_cs_e)).astype(BF16)
        state_b = state.astype(BF16)

        y_diag, y_off, new_state = [], [], []
        for g in range(SSD_GROUPS):
            bm = act[:, SSD_INNER + g * SSD_STATE:SSD_INNER + (g + 1) * SSD_STATE].astype(BF16)
            cm = act[:, SSD_INNER + SSD_GN + g * SSD_STATE:SSD_INNER + SSD_GN + (g + 1) * SSD_STATE].astype(BF16)
            cb = _dot_nt(cm, bm)
            y_off.append(_dot(cm, state_b[:, g * gw:(g + 1) * gw]))
            new_state.append(lax.dot_general(bm, xdt_end_b[:, g * gw:(g + 1) * gw], TN_DIMS,
                                             preferred_element_type=F32))
            for r in range(SSD_HEADS // SSD_GROUPS):
                h = g * (SSD_HEADS // SSD_GROUPS) + r
                seg = a_cs[:, h:h + 1] - a_cs_t[h:h + 1, :]
                decay = jnp.exp(jnp.where(causal, seg, NEG_BIG))
                y_diag.append(_dot((cb * decay).astype(BF16), xdt_b[:, h * hp:(h + 1) * hp]))
        state = jnp.exp(a_end_e) * state + jnp.concatenate(new_state, axis=1)

        y = jnp.concatenate(y_diag, axis=1) + jnp.concatenate(y_off, axis=1) * jnp.exp(a_cs_e)
        y = (y + dskip_ref[...] * xs) * sz_ref[rows, :].astype(F32)
        outs = []
        for g in range(SSD_GROUPS):
            yg = y[:, g * gw:(g + 1) * gw]
            outs.append(yg * lax.rsqrt(jnp.mean(yg * yg, axis=-1, keepdims=True) + RMS_EPS))
        y_ref[rows, :] = (jnp.concatenate(outs, axis=1) * nw_ref[...]).astype(BF16)

    state_ref[...] = state
    ext_ref[0:CONV_TAIL, :] = ext_ref[n_sub * chunk:n_sub * chunk + CONV_TAIL, :]


def _ssd(xbc, sz, dt, conv_w, conv_b, dt_bias, a_log, d_skip_e, norm_w, expand, bsz, seqlen):
    t = xbc.shape[0]
    n_sub = SSD_STEP_CHUNKS if (seqlen // SSD_CHUNK) % SSD_STEP_CHUNKS == 0 else 1
    rows = n_sub * SSD_CHUNK
    ns = seqlen // rows
    row = lambda b, c: (b * ns + c, 0)
    return pl.pallas_call(
        _ssd_body,
        grid=(bsz, ns),
        in_specs=[pl.BlockSpec((rows, SSD_XBC), row), pl.BlockSpec((rows, SSD_INNER), row),
                  pl.BlockSpec((rows, SSD_HEADS), row)]
                 + [_const_spec(w.shape) for w in (conv_w, conv_b, dt_bias, a_log, d_skip_e, norm_w, expand)],
        out_specs=pl.BlockSpec((rows, SSD_INNER), row),
        out_shape=jax.ShapeDtypeStruct((t, SSD_INNER), BF16),
        scratch_shapes=[pltpu.VMEM((SSD_STATE, SSD_INNER), F32),
                        pltpu.VMEM((CONV_TAIL + rows, SSD_XBC), BF16)],
        compiler_params=_params("arbitrary", "arbitrary"),
        name="ssd",
    )(xbc, sz, dt, conv_w, conv_b, dt_bias, a_log, d_skip_e, norm_w, expand)


def _score_of_key(key):
    return pltpu.bitcast(jnp.where(key >= 0, key, key ^ INT_MAX), F32)


def _dsa_body(slopes_ref, q_ref, qit_ref, wit_ref, ckv_ref, ki_ref, wuk_ref, wuv_ref, y_ref,
              sc_ref, sb_ref, bias_ref, ql_ref, s_ref, p_ref, m_ref, l_ref, alpha_ref, acc_ref, *, topk, pos_bits):
    qb = q_ref.shape[0]
    kc = KEY_CHUNK
    t0 = pl.program_id(1) * qb
    n_chunks = (t0 + qb + kc - 1) // kc
    q_pos = t0 + lax.broadcasted_iota(jnp.int32, (1, qb), 1)

    def key_pos(c):
        return c * kc + lax.broadcasted_iota(jnp.int32, (kc, qb), 0)

    def chunk_ds(c):
        return pl.ds(pl.multiple_of(c * kc, kc), kc)

    def score_chunk(c, carry):
        kblk = ki_ref[chunk_ds(c), :]
        score = jnp.zeros((kc, qb), F32)
        for h in range(0, IDX_HEADS, 2):
            w = jnp.concatenate([qit_ref[h * IDX_DIM:(h + 1) * IDX_DIM, :],
                                 qit_ref[(h + 1) * IDX_DIM:(h + 2) * IDX_DIM, :]], axis=1)
            wi = jnp.concatenate([wit_ref[h:h + 1, :], wit_ref[h + 1:h + 2, :]], axis=1)
            t = jnp.maximum(_dot(kblk, w), 0.0) * wi
            score = score + t[:, :qb] + t[:, qb:]
        score = jnp.where(key_pos(c) <= q_pos, score, -jnp.inf)
        sc_ref[chunk_ds(c), :] = score
        sb_ref[chunk_ds(c), :] = score.astype(BF16)
        return carry

    lax.fori_loop(0, n_chunks, score_chunk, 0)

    def count(pred):
        n_acc = 8

        def body(c, parts):
            hit = pred(sc_ref[chunk_ds(c), :], key_pos(c))
            parts = list(parts)
            for i in range(kc // SUBLANES):
                tile = hit[i * SUBLANES:(i + 1) * SUBLANES, :]
                parts[i % n_acc] = jnp.where(tile, parts[i % n_acc] + 1, parts[i % n_acc])
            return tuple(parts)
        parts = lax.fori_loop(0, n_chunks, body, (jnp.zeros((SUBLANES, qb), jnp.int32),) * n_acc)
        return jnp.sum(sum(parts), axis=0, keepdims=True)

    def count_rounded(cand_b):
        n_acc = 4

        def body(c, parts):
            hit = jnp.where(sb_ref[chunk_ds(c), :] >= cand_b, jnp.ones((kc, qb), BF16), jnp.zeros((kc, qb), BF16))
            parts = list(parts)
            for i in range(kc // BF16_ROWS):
                parts[i % n_acc] = parts[i % n_acc] + hit[i * BF16_ROWS:(i + 1) * BF16_ROWS, :]
            return tuple(parts)
        parts = lax.fori_loop(0, n_chunks, body, (jnp.zeros((BF16_ROWS, qb), BF16),) * n_acc)
        total = sum(p.astype(F32) for p in parts)
        return jnp.sum(total, axis=0, keepdims=True).astype(jnp.int32)

    def coarse_bit(i, key16):
        cand = key16 + (jnp.int32(1) << (15 - i))
        pattern = jnp.where(cand >= 0, cand, cand ^ 0x7FFF)
        cand_b = pltpu.bitcast(pattern << 16, F32).astype(BF16)
        return jnp.where(count_rounded(cand_b) >= topk, cand, key16)

    key16 = lax.fori_loop(0, 16, coarse_bit, jnp.full((1, qb), -(2 ** 15), jnp.int32))
    coarse = (key16 << 16) + jnp.where(key16 < 0, 0xFFFF, 0)
    window_lo = jnp.where(coarse < INT_MIN + 2 ** 16, INT_MIN, coarse - 2 ** 16)

    def thr_bit(i, state):
        thr, n_ge = state
        cand = thr + (jnp.int32(1) << (16 - i))
        cand_f = _score_of_key(cand)
        n = count(lambda s, p: s >= cand_f)
        return jnp.where(n >= topk, cand, thr), jnp.where(n >= topk, n, n_ge)

    def pending(state):
        return jnp.max(jnp.where(state[1] <= topk, 0, 1))

    window_lo_f = _score_of_key(window_lo)
    state = (window_lo, count(lambda s, p: s >= window_lo_f))
    state = lax.fori_loop(0, THR_BITS_ALWAYS, thr_bit, state)

    def more_bits(carry):
        i, state, _ = carry
        for b in range(THR_BITS_STEP):
            state = thr_bit(i + b, state)
        return i + THR_BITS_STEP, state, pending(state)

    _, (thr, _), _ = lax.while_loop(lambda carry: (carry[0] < 17) & (carry[2] > 0), more_bits,
                                    (jnp.int32(THR_BITS_ALWAYS), state, pending(state)))
    thr_f = jnp.where(thr >= LOWEST_FINITE_KEY, _score_of_key(jnp.maximum(thr, LOWEST_FINITE_KEY)), -jnp.inf)

    need = topk - count(lambda s, p: s > thr_f)
    n_tie = count(lambda s, p: (s == thr_f) & (p <= q_pos))

    def tie_cut():
        def cut_bit(i, cut):
            cand = cut + (jnp.int32(1) << (pos_bits - 1 - i))
            n_before = count(lambda s, p: (s == thr_f) & (p <= q_pos) & (p < cand))
            return jnp.where(n_before < need, cand, cut)
        return lax.fori_loop(0, pos_bits, cut_bit, jnp.zeros((1, qb), jnp.int32))

    eye = (lax.broadcasted_iota(jnp.int32, (qb, qb), 0)
           == lax.broadcasted_iota(jnp.int32, (qb, qb), 1)).astype(BF16)

    def write_bias(selected):
        def bias_chunk(c, carry):
            p = key_pos(c)
            sel = (p <= q_pos) & selected(sc_ref[chunk_ds(c), :], p)
            bias_ref[:, chunk_ds(c)] = _dot_nt(eye, jnp.where(sel, 0.0, NEG_BIG).astype(BF16))
            return carry
        lax.fori_loop(0, n_chunks, bias_chunk, 0)

    def bias_with_ties():
        cut = tie_cut()
        write_bias(lambda s, p: (s > thr_f) | ((s == thr_f) & (p <= cut)))

    def bias_no_ties():
        write_bias(lambda s, p: s >= thr_f)

    lax.cond(jnp.max(jnp.where(n_tie > need, 1, 0)) > 0, bias_with_ties, bias_no_ties)

    for h in range(ATT_HEADS):
        rows = slice(h * qb, (h + 1) * qb)
        ql = _dot(q_ref[:, h * ATT_HEAD_DIM:(h + 1) * ATT_HEAD_DIM], wuk_ref[h])
        ql_ref[rows, :] = (ql * (ATT_HEAD_DIM ** -0.5 * LOG2E)).astype(BF16)
        m_ref[rows, :] = jnp.full((qb, LANES), NEG_BIG, F32)
        l_ref[rows, :] = jnp.zeros((qb, LANES), F32)
        acc_ref[rows, :] = jnp.zeros((qb, KV_RANK), F32)
    lane_tiles = kc // LANES
    heads_per_part = ATT_HEADS // MXU_SPLIT

    part_rows = [slice(r * heads_per_part * qb, (r + 1) * heads_per_part * qb) for r in range(MXU_SPLIT)]

    def logits(part, c):
        cblk = ckv_ref[chunk_ds(c), :]
        rel = (c * kc - t0 + lax.broadcasted_iota(jnp.int32, (1, kc), 1)).astype(F32)
        s_part = _dot_nt(ql_ref[part_rows[part], :], cblk)
        for i_h in range(heads_per_part):
            h = part * heads_per_part + i_h
            rows = slice(h * qb, (h + 1) * qb)
            s = s_part[i_h * qb:(i_h + 1) * qb, :] + (slopes_ref[h] * LOG2E) * rel + bias_ref[:, chunk_ds(c)]
            s_ref[rows, :] = s
            mx = s[:, 0:LANES]
            for i in range(1, lane_tiles):
                mx = jnp.maximum(mx, s[:, i * LANES:(i + 1) * LANES])
            m_old = m_ref[rows, :]
            m_new = jnp.maximum(m_old, jnp.max(mx, axis=-1, keepdims=True))
            m_ref[rows, :] = m_new
            alpha_ref[rows, :] = jnp.exp2(m_old - m_new)

    def values(part, c):
        cblk = ckv_ref[chunk_ds(c), :]
        for i_h in range(heads_per_part):
            rows = slice((part * heads_per_part + i_h) * qb, (part * heads_per_part + i_h + 1) * qb)
            m = m_ref[rows, :]
            l = alpha_ref[rows, :] * l_ref[rows, :]
            for i in range(lane_tiles):
                p = jnp.exp2(s_ref[rows, i * LANES:(i + 1) * LANES] - m)
                l = l + p
                p_ref[rows, i * LANES:(i + 1) * LANES] = p.astype(BF16)
            l_ref[rows, :] = l
        upd = _dot(p_ref[part_rows[part], :], cblk)
        for i_h in range(heads_per_part):
            rows = slice((part * heads_per_part + i_h) * qb, (part * heads_per_part + i_h + 1) * qb)
            alpha = alpha_ref[rows, :]
            for i in range(KV_RANK // LANES):
                cols = slice(i * LANES, (i + 1) * LANES)
                acc_ref[rows, cols] = alpha * acc_ref[rows, cols] + upd[i_h * qb:(i_h + 1) * qb, cols]

    last = n_chunks - 1
    for part in range(MXU_SPLIT):
        logits(part, 0)

    def att_chunk(c, carry):
        for part in range(MXU_SPLIT):
            values(part, c)
            logits(part, c + 1)
        return carry

    lax.fori_loop(0, last, att_chunk, 0)
    for part in range(MXU_SPLIT):
        values(part, last)

    o = (acc_ref[...] / jnp.sum(l_ref[...], axis=-1, keepdims=True)).astype(BF16)
    y_ref[...] = jnp.concatenate([_dot(o[h * qb:(h + 1) * qb, :], wuv_ref[h]) for h in range(ATT_HEADS)],
                                 axis=1).astype(BF16)


def _dsa(q, qit, wit, ckv, ki, w_uk, w_uv, slopes, bsz, seqlen):
    t = q.shape[0]
    qb = Q_BLOCK
    nq = seqlen // qb
    rows = ATT_HEADS * qb
    topk = min(TOPK_MAX, seqlen // 4)
    pos_bits = max(1, (seqlen - 1).bit_length())
    seq_pad = -(-seqlen // KEY_CHUNK) * KEY_CHUNK
    row = lambda b, j: (b * nq + j, 0)
    col = lambda b, j: (0, b * nq + j)
    per_batch = lambda b, j: (b, 0, 0)
    if seq_pad != seqlen:
        ckv = jnp.pad(ckv.reshape(bsz, seqlen, KV_RANK), ((0, 0), (0, seq_pad - seqlen), (0, 0)))
        ki = jnp.pad(ki.reshape(bsz, seqlen, IDX_DIM), ((0, 0), (0, seq_pad - seqlen), (0, 0)))
    body = functools.partial(_dsa_body, topk=topk, pos_bits=pos_bits)
    return pl.pallas_call(
        body,
        grid=(bsz, nq),
        in_specs=[pl.BlockSpec(memory_space=pltpu.SMEM),
                  pl.BlockSpec((qb, ATT_WIDTH), row), pl.BlockSpec((IDX_HEADS * IDX_DIM, qb), col),
                  pl.BlockSpec((IDX_HEADS, qb), col),
                  pl.BlockSpec((None, seq_pad, KV_RANK), per_batch, pipeline_mode=pl.Buffered(1)),
                  pl.BlockSpec((None, seq_pad, IDX_DIM), per_batch, pipeline_mode=pl.Buffered(1)),
                  _const_spec(w_uk.shape), _const_spec(w_uv.shape)],
        out_specs=pl.BlockSpec((qb, ATT_WIDTH), row),
        out_shape=jax.ShapeDtypeStruct((t, ATT_WIDTH), BF16),
        scratch_shapes=[pltpu.VMEM((seq_pad, qb), F32), pltpu.VMEM((seq_pad, qb), BF16),
                        pltpu.VMEM((qb, seq_pad), F32),
                        pltpu.VMEM((rows, KV_RANK), BF16),
                        pltpu.VMEM((rows, KEY_CHUNK), F32), pltpu.VMEM((rows, KEY_CHUNK), BF16),
                        pltpu.VMEM((rows, LANES), F32), pltpu.VMEM((rows, LANES), F32),
                        pltpu.VMEM((rows, LANES), F32), pltpu.VMEM((rows, KV_RANK), F32)],
        compiler_params=_params("arbitrary", "arbitrary"),
        name="dsa",
    )(slopes, q, qit, wit, ckv.reshape(bsz, seq_pad, KV_RANK), ki.reshape(bsz, seq_pad, IDX_DIM), w_uk, w_uv)


def _merge_ln_body(x_ref, ys_ref, ya_ref, sg_ref, wps_ref, wpa_ref, wo_ref, g_ref, b_ref, o_ref):
    sg = sg_ref[...].astype(F32)
    merged = (sg[:, :D_MODEL] * _dot(ys_ref[...], wps_ref[...])
              + sg[:, D_MODEL:] * _dot(ya_ref[...], wpa_ref[...]))
    out = _dot(merged.astype(BF16), wo_ref[...])
    o_ref[...] = _layer_norm(ALPHA * x_ref[...] + out, g_ref[...], b_ref[...])


def _merge_ln(x, ys, ya, sg, wps, wpa, wo, g, b, tm):
    t = x.shape[0]
    row = lambda i: (i, 0)
    return pl.pallas_call(
        _merge_ln_body,
        grid=(t // tm,),
        in_specs=[pl.BlockSpec((tm, D_MODEL), row), pl.BlockSpec((tm, SSD_INNER), row),
                  pl.BlockSpec((tm, ATT_WIDTH), row), pl.BlockSpec((tm, 2 * D_MODEL), row)]
                 + [_const_spec(w.shape) for w in (wps, wpa, wo, g, b)],
        out_specs=pl.BlockSpec((tm, D_MODEL), row),
        out_shape=jax.ShapeDtypeStruct((t, D_MODEL), F32),
        compiler_params=_params("parallel"),
        name="merge_ln",
    )(x, ys, ya, sg, wps, wpa, wo, g, b)


def _mem_kv_body(m_ref, w_ref, o_ref):
    o_ref[...] = _dot(m_ref[...].astype(BF16), w_ref[...]).astype(BF16)


def _mem_kv(mem, w_mkv, bsz, mem_len):
    row = lambda b: (b, 0)
    return pl.pallas_call(
        _mem_kv_body,
        grid=(bsz,),
        in_specs=[pl.BlockSpec((mem_len, D_MODEL), row), _const_spec(w_mkv.shape)],
        out_specs=pl.BlockSpec((mem_len, 2 * D_MODEL), row),
        out_shape=jax.ShapeDtypeStruct((bsz * mem_len, 2 * D_MODEL), BF16),
        compiler_params=_params("parallel"),
        name="mem_kv",
    )(mem, w_mkv)


def _xattn_ln_body(x_ref, kv_ref, wq_ref, wo_ref, g_ref, b_ref, o_ref):
    x = x_ref[...]
    q = (_dot(x.astype(BF16), wq_ref[...]) * (MEM_HEAD_DIM ** -0.5)).astype(BF16)
    heads = []
    for h in range(MEM_HEADS):
        lo = h * MEM_HEAD_DIM
        s = _dot_nt(q[:, lo:lo + MEM_HEAD_DIM], kv_ref[:, lo:lo + MEM_HEAD_DIM])
        p = jnp.exp(s - jnp.max(s, axis=-1, keepdims=True))
        o = _dot(p.astype(BF16), kv_ref[:, D_MODEL + lo:D_MODEL + lo + MEM_HEAD_DIM])
        heads.append((o / jnp.sum(p, axis=-1, keepdims=True)).astype(BF16))
    out = _dot(jnp.concatenate(heads, axis=1), wo_ref[...])
    o_ref[...] = _layer_norm(ALPHA * x + out, g_ref[...], b_ref[...])


def _xattn_ln(x, kv, wq, wo, g, b, bsz, seqlen, mem_len, tm):
    t = x.shape[0]
    nt = seqlen // tm
    return pl.pallas_call(
        _xattn_ln_body,
        grid=(bsz, nt),
        in_specs=[pl.BlockSpec((tm, D_MODEL), lambda b_, i: (b_ * nt + i, 0)),
                  pl.BlockSpec((mem_len, 2 * D_MODEL), lambda b_, i: (b_, 0))]
                 + [_const_spec(w.shape) for w in (wq, wo, g, b)],
        out_specs=pl.BlockSpec((tm, D_MODEL), lambda b_, i: (b_ * nt + i, 0)),
        out_shape=jax.ShapeDtypeStruct((t, D_MODEL), F32),
        compiler_params=_params("parallel", "parallel"),
        name="xattn_ln",
    )(x, kv, wq, wo, g, b)


def _split_w_in(w_in):
    sizes = (D_MODEL, D_MODEL, SSD_INNER, SSD_XBC, SSD_HEADS, ATT_WIDTH, KV_RANK, IDX_HEADS * IDX_DIM,
             IDX_DIM, IDX_HEADS)
    parts, off = [], 0
    for s in sizes:
        parts.append(w_in[:, off:off + s])
        off += s
    g_ssd, g_att, z, xbc, dt, q, ckv, qi, ki, wi = parts
    misc = jnp.concatenate([dt, ki, jnp.zeros((w_in.shape[0], LANES - SSD_HEADS - IDX_DIM), w_in.dtype)], axis=1)
    return [w.astype(BF16) for w in (jnp.concatenate([g_ssd, g_att], axis=1), z, xbc, q, ckv, qi.T, misc, wi.T)]


def _layer(x, mem, bsz, seqlen, mem_len, p):
    (ffn1_w_in, ffn1_w_out, ln1_g, ln1_b, w_in, conv_w, conv_b, dt_bias, a_log, d_skip, ssd_norm_w,
     kv_norm_w, w_uk, w_uv, w_proj_ssd, w_proj_att, w_out, ln2_g, ln2_b,
     w_mq, w_mkv, w_mo, ln3_g, ln3_b, ffn2_w_in, ffn2_w_out, ln4_g, ln4_b) = p
    tm = min(TOKEN_TILE, seqlen)
    vec = lambda v: v.reshape(1, -1)
    bf = lambda w: w.astype(BF16)

    x = _ffn_ln(x, bf(ffn1_w_in), bf(ffn1_w_out), vec(ln1_g), vec(ln1_b), tm)

    sg, sz, xbc, q, ckv, qit, ki, dt, wit = _in_proj(x, *_split_w_in(w_in), vec(kv_norm_w), tm)

    expand = jnp.repeat(jnp.eye(SSD_HEADS, dtype=BF16), SSD_HEAD_DIM, axis=1)
    y_ssd = _ssd(xbc, sz, dt, conv_w, vec(conv_b), vec(dt_bias), vec(a_log),
                 vec(jnp.repeat(d_skip, SSD_HEAD_DIM)), vec(ssd_norm_w), expand, bsz, seqlen)

    slopes = 2.0 ** (-8.0 * jnp.arange(1, ATT_HEADS + 1, dtype=F32) / ATT_HEADS)
    y_att = _dsa(q, qit, wit, ckv, ki, bf(w_uk), bf(w_uv), slopes, bsz, seqlen)

    x = _merge_ln(x, y_ssd, y_att, sg, bf(w_proj_ssd), bf(w_proj_att), bf(w_out), vec(ln2_g), vec(ln2_b), tm)

    kv = _mem_kv(mem, bf(w_mkv), bsz, mem_len)
    x = _xattn_ln(x, kv, bf(w_mq), bf(w_mo), vec(ln3_g), vec(ln3_b), bsz, seqlen, mem_len, tm)

    return _ffn_ln(x, bf(ffn2_w_in), bf(ffn2_w_out), vec(ln4_g), vec(ln4_b), tm)


def kernel(x, mem, ffn1_w_in, ffn1_w_out, ln1_g, ln1_b, w_in, conv_w, conv_b, dt_bias, a_log, d_skip,
           ssd_norm_w, kv_norm_w, w_uk, w_uv, w_proj_ssd, w_proj_att, w_out, ln2_g, ln2_b,
           w_mq, w_mkv, w_mo, ln3_g, ln3_b, ffn2_w_in, ffn2_w_out, ln4_g, ln4_b):
    bsz, seqlen, d = x.shape
    mem_len = mem.shape[1]
    params = (ffn1_w_in, ffn1_w_out, ln1_g, ln1_b, w_in, conv_w, conv_b, dt_bias, a_log, d_skip, ssd_norm_w,
              kv_norm_w, w_uk, w_uv, w_proj_ssd, w_proj_att, w_out, ln2_g, ln2_b,
              w_mq, w_mkv, w_mo, ln3_g, ln3_b, ffn2_w_in, ffn2_w_out, ln4_g, ln4_b)
    h = x.reshape(bsz * seqlen, d)
    m = mem.reshape(bsz * mem_len, d)
    for layer in range(ffn1_w_in.shape[0]):
        h = _layer(h, m, bsz, seqlen, mem_len, tuple(w[layer] for w in params))
    return h.reshape(bsz, seqlen, d)
```

```python
import functools
import math

import jax
import jax.numpy as jnp
from jax import lax
from jax.experimental import pallas as pl
from jax.experimental.pallas import tpu as pltpu

F32 = jnp.float32
BF16 = jnp.bfloat16

D_MODEL = 1024
DEPTH = 1
SSD_HEADS = 16
SSD_HEAD_DIM = 64
SSD_INNER = SSD_HEADS * SSD_HEAD_DIM
SSD_GROUPS = 4
SSD_STATE = 128
SSD_CONV = 4
SSD_CHUNK = 128
SSD_STEP_CHUNKS = 2
SSD_GN = SSD_GROUPS * SSD_STATE
SSD_XBC = SSD_INNER + 2 * SSD_GN
ATT_HEADS = 16
ATT_HEAD_DIM = 64
ATT_WIDTH = ATT_HEADS * ATT_HEAD_DIM
KV_RANK = 256
IDX_HEADS = 16
IDX_DIM = 64
TOPK_MAX = 256
Q_BLOCK = 256
MEM_HEADS = 4
MEM_HEAD_DIM = D_MODEL // MEM_HEADS
D_FF = 2816
ALPHA = (2.0 * DEPTH) ** 0.25
LN_EPS = 1e-5
RMS_EPS = 1e-6

INT_MIN = -(2 ** 31)
INT_MAX = 2 ** 31 - 1
LOWEST_FINITE_KEY = INT_MIN + 2 ** 23
NEG_BIG = -1e30
LOG2E = math.log2(math.e)

LANES = 128
SUBLANES = 8
BF16_ROWS = 16
TOKEN_TILE = 512
FFN_CHUNK = 1408
KEY_CHUNK = 512
THR_BITS_ALWAYS = 9
THR_BITS_STEP = 2
MXU_SPLIT = 4
CONV_TAIL = 16
VMEM_LIMIT = 56 * 1024 * 1024

NT_DIMS = (((1,), (1,)), ((), ()))
TN_DIMS = (((0,), (0,)), ((), ()))


def _const_spec(shape):
    return pl.BlockSpec(shape, lambda *_: (0,) * len(shape), pipeline_mode=pl.Buffered(1))


def _params(*sem):
    return pltpu.CompilerParams(dimension_semantics=sem, vmem_limit_bytes=VMEM_LIMIT)


def _layer_norm(y, g, b):
    mu = jnp.mean(y, axis=-1, keepdims=True)
    d = y - mu
    var = jnp.mean(d * d, axis=-1, keepdims=True)
    return d * lax.rsqrt(var + LN_EPS) * g + b


def _dot(a, b):
    return jnp.dot(a, b, preferred_element_type=F32)


def _dot_nt(a, b):
    return lax.dot_general(a, b, NT_DIMS, preferred_element_type=F32)


def _split3(x):
    hi = x.astype(BF16)
    rest = x - hi.astype(F32)
    mid = rest.astype(BF16)
    return hi, mid, (rest - mid.astype(F32)).astype(BF16)


def _select_sum(sel01, pieces, lhs_is_01):
    if lhs_is_01:
        return sum(_dot(sel01, p) for p in pieces)
    return sum(_dot(p, sel01) for p in pieces)


def _ffn_ln_body(x_ref, win_ref, wout_ref, g_ref, b_ref, o_ref):
    x = x_ref[...]
    xb = x.astype(BF16)
    acc = jnp.zeros(x.shape, F32)
    for c in range(D_FF // FFN_CHUNK):
        lo = c * FFN_CHUNK
        gate = _dot(xb, win_ref[:, lo:lo + FFN_CHUNK])
        up = _dot(xb, win_ref[:, D_FF + lo:D_FF + lo + FFN_CHUNK])
        act = (gate * jax.nn.sigmoid(gate) * up).astype(BF16)
        acc = acc + _dot(act, wout_ref[lo:lo + FFN_CHUNK, :])
    o_ref[...] = _layer_norm(ALPHA * x + 0.5 * acc, g_ref[...], b_ref[...])


def _ffn_ln(x, w_in, w_out, g, b, tm):
    t = x.shape[0]
    row = lambda i: (i, 0)
    return pl.pallas_call(
        _ffn_ln_body,
        grid=(t // tm,),
        in_specs=[pl.BlockSpec((tm, D_MODEL), row),
                  _const_spec(w_in.shape), _const_spec(w_out.shape),
                  _const_spec(g.shape), _const_spec(b.shape)],
        out_specs=pl.BlockSpec((tm, D_MODEL), row),
        out_shape=jax.ShapeDtypeStruct((t, D_MODEL), F32),
        compiler_params=_params("parallel"),
        name="ffn_ln",
    )(x, w_in, w_out, g, b)


def _in_proj_body(x_ref, wg_ref, wz_ref, wxbc_ref, wq_ref, wckv_ref, wqit_ref, wmisc_ref, wwit_ref, kvn_ref,
                  sg_ref, sz_ref, xbc_ref, q_ref, ckv_ref, ckvt_ref, qit_ref, ki_ref, dt_ref, wit_ref):
    xb = x_ref[...].astype(BF16)
    sg_ref[...] = jax.nn.sigmoid(_dot(xb, wg_ref[...])).astype(BF16)
    z = _dot(xb, wz_ref[...])
    sz_ref[...] = (z * jax.nn.sigmoid(z)).astype(BF16)
    xbc_ref[...] = _dot(xb, wxbc_ref[...]).astype(BF16)
    q_ref[...] = _dot(xb, wq_ref[...]).astype(BF16)
    c = _dot(xb, wckv_ref[...])
    c = c * lax.rsqrt(jnp.mean(c * c, axis=-1, keepdims=True) + RMS_EPS) * kvn_ref[...]
    ckv_ref[...] = c.astype(BF16)
    ckvt_ref[...] = c.T.astype(BF16)
    qit_ref[...] = _dot_nt(wqit_ref[...], xb).astype(BF16)
    wit_ref[...] = _dot_nt(wwit_ref[...], xb) * (IDX_HEADS ** -0.5)
    misc = _dot(xb, wmisc_ref[...])
    dt_ref[...] = misc[:, 0:SSD_HEADS]
    ki_ref[...] = misc[:, SSD_HEADS:SSD_HEADS + IDX_DIM].astype(BF16)


def _in_proj(x, wg, wz, wxbc, wq, wckv, wqit, wmisc, wwit, kvn, tm):
    t = x.shape[0]
    row = lambda i: (i, 0)
    col = lambda i: (0, i)
    out = [
        ((t, 2 * D_MODEL), BF16, (tm, 2 * D_MODEL), row),
        ((t, SSD_INNER), BF16, (tm, SSD_INNER), row),
        ((t, SSD_XBC), BF16, (tm, SSD_XBC), row),
        ((t, ATT_WIDTH), BF16, (tm, ATT_WIDTH), row),
        ((t, KV_RANK), BF16, (tm, KV_RANK), row),
        ((KV_RANK, t), BF16, (KV_RANK, tm), col),
        ((IDX_HEADS * IDX_DIM, t), BF16, (IDX_HEADS * IDX_DIM, tm), col),
        ((t, IDX_DIM), BF16, (tm, IDX_DIM), row),
        ((t, SSD_HEADS), F32, (tm, SSD_HEADS), row),
        ((IDX_HEADS, t), F32, (IDX_HEADS, tm), col),
    ]
    return pl.pallas_call(
        _in_proj_body,
        grid=(t // tm,),
        in_specs=[pl.BlockSpec((tm, D_MODEL), row)] + [_const_spec(w.shape) for w in
                                                        (wg, wz, wxbc, wq, wckv, wqit, wmisc, wwit, kvn)],
        out_specs=[pl.BlockSpec(blk, im) for _, _, blk, im in out],
        out_shape=[jax.ShapeDtypeStruct(shp, dt) for shp, dt, _, _ in out],
        compiler_params=_params("parallel"),
        name="in_proj",
    )(x, wg, wz, wxbc, wq, wckv, wqit, wmisc, wwit, kvn)


def _ssd_body(xbc_ref, sz_ref, dt_ref, cw_ref, cb_ref, dtb_ref, alog_ref, dskip_ref, nw_ref, expand_ref,
              y_ref, state_ref, ext_ref):
    chunk = SSD_CHUNK
    n_sub = xbc_ref.shape[0] // chunk
    hp = SSD_HEAD_DIM
    gw = SSD_INNER // SSD_GROUPS

    @pl.when(pl.program_id(1) == 0)
    def _():
        state_ref[...] = jnp.zeros(state_ref.shape, F32)
        ext_ref[0:CONV_TAIL, :] = jnp.zeros((CONV_TAIL, SSD_XBC), BF16)

    ext_ref[CONV_TAIL:, :] = xbc_ref[...]
    t_i = lax.broadcasted_iota(jnp.int32, (chunk, CONV_TAIL + chunk), 0)
    j_i = lax.broadcasted_iota(jnp.int32, (chunk, CONV_TAIL + chunk), 1)
    r_i = lax.broadcasted_iota(jnp.int32, (chunk, chunk), 0)
    c_i = lax.broadcasted_iota(jnp.int32, (chunk, chunk), 1)
    causal = r_i >= c_i
    causal_b = causal.astype(BF16)
    eye = (lax.broadcasted_iota(jnp.int32, (SSD_HEADS, SSD_HEADS), 0)
           == lax.broadcasted_iota(jnp.int32, (SSD_HEADS, SSD_HEADS), 1)).astype(BF16)
    expand = expand_ref[...]
    neg_a = -jnp.exp(alog_ref[...])
    state = state_ref[...]

    for sub in range(n_sub):
        rows = slice(sub * chunk, (sub + 1) * chunk)
        ext = ext_ref[sub * chunk:sub * chunk + CONV_TAIL + chunk, :]
        conv = cb_ref[...] + cw_ref[SSD_CONV - 1:SSD_CONV, :] * xbc_ref[rows, :].astype(F32)
        for k in range(SSD_CONV - 1):
            shift = (j_i == t_i + (CONV_TAIL - (SSD_CONV - 1) + k)).astype(BF16)
            conv = conv + cw_ref[k:k + 1, :] * _dot(shift, ext)
        act = conv * jax.nn.sigmoid(conv)
        xs = act[:, :SSD_INNER]

        pre = dt_ref[rows, :] + dtb_ref[...]
        dt = jnp.maximum(pre, 0.0) + jnp.log(1.0 + jnp.exp(-jnp.abs(pre)))
        a = dt * neg_a
        a_cs = _select_sum(causal_b, _split3(a), lhs_is_01=True)
        a_cs_3 = _split3(a_cs)
        a_cs_t = sum(_dot_nt(eye, piece) for piece in a_cs_3)
        dt_e = _select_sum(expand, _split3(dt), lhs_is_01=False)
        a_cs_e = _select_sum(expand, a_cs_3, lhs_is_01=False)
        a_end_e = a_cs_e[chunk - 1:chunk, :]

        xdt = xs * dt_e
        xdt_b = xdt.astype(BF16)
        xdt_end_b = (xdt * jnp.exp(a_end_e - a_cs_e)).astype(BF16)
        state_b = state.astype(BF16)

        y_diag, y_off, new_state = [], [], []
        for g in range(SSD_GROUPS):
            bm = act[:, SSD_INNER + g * SSD_STATE:SSD_INNER + (g + 1) * SSD_STATE].astype(BF16)
            cm = act[:, SSD_INNER + SSD_GN + g * SSD_STATE:SSD_INNER + SSD_GN + (g + 1) * SSD_STATE].astype(BF16)
            cb = _dot_nt(cm, bm)
            y_off.append(_dot(cm, state_b[:, g * gw:(g + 1) * gw]))
            new_state.append(lax.dot_general(bm, xdt_end_b[:, g * gw:(g + 1) * gw], TN_DIMS,
                                             preferred_element_type=F32))
            for r in range(SSD_HEADS // SSD_GROUPS):
                h = g * (SSD_HEADS // SSD_GROUPS) + r
                seg = a_cs[:, h:h + 1] - a_cs_t[h:h + 1, :]
                decay = jnp.exp(jnp.where(causal, seg, NEG_BIG))
                y_diag.append(_dot((cb * decay).astype(BF16), xdt_b[:, h * hp:(h + 1) * hp]))
        state = jnp.exp(a_end_e) * state + jnp.concatenate(new_state, axis=1)

        y = jnp.concatenate(y_diag, axis=1) + jnp.concatenate(y_off, axis=1) * jnp.exp(a_cs_e)
        y = (y + dskip_ref[...] * xs) * sz_ref[rows, :].astype(F32)
        outs = []
        for g in range(SSD_GROUPS):
            yg = y[:, g * gw:(g + 1) * gw]
            outs.append(yg * lax.rsqrt(jnp.mean(yg * yg, axis=-1, keepdims=True) + RMS_EPS))
        y_ref[rows, :] = (jnp.concatenate(outs, axis=1) * nw_ref[...]).astype(BF16)

    state_ref[...] = state
    ext_ref[0:CONV_TAIL, :] = ext_ref[n_sub * chunk:n_sub * chunk + CONV_TAIL, :]


def _ssd(xbc, sz, dt, conv_w, conv_b, dt_bias, a_log, d_skip_e, norm_w, expand, bsz, seqlen):
    t = xbc.shape[0]
    n_sub = SSD_STEP_CHUNKS if (seqlen // SSD_CHUNK) % SSD_STEP_CHUNKS == 0 else 1
    rows = n_sub * SSD_CHUNK
    ns = seqlen // rows
    row = lambda b, c: (b * ns + c, 0)
    return pl.pallas_call(
        _ssd_body,
        grid=(bsz, ns),
        in_specs=[pl.BlockSpec((rows, SSD_XBC), row), pl.BlockSpec((rows, SSD_INNER), row),
                  pl.BlockSpec((rows, SSD_HEADS), row)]
                 + [_const_spec(w.shape) for w in (conv_w, conv_b, dt_bias, a_log, d_skip_e, norm_w, expand)],
        out_specs=pl.BlockSpec((rows, SSD_INNER), row),
        out_shape=jax.ShapeDtypeStruct((t, SSD_INNER), BF16),
        scratch_shapes=[pltpu.VMEM((SSD_STATE, SSD_INNER), F32),
                        pltpu.VMEM((CONV_TAIL + rows, SSD_XBC), BF16)],
        compiler_params=_params("arbitrary", "arbitrary"),
        name="ssd",
    )(xbc, sz, dt, conv_w, conv_b, dt_bias, a_log, d_skip_e, norm_w, expand)


def _score_of_key(key):
    return pltpu.bitcast(jnp.where(key >= 0, key, key ^ INT_MAX), F32)


def _dsa_body(slopes_ref, q_ref, qit_ref, wit_ref, ckv_ref, ckvt_ref, ki_ref, wuk_ref, wuv_ref, y_ref,
              sc_ref, sb_ref, bias_ref, ql_ref, s_ref, p_ref, m_ref, l_ref, alpha_ref, acc_ref, *, topk, pos_bits):
    qb = q_ref.shape[0]
    kc = KEY_CHUNK
    t0 = pl.program_id(1) * qb
    n_chunks = (t0 + qb + kc - 1) // kc
    q_pos = t0 + lax.broadcasted_iota(jnp.int32, (1, qb), 1)

    def key_pos(c):
        return c * kc + lax.broadcasted_iota(jnp.int32, (kc, qb), 0)

    def chunk_ds(c):
        return pl.ds(pl.multiple_of(c * kc, kc), kc)

    def score_chunk(c, carry):
        kblk = ki_ref[chunk_ds(c), :]
        score = jnp.zeros((kc, qb), F32)
        for h in range(0, IDX_HEADS, 2):
            w = jnp.concatenate([qit_ref[h * IDX_DIM:(h + 1) * IDX_DIM, :],
                                 qit_ref[(h + 1) * IDX_DIM:(h + 2) * IDX_DIM, :]], axis=1)
            wi = jnp.concatenate([wit_ref[h:h + 1, :], wit_ref[h + 1:h + 2, :]], axis=1)
            t = jnp.maximum(_dot(kblk, w), 0.0) * wi
            score = score + t[:, :qb] + t[:, qb:]
        score = jnp.where(key_pos(c) <= q_pos, score, -jnp.inf)
        sc_ref[chunk_ds(c), :] = score
        sb_ref[chunk_ds(c), :] = score.astype(BF16)
        return carry

    lax.fori_loop(0, n_chunks, score_chunk, 0)

    def count(pred):
        n_acc = 8

        def body(c, parts):
            hit = pred(sc_ref[chunk_ds(c), :], key_pos(c))
            parts = list(parts)
            for i in range(kc // SUBLANES):
                tile = hit[i * SUBLANES:(i + 1) * SUBLANES, :]
                parts[i % n_acc] = jnp.where(tile, parts[i % n_acc] + 1, parts[i % n_acc])
            return tuple(parts)
        parts = lax.fori_loop(0, n_chunks, body, (jnp.zeros((SUBLANES, qb), jnp.int32),) * n_acc)
        return jnp.sum(sum(parts), axis=0, keepdims=True)

    def count_rounded(cand_b):
        n_acc = 4

        def body(c, parts):
            hit = jnp.where(sb_ref[chunk_ds(c), :] >= cand_b, jnp.ones((kc, qb), BF16), jnp.zeros((kc, qb), BF16))
            parts = list(parts)
            for i in range(kc // BF16_ROWS):
                parts[i % n_acc] = parts[i % n_acc] + hit[i * BF16_ROWS:(i + 1) * BF16_ROWS, :]
            return tuple(parts)
        parts = lax.fori_loop(0, n_chunks, body, (jnp.zeros((BF16_ROWS, qb), BF16),) * n_acc)
        total = sum(p.astype(F32) for p in parts)
        return jnp.sum(total, axis=0, keepdims=True).astype(jnp.int32)

    def coarse_bit(i, key16):
        cand = key16 + (jnp.int32(1) << (15 - i))
        pattern = jnp.where(cand >= 0, cand, cand ^ 0x7FFF)
        cand_b = pltpu.bitcast(pattern << 16, F32).astype(BF16)
        return jnp.where(count_rounded(cand_b) >= topk, cand, key16)

    key16 = lax.fori_loop(0, 16, coarse_bit, jnp.full((1, qb), -(2 ** 15), jnp.int32))
    coarse = (key16 << 16) + jnp.where(key16 < 0, 0xFFFF, 0)
    window_lo = jnp.where(coarse < INT_MIN + 2 ** 16, INT_MIN, coarse - 2 ** 16)

    def thr_bit(i, state):
        thr, n_ge = state
        cand = thr + (jnp.int32(1) << (16 - i))
        cand_f = _score_of_key(cand)
        n = count(lambda s, p: s >= cand_f)
        return jnp.where(n >= topk, cand, thr), jnp.where(n >= topk, n, n_ge)

    def pending(state):
        return jnp.max(jnp.where(state[1] <= topk, 0, 1))

    window_lo_f = _score_of_key(window_lo)
    state = (window_lo, count(lambda s, p: s >= window_lo_f))
    state = lax.fori_loop(0, THR_BITS_ALWAYS, thr_bit, state)

    def more_bits(carry):
        i, state, _ = carry
        for b in range(THR_BITS_STEP):
            state = thr_bit(i + b, state)
        return i + THR_BITS_STEP, state, pending(state)

    _, (thr, _), _ = lax.while_loop(lambda carry: (carry[0] < 17) & (carry[2] > 0), more_bits,
                                    (jnp.int32(THR_BITS_ALWAYS), state, pending(state)))
    thr_f = jnp.where(thr >= LOWEST_FINITE_KEY, _score_of_key(jnp.maximum(thr, LOWEST_FINITE_KEY)), -jnp.inf)

    need = topk - count(lambda s, p: s > thr_f)
    n_tie = count(lambda s, p: (s == thr_f) & (p <= q_pos))

    def tie_cut():
        def cut_bit(i, cut):
            cand = cut + (jnp.int32(1) << (pos_bits - 1 - i))
            n_before = count(lambda s, p: (s == thr_f) & (p <= q_pos) & (p < cand))
            return jnp.where(n_before < need, cand, cut)
        return lax.fori_loop(0, pos_bits, cut_bit, jnp.zeros((1, qb), jnp.int32))

    eye = (lax.broadcasted_iota(jnp.int32, (qb, qb), 0)
           == lax.broadcasted_iota(jnp.int32, (qb, qb), 1)).astype(BF16)

    def write_bias(selected):
        def bias_chunk(c, carry):
            p = key_pos(c)
            sel = (p <= q_pos) & selected(sc_ref[chunk_ds(c), :], p)
            bias_ref[:, chunk_ds(c)] = _dot_nt(eye, jnp.where(sel, 0.0, NEG_BIG).astype(BF16))
            return carry
        lax.fori_loop(0, n_chunks, bias_chunk, 0)

    def bias_with_ties():
        cut = tie_cut()
        write_bias(lambda s, p: (s > thr_f) | ((s == thr_f) & (p <= cut)))

    def bias_no_ties():
        write_bias(lambda s, p: s >= thr_f)

    lax.cond(jnp.max(jnp.where(n_tie > need, 1, 0)) > 0, bias_with_ties, bias_no_ties)

    for h in range(ATT_HEADS):
        rows = slice(h * qb, (h + 1) * qb)
        ql = _dot(q_ref[:, h * ATT_HEAD_DIM:(h + 1) * ATT_HEAD_DIM], wuk_ref[h])
        ql_ref[rows, :] = (ql * (ATT_HEAD_DIM ** -0.5 * LOG2E)).astype(BF16)
        m_ref[rows, :] = jnp.full((qb, LANES), NEG_BIG, F32)
        l_ref[rows, :] = jnp.zeros((qb, LANES), F32)
        acc_ref[rows, :] = jnp.zeros((qb, KV_RANK), F32)
    lane_tiles = kc // LANES
    heads_per_part = ATT_HEADS // MXU_SPLIT

    part_rows = [slice(r * heads_per_part * qb, (r + 1) * heads_per_part * qb) for r in range(MXU_SPLIT)]

    def logits(part, c):
        rel = (c * kc - t0 + lax.broadcasted_iota(jnp.int32, (1, kc), 1)).astype(F32)
        s_part = _dot(ql_ref[part_rows[part], :], ckvt_ref[:, chunk_ds(c)])
        for i_h in range(heads_per_part):
            h = part * heads_per_part + i_h
            rows = slice(h * qb, (h + 1) * qb)
            s = s_part[i_h * qb:(i_h + 1) * qb, :] + (slopes_ref[h] * LOG2E) * rel + bias_ref[:, chunk_ds(c)]
            s_ref[rows, :] = s
            mx = s[:, 0:LANES]
            for i in range(1, lane_tiles):
                mx = jnp.maximum(mx, s[:, i * LANES:(i + 1) * LANES])
            m_old = m_ref[rows, :]
            m_new = jnp.maximum(m_old, jnp.max(mx, axis=-1, keepdims=True))
            m_ref[rows, :] = m_new
            alpha_ref[rows, :] = jnp.exp2(m_old - m_new)

    def values(part, c):
        cblk = ckv_ref[chunk_ds(c), :]
        for i_h in range(heads_per_part):
            rows = slice((part * heads_per_part + i_h) * qb, (part * heads_per_part + i_h + 1) * qb)
            m = m_ref[rows, :]
            l = alpha_ref[rows, :] * l_ref[rows, :]
            for i in range(lane_tiles):
                p = jnp.exp2(s_ref[rows, i * LANES:(i + 1) * LANES] - m)
                l = l + p
                p_ref[rows, i * LANES:(i + 1) * LANES] = p.astype(BF16)
            l_ref[rows, :] = l
        upd = _dot(p_ref[part_rows[part], :], cblk)
        for i_h in range(heads_per_part):
            rows = slice((part * heads_per_part + i_h) * qb, (part * heads_per_part + i_h + 1) * qb)
            alpha = alpha_ref[rows, :]
            for i in range(KV_RANK // LANES):
                cols = slice(i * LANES, (i + 1) * LANES)
                acc_ref[rows, cols] = alpha * acc_ref[rows, cols] + upd[i_h * qb:(i_h + 1) * qb, cols]

    last = n_chunks - 1
    for part in range(MXU_SPLIT):
        logits(part, 0)

    def att_chunk(c, carry):
        for part in range(MXU_SPLIT):
            values(part, c)
            logits(part, c + 1)
        return carry

    lax.fori_loop(0, last, att_chunk, 0)
    for part in range(MXU_SPLIT):
        values(part, last)

    o = (acc_ref[...] / jnp.sum(l_ref[...], axis=-1, keepdims=True)).astype(BF16)
    y_ref[...] = jnp.concatenate([_dot(o[h * qb:(h + 1) * qb, :], wuv_ref[h]) for h in range(ATT_HEADS)],
                                 axis=1).astype(BF16)


def _dsa(q, qit, wit, ckv, ckvt, ki, w_uk, w_uv, slopes, bsz, seqlen):
    t = q.shape[0]
    qb = Q_BLOCK
    nq = seqlen // qb
    rows = ATT_HEADS * qb
    topk = min(TOPK_MAX, seqlen // 4)
    pos_bits = max(1, (seqlen - 1).bit_length())
    seq_pad = -(-seqlen // KEY_CHUNK) * KEY_CHUNK
    row = lambda b, j: (b * nq + j, 0)
    col = lambda b, j: (0, b * nq + j)
    per_batch = lambda b, j: (b, 0, 0)
    if seq_pad != seqlen:
        ckv = jnp.pad(ckv.reshape(bsz, seqlen, KV_RANK), ((0, 0), (0, seq_pad - seqlen), (0, 0)))
        ki = jnp.pad(ki.reshape(bsz, seqlen, IDX_DIM), ((0, 0), (0, seq_pad - seqlen), (0, 0)))
    ckvt = ckvt.reshape(KV_RANK, bsz, seqlen)
    if seq_pad != seqlen:
        ckvt = jnp.pad(ckvt, ((0, 0), (0, 0), (0, seq_pad - seqlen)))
    ckvt = ckvt.reshape(KV_RANK, bsz * seq_pad)
    body = functools.partial(_dsa_body, topk=topk, pos_bits=pos_bits)
    return pl.pallas_call(
        body,
        grid=(bsz, nq),
        in_specs=[pl.BlockSpec(memory_space=pltpu.SMEM),
                  pl.BlockSpec((qb, ATT_WIDTH), row), pl.BlockSpec((IDX_HEADS * IDX_DIM, qb), col),
                  pl.BlockSpec((IDX_HEADS, qb), col),
                  pl.BlockSpec((None, seq_pad, KV_RANK), per_batch, pipeline_mode=pl.Buffered(1)),
                  pl.BlockSpec((KV_RANK, seq_pad), lambda b, j: (0, b), pipeline_mode=pl.Buffered(1)),
                  pl.BlockSpec((None, seq_pad, IDX_DIM), per_batch, pipeline_mode=pl.Buffered(1)),
                  _const_spec(w_uk.shape), _const_spec(w_uv.shape)],
        out_specs=pl.BlockSpec((qb, ATT_WIDTH), row),
        out_shape=jax.ShapeDtypeStruct((t, ATT_WIDTH), BF16),
        scratch_shapes=[pltpu.VMEM((seq_pad, qb), F32), pltpu.VMEM((seq_pad, qb), BF16),
                        pltpu.VMEM((qb, seq_pad), F32),
                        pltpu.VMEM((rows, KV_RANK), BF16),
                        pltpu.VMEM((rows, KEY_CHUNK), F32), pltpu.VMEM((rows, KEY_CHUNK), BF16),
                        pltpu.VMEM((rows, LANES), F32), pltpu.VMEM((rows, LANES), F32),
                        pltpu.VMEM((rows, LANES), F32), pltpu.VMEM((rows, KV_RANK), F32)],
        compiler_params=_params("arbitrary", "arbitrary"),
        name="dsa",
    )(slopes, q, qit, wit, ckv.reshape(bsz, seq_pad, KV_RANK), ckvt, ki.reshape(bsz, seq_pad, IDX_DIM), w_uk, w_uv)


def _merge_ln_body(x_ref, ys_ref, ya_ref, sg_ref, wps_ref, wpa_ref, wo_ref, g_ref, b_ref, o_ref):
    sg = sg_ref[...].astype(F32)
    merged = (sg[:, :D_MODEL] * _dot(ys_ref[...], wps_ref[...])
              + sg[:, D_MODEL:] * _dot(ya_ref[...], wpa_ref[...]))
    out = _dot(merged.astype(BF16), wo_ref[...])
    o_ref[...] = _layer_norm(ALPHA * x_ref[...] + out, g_ref[...], b_ref[...])


def _merge_ln(x, ys, ya, sg, wps, wpa, wo, g, b, tm):
    t = x.shape[0]
    row = lambda i: (i, 0)
    return pl.pallas_call(
        _merge_ln_body,
        grid=(t // tm,),
        in_specs=[pl.BlockSpec((tm, D_MODEL), row), pl.BlockSpec((tm, SSD_INNER), row),
                  pl.BlockSpec((tm, ATT_WIDTH), row), pl.BlockSpec((tm, 2 * D_MODEL), row)]
                 + [_const_spec(w.shape) for w in (wps, wpa, wo, g, b)],
        out_specs=pl.BlockSpec((tm, D_MODEL), row),
        out_shape=jax.ShapeDtypeStruct((t, D_MODEL), F32),
        compiler_params=_params("parallel"),
        name="merge_ln",
    )(x, ys, ya, sg, wps, wpa, wo, g, b)


def _mem_kv_body(m_ref, w_ref, o_ref):
    o_ref[...] = _dot(m_ref[...].astype(BF16), w_ref[...]).astype(BF16)


def _mem_kv(mem, w_mkv, bsz, mem_len):
    row = lambda b: (b, 0)
    return pl.pallas_call(
        _mem_kv_body,
        grid=(bsz,),
        in_specs=[pl.BlockSpec((mem_len, D_MODEL), row), _const_spec(w_mkv.shape)],
        out_specs=pl.BlockSpec((mem_len, 2 * D_MODEL), row),
        out_shape=jax.ShapeDtypeStruct((bsz * mem_len, 2 * D_MODEL), BF16),
        compiler_params=_params("parallel"),
        name="mem_kv",
    )(mem, w_mkv)


def _xattn_ln_body(x_ref, kv_ref, wq_ref, wo_ref, g_ref, b_ref, o_ref):
    x = x_ref[...]
    q = (_dot(x.astype(BF16), wq_ref[...]) * (MEM_HEAD_DIM ** -0.5)).astype(BF16)
    heads = []
    for h in range(MEM_HEADS):
        lo = h * MEM_HEAD_DIM
        s = _dot_nt(q[:, lo:lo + MEM_HEAD_DIM], kv_ref[:, lo:lo + MEM_HEAD_DIM])
        p = jnp.exp(s - jnp.max(s, axis=-1, keepdims=True))
        o = _dot(p.astype(BF16), kv_ref[:, D_MODEL + lo:D_MODEL + lo + MEM_HEAD_DIM])
        heads.append((o / jnp.sum(p, axis=-1, keepdims=True)).astype(BF16))
    out = _dot(jnp.concatenate(heads, axis=1), wo_ref[...])
    o_ref[...] = _layer_norm(ALPHA * x + out, g_ref[...], b_ref[...])


def _xattn_ln(x, kv, wq, wo, g, b, bsz, seqlen, mem_len, tm):
    t = x.shape[0]
    nt = seqlen // tm
    return pl.pallas_call(
        _xattn_ln_body,
        grid=(bsz, nt),
        in_specs=[pl.BlockSpec((tm, D_MODEL), lambda b_, i: (b_ * nt + i, 0)),
                  pl.BlockSpec((mem_len, 2 * D_MODEL), lambda b_, i: (b_, 0))]
                 + [_const_spec(w.shape) for w in (wq, wo, g, b)],
        out_specs=pl.BlockSpec((tm, D_MODEL), lambda b_, i: (b_ * nt + i, 0)),
        out_shape=jax.ShapeDtypeStruct((t, D_MODEL), F32),
        compiler_params=_params("parallel", "parallel"),
        name="xattn_ln",
    )(x, kv, wq, wo, g, b)


def _split_w_in(w_in):
    sizes = (D_MODEL, D_MODEL, SSD_INNER, SSD_XBC, SSD_HEADS, ATT_WIDTH, KV_RANK, IDX_HEADS * IDX_DIM,
             IDX_DIM, IDX_HEADS)
    parts, off = [], 0
    for s in sizes:
        parts.append(w_in[:, off:off + s])
        off += s
    g_ssd, g_att, z, xbc, dt, q, ckv, qi, ki, wi = parts
    misc = jnp.concatenate([dt, ki, jnp.zeros((w_in.shape[0], LANES - SSD_HEADS - IDX_DIM), w_in.dtype)], axis=1)
    return [w.astype(BF16) for w in (jnp.concatenate([g_ssd, g_att], axis=1), z, xbc, q, ckv, qi.T, misc, wi.T)]


def _layer(x, mem, bsz, seqlen, mem_len, p):
    (ffn1_w_in, ffn1_w_out, ln1_g, ln1_b, w_in, conv_w, conv_b, dt_bias, a_log, d_skip, ssd_norm_w,
     kv_norm_w, w_uk, w_uv, w_proj_ssd, w_proj_att, w_out, ln2_g, ln2_b,
     w_mq, w_mkv, w_mo, ln3_g, ln3_b, ffn2_w_in, ffn2_w_out, ln4_g, ln4_b) = p
    tm = min(TOKEN_TILE, seqlen)
    vec = lambda v: v.reshape(1, -1)
    bf = lambda w: w.astype(BF16)

    x = _ffn_ln(x, bf(ffn1_w_in), bf(ffn1_w_out), vec(ln1_g), vec(ln1_b), tm)

    sg, sz, xbc, q, ckv, ckvt, qit, ki, dt, wit = _in_proj(x, *_split_w_in(w_in), vec(kv_norm_w), tm)

    expand = jnp.repeat(jnp.eye(SSD_HEADS, dtype=BF16), SSD_HEAD_DIM, axis=1)
    y_ssd = _ssd(xbc, sz, dt, conv_w, vec(conv_b), vec(dt_bias), vec(a_log),
                 vec(jnp.repeat(d_skip, SSD_HEAD_DIM)), vec(ssd_norm_w), expand, bsz, seqlen)

    slopes = 2.0 ** (-8.0 * jnp.arange(1, ATT_HEADS + 1, dtype=F32) / ATT_HEADS)
    y_att = _dsa(q, qit, wit, ckv, ckvt, ki, bf(w_uk), bf(w_uv), slopes, bsz, seqlen)

    x = _merge_ln(x, y_ssd, y_att, sg, bf(w_proj_ssd), bf(w_proj_att), bf(w_out), vec(ln2_g), vec(ln2_b), tm)

    kv = _mem_kv(mem, bf(w_mkv), bsz, mem_len)
    x = _xattn_ln(x, kv, bf(w_mq), bf(w_mo), vec(ln3_g), vec(ln3_b), bsz, seqlen, mem_len, tm)

    return _ffn_ln(x, bf(ffn2_w_in), bf(ffn2_w_out), vec(ln4_g), vec(ln4_b), tm)


def kernel(x, mem, ffn1_w_in, ffn1_w_out, ln1_g, ln1_b, w_in, conv_w, conv_b, dt_bias, a_log, d_skip,
           ssd_norm_w, kv_norm_w, w_uk, w_uv, w_proj_ssd, w_proj_att, w_out, ln2_g, ln2_b,
           w_mq, w_mkv, w_mo, ln3_g, ln3_b, ffn2_w_in, ffn2_w_out, ln4_g, ln4_b):
    bsz, seqlen, d = x.shape
    mem_len = mem.shape[1]
    params = (ffn1_w_in, ffn1_w_out, ln1_g, ln1_b, w_in, conv_w, conv_b, dt_bias, a_log, d_skip, ssd_norm_w,
              kv_norm_w, w_uk, w_uv, w_proj_ssd, w_proj_att, w_out, ln2_g, ln2_b,
              w_mq, w_mkv, w_mo, ln3_g, ln3_b, ffn2_w_in, ffn2_w_out, ln4_g, ln4_b)
    h = x.reshape(bsz * seqlen, d)
    m = mem.reshape(bsz * mem_len, d)
    for layer in range(ffn1_w_in.shape[0]):
        h = _layer(h, m, bsz, seqlen, mem_len, tuple(w[layer] for w in params))
    return h.reshape(bsz, seqlen, d)
```

```python
import functools
import math

import jax
import jax.numpy as jnp
from jax import lax
from jax.experimental import pallas as pl
from jax.experimental.pallas import tpu as pltpu

F32 = jnp.float32
BF16 = jnp.bfloat16

D_MODEL = 1024
DEPTH = 1
SSD_HEADS = 16
SSD_HEAD_DIM = 64
SSD_INNER = SSD_HEADS * SSD_HEAD_DIM
SSD_GROUPS = 4
SSD_STATE = 128
SSD_CONV = 4
SSD_CHUNK = 128
SSD_STEP_CHUNKS = 2
SSD_GN = SSD_GROUPS * SSD_STATE
SSD_XBC = SSD_INNER + 2 * SSD_GN
ATT_HEADS = 16
ATT_HEAD_DIM = 64
ATT_WIDTH = ATT_HEADS * ATT_HEAD_DIM
KV_RANK = 256
IDX_HEADS = 16
IDX_DIM = 64
TOPK_MAX = 256
Q_BLOCK = 256
MEM_HEADS = 4
MEM_HEAD_DIM = D_MODEL // MEM_HEADS
D_FF = 2816
ALPHA = (2.0 * DEPTH) ** 0.25
LN_EPS = 1e-5
RMS_EPS = 1e-6

INT_MIN = -(2 ** 31)
INT_MAX = 2 ** 31 - 1
LOWEST_FINITE_KEY = INT_MIN + 2 ** 23
NEG_BIG = -1e30
LOG2E = math.log2(math.e)

LANES = 128
SUBLANES = 8
BF16_ROWS = 16
TOKEN_TILE = 512
FFN_CHUNK = 2816
KEY_CHUNK = 512
THR_BITS_ALWAYS = 9
THR_BITS_STEP = 2
MXU_SPLIT = 4
CONV_TAIL = 16
VMEM_LIMIT = 56 * 1024 * 1024

NT_DIMS = (((1,), (1,)), ((), ()))
TN_DIMS = (((0,), (0,)), ((), ()))


def _const_spec(shape):
    return pl.BlockSpec(shape, lambda *_: (0,) * len(shape), pipeline_mode=pl.Buffered(1))


def _params(*sem):
    return pltpu.CompilerParams(dimension_semantics=sem, vmem_limit_bytes=VMEM_LIMIT)


def _layer_norm(y, g, b):
    mu = jnp.mean(y, axis=-1, keepdims=True)
    d = y - mu
    var = jnp.mean(d * d, axis=-1, keepdims=True)
    return d * lax.rsqrt(var + LN_EPS) * g + b


def _dot(a, b):
    return jnp.dot(a, b, preferred_element_type=F32)


def _dot_nt(a, b):
    return lax.dot_general(a, b, NT_DIMS, preferred_element_type=F32)


def _split3(x):
    hi = x.astype(BF16)
    rest = x - hi.astype(F32)
    mid = rest.astype(BF16)
    return hi, mid, (rest - mid.astype(F32)).astype(BF16)


def _select_sum(sel01, pieces, lhs_is_01):
    if lhs_is_01:
        return sum(_dot(sel01, p) for p in pieces)
    return sum(_dot(p, sel01) for p in pieces)


def _ffn_ln_body(x_ref, win_ref, wout_ref, g_ref, b_ref, o_ref):
    x = x_ref[...]
    xb = x.astype(BF16)
    acc = jnp.zeros(x.shape, F32)
    for c in range(D_FF // FFN_CHUNK):
        lo = c * FFN_CHUNK
        gate = _dot(xb, win_ref[:, lo:lo + FFN_CHUNK])
        up = _dot(xb, win_ref[:, D_FF + lo:D_FF + lo + FFN_CHUNK])
        act = (gate * jax.nn.sigmoid(gate) * up).astype(BF16)
        acc = acc + _dot(act, wout_ref[lo:lo + FFN_CHUNK, :])
    o_ref[...] = _layer_norm(ALPHA * x + 0.5 * acc, g_ref[...], b_ref[...])


def _ffn_ln(x, w_in, w_out, g, b, tm):
    t = x.shape[0]
    row = lambda i: (i, 0)
    return pl.pallas_call(
        _ffn_ln_body,
        grid=(t // tm,),
        in_specs=[pl.BlockSpec((tm, D_MODEL), row),
                  _const_spec(w_in.shape), _const_spec(w_out.shape),
                  _const_spec(g.shape), _const_spec(b.shape)],
        out_specs=pl.BlockSpec((tm, D_MODEL), row),
        out_shape=jax.ShapeDtypeStruct((t, D_MODEL), F32),
        compiler_params=_params("parallel"),
        name="ffn_ln",
    )(x, w_in, w_out, g, b)


def _in_proj_body(x_ref, wg_ref, wz_ref, wxbc_ref, wq_ref, wckv_ref, wqit_ref, wmisc_ref, wwit_ref, kvn_ref,
                  sg_ref, sz_ref, xbc_ref, q_ref, ckv_ref, ckvt_ref, qit_ref, ki_ref, dt_ref, wit_ref):
    xb = x_ref[...].astype(BF16)
    sg_ref[...] = jax.nn.sigmoid(_dot(xb, wg_ref[...])).astype(BF16)
    z = _dot(xb, wz_ref[...])
    sz_ref[...] = (z * jax.nn.sigmoid(z)).astype(BF16)
    xbc_ref[...] = _dot(xb, wxbc_ref[...]).astype(BF16)
    q_ref[...] = _dot(xb, wq_ref[...]).astype(BF16)
    c = _dot(xb, wckv_ref[...])
    c = c * lax.rsqrt(jnp.mean(c * c, axis=-1, keepdims=True) + RMS_EPS) * kvn_ref[...]
    ckv_ref[...] = c.astype(BF16)
    ckvt_ref[...] = c.T.astype(BF16)
    qit_ref[...] = _dot_nt(wqit_ref[...], xb).astype(BF16)
    wit_ref[...] = _dot_nt(wwit_ref[...], xb) * (IDX_HEADS ** -0.5)
    misc = _dot(xb, wmisc_ref[...])
    dt_ref[...] = misc[:, 0:SSD_HEADS]
    ki_ref[...] = misc[:, SSD_HEADS:SSD_HEADS + IDX_DIM].astype(BF16)


def _in_proj(x, wg, wz, wxbc, wq, wckv, wqit, wmisc, wwit, kvn, tm):
    t = x.shape[0]
    row = lambda i: (i, 0)
    col = lambda i: (0, i)
    out = [
        ((t, 2 * D_MODEL), BF16, (tm, 2 * D_MODEL), row),
        ((t, SSD_INNER), BF16, (tm, SSD_INNER), row),
        ((t, SSD_XBC), BF16, (tm, SSD_XBC), row),
        ((t, ATT_WIDTH), BF16, (tm, ATT_WIDTH), row),
        ((t, KV_RANK), BF16, (tm, KV_RANK), row),
        ((KV_RANK, t), BF16, (KV_RANK, tm), col),
        ((IDX_HEADS * IDX_DIM, t), BF16, (IDX_HEADS * IDX_DIM, tm), col),
        ((t, IDX_DIM), BF16, (tm, IDX_DIM), row),
        ((t, SSD_HEADS), F32, (tm, SSD_HEADS), row),
        ((IDX_HEADS, t), F32, (IDX_HEADS, tm), col),
    ]
    return pl.pallas_call(
        _in_proj_body,
        grid=(t // tm,),
        in_specs=[pl.BlockSpec((tm, D_MODEL), row)] + [_const_spec(w.shape) for w in
                                                        (wg, wz, wxbc, wq, wckv, wqit, wmisc, wwit, kvn)],
        out_specs=[pl.BlockSpec(blk, im) for _, _, blk, im in out],
        out_shape=[jax.ShapeDtypeStruct(shp, dt) for shp, dt, _, _ in out],
        compiler_params=_params("parallel"),
        name="in_proj",
    )(x, wg, wz, wxbc, wq, wckv, wqit, wmisc, wwit, kvn)


def _ssd_body(xbc_ref, sz_ref, dt_ref, cw_ref, cb_ref, dtb_ref, alog_ref, dskip_ref, nw_ref, expand_ref,
              y_ref, state_ref, ext_ref):
    chunk = SSD_CHUNK
    n_sub = xbc_ref.shape[0] // chunk
    hp = SSD_HEAD_DIM
    gw = SSD_INNER // SSD_GROUPS

    @pl.when(pl.program_id(1) == 0)
    def _():
        state_ref[...] = jnp.zeros(state_ref.shape, F32)
        ext_ref[0:CONV_TAIL, :] = jnp.zeros((CONV_TAIL, SSD_XBC), BF16)

    ext_ref[CONV_TAIL:, :] = xbc_ref[...]
    t_i = lax.broadcasted_iota(jnp.int32, (chunk, CONV_TAIL + chunk), 0)
    j_i = lax.broadcasted_iota(jnp.int32, (chunk, CONV_TAIL + chunk), 1)
    r_i = lax.broadcasted_iota(jnp.int32, (chunk, chunk), 0)
    c_i = lax.broadcasted_iota(jnp.int32, (chunk, chunk), 1)
    causal = r_i >= c_i
    causal_b = causal.astype(BF16)
    eye = (lax.broadcasted_iota(jnp.int32, (SSD_HEADS, SSD_HEADS), 0)
           == lax.broadcasted_iota(jnp.int32, (SSD_HEADS, SSD_HEADS), 1)).astype(BF16)
    expand = expand_ref[...]
    neg_a = -jnp.exp(alog_ref[...])
    state = state_ref[...]

    for sub in range(n_sub):
        rows = slice(sub * chunk, (sub + 1) * chunk)
        ext = ext_ref[sub * chunk:sub * chunk + CONV_TAIL + chunk, :]
        conv = cb_ref[...] + cw_ref[SSD_CONV - 1:SSD_CONV, :] * xbc_ref[rows, :].astype(F32)
        for k in range(SSD_CONV - 1):
            shift = (j_i == t_i + (CONV_TAIL - (SSD_CONV - 1) + k)).astype(BF16)
            conv = conv + cw_ref[k:k + 1, :] * _dot(shift, ext)
        act = conv * jax.nn.sigmoid(conv)
        xs = act[:, :SSD_INNER]

        pre = dt_ref[rows, :] + dtb_ref[...]
        dt = jnp.maximum(pre, 0.0) + jnp.log(1.0 + jnp.exp(-jnp.abs(pre)))
        a = dt * neg_a
        a_cs = _select_sum(causal_b, _split3(a), lhs_is_01=True)
        a_cs_3 = _split3(a_cs)
        a_cs_t = sum(_dot_nt(eye, piece) for piece in a_cs_3)
        dt_e = _select_sum(expand, _split3(dt), lhs_is_01=False)
        a_cs_e = _select_sum(expand, a_cs_3, lhs_is_01=False)
        a_end_e = a_cs_e[chunk - 1:chunk, :]

        xdt = xs * dt_e
        xdt_b = xdt.astype(BF16)
        xdt_end_b = (xdt * jnp.exp(a_end_e - a_cs_e)).astype(BF16)
        state_b = state.astype(BF16)

        y_diag, y_off, new_state = [], [], []
        for g in range(SSD_GROUPS):
            bm = act[:, SSD_INNER + g * SSD_STATE:SSD_INNER + (g + 1) * SSD_STATE].astype(BF16)
            cm = act[:, SSD_INNER + SSD_GN + g * SSD_STATE:SSD_INNER + SSD_GN + (g + 1) * SSD_STATE].astype(BF16)
            cb = _dot_nt(cm, bm)
            y_off.append(_dot(cm, state_b[:, g * gw:(g + 1) * gw]))
            new_state.append(lax.dot_general(bm, xdt_end_b[:, g * gw:(g + 1) * gw], TN_DIMS,
                                             preferred_element_type=F32))
            for r in range(SSD_HEADS // SSD_GROUPS):
                h = g * (SSD_HEADS // SSD_GROUPS) + r
                seg = a_cs[:, h:h + 1] - a_cs_t[h:h + 1, :]
                decay = jnp.exp(jnp.where(causal, seg, NEG_BIG))
                y_diag.append(_dot((cb * decay).astype(BF16), xdt_b[:, h * hp:(h + 1) * hp]))
        state = jnp.exp(a_end_e) * state + jnp.concatenate(new_state, axis=1)

        y = jnp.concatenate(y_diag, axis=1) + jnp.concatenate(y_off, axis=1) * jnp.exp(a_cs_e)
        y = (y + dskip_ref[...] * xs) * sz_ref[rows, :].astype(F32)
        outs = []
        for g in range(SSD_GROUPS):
            yg = y[:, g * gw:(g + 1) * gw]
            outs.append(yg * lax.rsqrt(jnp.mean(yg * yg, axis=-1, keepdims=True) + RMS_EPS))
        y_ref[rows, :] = (jnp.concatenate(outs, axis=1) * nw_ref[...]).astype(BF16)

    state_ref[...] = state
    ext_ref[0:CONV_TAIL, :] = ext_ref[n_sub * chunk:n_sub * chunk + CONV_TAIL, :]


def _ssd(xbc, sz, dt, conv_w, conv_b, dt_bias, a_log, d_skip_e, norm_w, expand, bsz, seqlen):
    t = xbc.shape[0]
    n_sub = SSD_STEP_CHUNKS if (seqlen // SSD_CHUNK) % SSD_STEP_CHUNKS == 0 else 1
    rows = n_sub * SSD_CHUNK
    ns = seqlen // rows
    row = lambda b, c: (b * ns + c, 0)
    return pl.pallas_call(
        _ssd_body,
        grid=(bsz, ns),
        in_specs=[pl.BlockSpec((rows, SSD_XBC), row), pl.BlockSpec((rows, SSD_INNER), row),
                  pl.BlockSpec((rows, SSD_HEADS), row)]
                 + [_const_spec(w.shape) for w in (conv_w, conv_b, dt_bias, a_log, d_skip_e, norm_w, expand)],
        out_specs=pl.BlockSpec((rows, SSD_INNER), row),
        out_shape=jax.ShapeDtypeStruct((t, SSD_INNER), BF16),
        scratch_shapes=[pltpu.VMEM((SSD_STATE, SSD_INNER), F32),
                        pltpu.VMEM((CONV_TAIL + rows, SSD_XBC), BF16)],
        compiler_params=_params("arbitrary", "arbitrary"),
        name="ssd",
    )(xbc, sz, dt, conv_w, conv_b, dt_bias, a_log, d_skip_e, norm_w, expand)


def _score_of_key(key):
    return pltpu.bitcast(jnp.where(key >= 0, key, key ^ INT_MAX), F32)


def _dsa_body(slopes_ref, q_ref, qit_ref, wit_ref, ckv_ref, ckvt_ref, ki_ref, wuk_ref, wuv_ref, y_ref,
              sc_ref, sb_ref, bias_ref, ql_ref, s_ref, p_ref, m_ref, l_ref, alpha_ref, acc_ref, *, topk, pos_bits):
    qb = q_ref.shape[0]
    kc = KEY_CHUNK
    t0 = pl.program_id(1) * qb
    n_chunks = (t0 + qb + kc - 1) // kc
    q_pos = t0 + lax.broadcasted_iota(jnp.int32, (1, qb), 1)

    def key_pos(c):
        return c * kc + lax.broadcasted_iota(jnp.int32, (kc, qb), 0)

    def chunk_ds(c):
        return pl.ds(pl.multiple_of(c * kc, kc), kc)

    def score_chunk(c, carry):
        kblk = ki_ref[chunk_ds(c), :]
        score = jnp.zeros((kc, qb), F32)
        for h in range(0, IDX_HEADS, 2):
            w = jnp.concatenate([qit_ref[h * IDX_DIM:(h + 1) * IDX_DIM, :],
                                 qit_ref[(h + 1) * IDX_DIM:(h + 2) * IDX_DIM, :]], axis=1)
            wi = jnp.concatenate([wit_ref[h:h + 1, :], wit_ref[h + 1:h + 2, :]], axis=1)
            t = jnp.maximum(_dot(kblk, w), 0.0) * wi
            score = score + t[:, :qb] + t[:, qb:]
        score = jnp.where(key_pos(c) <= q_pos, score, -jnp.inf)
        sc_ref[chunk_ds(c), :] = score
        sb_ref[chunk_ds(c), :] = score.astype(BF16)
        return carry

    lax.fori_loop(0, n_chunks, score_chunk, 0)

    def count(pred):
        n_acc = 8

        def body(c, parts):
            hit = pred(sc_ref[chunk_ds(c), :], key_pos(c))
            parts = list(parts)
            for i in range(kc // SUBLANES):
                tile = hit[i * SUBLANES:(i + 1) * SUBLANES, :]
                parts[i % n_acc] = jnp.where(tile, parts[i % n_acc] + 1, parts[i % n_acc])
            return tuple(parts)
        parts = lax.fori_loop(0, n_chunks, body, (jnp.zeros((SUBLANES, qb), jnp.int32),) * n_acc)
        return jnp.sum(sum(parts), axis=0, keepdims=True)

    def count_rounded(cand_b):
        n_acc = 4

        def body(c, parts):
            hit = jnp.where(sb_ref[chunk_ds(c), :] >= cand_b, jnp.ones((kc, qb), BF16), jnp.zeros((kc, qb), BF16))
            parts = list(parts)
            for i in range(kc // BF16_ROWS):
                parts[i % n_acc] = parts[i % n_acc] + hit[i * BF16_ROWS:(i + 1) * BF16_ROWS, :]
            return tuple(parts)
        parts = lax.fori_loop(0, n_chunks, body, (jnp.zeros((BF16_ROWS, qb), BF16),) * n_acc)
        total = sum(p.astype(F32) for p in parts)
        return jnp.sum(total, axis=0, keepdims=True).astype(jnp.int32)

    def coarse_bit(i, key16):
        cand = key16 + (jnp.int32(1) << (15 - i))
        pattern = jnp.where(cand >= 0, cand, cand ^ 0x7FFF)
        cand_b = pltpu.bitcast(pattern << 16, F32).astype(BF16)
        return jnp.where(count_rounded(cand_b) >= topk, cand, key16)

    key16 = lax.fori_loop(0, 16, coarse_bit, jnp.full((1, qb), -(2 ** 15), jnp.int32))
    coarse = (key16 << 16) + jnp.where(key16 < 0, 0xFFFF, 0)
    window_lo = jnp.where(coarse < INT_MIN + 2 ** 16, INT_MIN, coarse - 2 ** 16)

    def thr_bit(i, state):
        thr, n_ge = state
        cand = thr + (jnp.int32(1) << (16 - i))
        cand_f = _score_of_key(cand)
        n = count(lambda s, p: s >= cand_f)
        return jnp.where(n >= topk, cand, thr), jnp.where(n >= topk, n, n_ge)

    def pending(state):
        return jnp.max(jnp.where(state[1] <= topk, 0, 1))

    window_lo_f = _score_of_key(window_lo)
    state = (window_lo, count(lambda s, p: s >= window_lo_f))
    state = lax.fori_loop(0, THR_BITS_ALWAYS, thr_bit, state)

    def more_bits(carry):
        i, state, _ = carry
        for b in range(THR_BITS_STEP):
            state = thr_bit(i + b, state)
        return i + THR_BITS_STEP, state, pending(state)

    _, (thr, _), _ = lax.while_loop(lambda carry: (carry[0] < 17) & (carry[2] > 0), more_bits,
                                    (jnp.int32(THR_BITS_ALWAYS), state, pending(state)))
    thr_f = jnp.where(thr >= LOWEST_FINITE_KEY, _score_of_key(jnp.maximum(thr, LOWEST_FINITE_KEY)), -jnp.inf)

    need = topk - count(lambda s, p: s > thr_f)
    n_tie = count(lambda s, p: (s == thr_f) & (p <= q_pos))

    def tie_cut():
        def cut_bit(i, cut):
            cand = cut + (jnp.int32(1) << (pos_bits - 1 - i))
            n_before = count(lambda s, p: (s == thr_f) & (p <= q_pos) & (p < cand))
            return jnp.where(n_before < need, cand, cut)
        return lax.fori_loop(0, pos_bits, cut_bit, jnp.zeros((1, qb), jnp.int32))

    eye = (lax.broadcasted_iota(jnp.int32, (qb, qb), 0)
           == lax.broadcasted_iota(jnp.int32, (qb, qb), 1)).astype(BF16)

    def write_bias(selected):
        def bias_chunk(c, carry):
            p = key_pos(c)
            sel = (p <= q_pos) & selected(sc_ref[chunk_ds(c), :], p)
            bias_ref[:, chunk_ds(c)] = _dot_nt(eye, jnp.where(sel, 0.0, NEG_BIG).astype(BF16))
            return carry
        lax.fori_loop(0, n_chunks, bias_chunk, 0)

    def bias_with_ties():
        cut = tie_cut()
        write_bias(lambda s, p: (s > thr_f) | ((s == thr_f) & (p <= cut)))

    def bias_no_ties():
        write_bias(lambda s, p: s >= thr_f)

    lax.cond(jnp.max(jnp.where(n_tie > need, 1, 0)) > 0, bias_with_ties, bias_no_ties)

    for h in range(ATT_HEADS):
        rows = slice(h * qb, (h + 1) * qb)
        ql = _dot(q_ref[:, h * ATT_HEAD_DIM:(h + 1) * ATT_HEAD_DIM], wuk_ref[h])
        ql_ref[rows, :] = (ql * (ATT_HEAD_DIM ** -0.5 * LOG2E)).astype(BF16)
        m_ref[rows, :] = jnp.full((qb, LANES), NEG_BIG, F32)
        l_ref[rows, :] = jnp.zeros((qb, LANES), F32)
        acc_ref[rows, :] = jnp.zeros((qb, KV_RANK), F32)
    lane_tiles = kc // LANES
    heads_per_part = ATT_HEADS // MXU_SPLIT

    part_rows = [slice(r * heads_per_part * qb, (r + 1) * heads_per_part * qb) for r in range(MXU_SPLIT)]

    def logits(part, c):
        rel = (c * kc - t0 + lax.broadcasted_iota(jnp.int32, (1, kc), 1)).astype(F32)
        s_part = _dot(ql_ref[part_rows[part], :], ckvt_ref[:, chunk_ds(c)])
        for i_h in range(heads_per_part):
            h = part * heads_per_part + i_h
            rows = slice(h * qb, (h + 1) * qb)
            s = s_part[i_h * qb:(i_h + 1) * qb, :] + (slopes_ref[h] * LOG2E) * rel + bias_ref[:, chunk_ds(c)]
            s_ref[rows, :] = s
            mx = s[:, 0:LANES]
            for i in range(1, lane_tiles):
                mx = jnp.maximum(mx, s[:, i * LANES:(i + 1) * LANES])
            m_old = m_ref[rows, :]
            m_new = jnp.maximum(m_old, jnp.max(mx, axis=-1, keepdims=True))
            m_ref[rows, :] = m_new
            alpha_ref[rows, :] = jnp.exp2(m_old - m_new)

    def values(part, c):
        cblk = ckv_ref[chunk_ds(c), :]
        for i_h in range(heads_per_part):
            rows = slice((part * heads_per_part + i_h) * qb, (part * heads_per_part + i_h + 1) * qb)
            m = m_ref[rows, :]
            l = alpha_ref[rows, :] * l_ref[rows, :]
            for i in range(lane_tiles):
                p = jnp.exp2(s_ref[rows, i * LANES:(i + 1) * LANES] - m)
                l = l + p
                p_ref[rows, i * LANES:(i + 1) * LANES] = p.astype(BF16)
            l_ref[rows, :] = l
        upd = _dot(p_ref[part_rows[part], :], cblk)
        for i_h in range(heads_per_part):
            rows = slice((part * heads_per_part + i_h) * qb, (part * heads_per_part + i_h + 1) * qb)
            alpha = alpha_ref[rows, :]
            for i in range(KV_RANK // LANES):
                cols = slice(i * LANES, (i + 1) * LANES)
                acc_ref[rows, cols] = alpha * acc_ref[rows, cols] + upd[i_h * qb:(i_h + 1) * qb, cols]

    last = n_chunks - 1
    for part in range(MXU_SPLIT):
        logits(part, 0)

    def att_chunk(c, carry):
        for part in range(MXU_SPLIT):
            values(part, c)
            logits(part, c + 1)
        return carry

    lax.fori_loop(0, last, att_chunk, 0)
    for part in range(MXU_SPLIT):
        values(part, last)

    o = (acc_ref[...] / jnp.sum(l_ref[...], axis=-1, keepdims=True)).astype(BF16)
    y_ref[...] = jnp.concatenate([_dot(o[h * qb:(h + 1) * qb, :], wuv_ref[h]) for h in range(ATT_HEADS)],
                                 axis=1).astype(BF16)


def _dsa(q, qit, wit, ckv, ckvt, ki, w_uk, w_uv, slopes, bsz, seqlen):
    t = q.shape[0]
    qb = Q_BLOCK
    nq = seqlen // qb
    rows = ATT_HEADS * qb
    topk = min(TOPK_MAX, seqlen // 4)
    pos_bits = max(1, (seqlen - 1).bit_length())
    seq_pad = -(-seqlen // KEY_CHUNK) * KEY_CHUNK
    row = lambda b, j: (b * nq + j, 0)
    col = lambda b, j: (0, b * nq + j)
    per_batch = lambda b, j: (b, 0, 0)
    if seq_pad != seqlen:
        ckv = jnp.pad(ckv.reshape(bsz, seqlen, KV_RANK), ((0, 0), (0, seq_pad - seqlen), (0, 0)))
        ki = jnp.pad(ki.reshape(bsz, seqlen, IDX_DIM), ((0, 0), (0, seq_pad - seqlen), (0, 0)))
    ckvt = ckvt.reshape(KV_RANK, bsz, seqlen)
    if seq_pad != seqlen:
        ckvt = jnp.pad(ckvt, ((0, 0), (0, 0), (0, seq_pad - seqlen)))
    ckvt = ckvt.reshape(KV_RANK, bsz * seq_pad)
    body = functools.partial(_dsa_body, topk=topk, pos_bits=pos_bits)
    return pl.pallas_call(
        body,
        grid=(bsz, nq),
        in_specs=[pl.BlockSpec(memory_space=pltpu.SMEM),
                  pl.BlockSpec((qb, ATT_WIDTH), row), pl.BlockSpec((IDX_HEADS * IDX_DIM, qb), col),
                  pl.BlockSpec((IDX_HEADS, qb), col),
                  pl.BlockSpec((None, seq_pad, KV_RANK), per_batch, pipeline_mode=pl.Buffered(1)),
                  pl.BlockSpec((KV_RANK, seq_pad), lambda b, j: (0, b), pipeline_mode=pl.Buffered(1)),
                  pl.BlockSpec((None, seq_pad, IDX_DIM), per_batch, pipeline_mode=pl.Buffered(1)),
                  _const_spec(w_uk.shape), _const_spec(w_uv.shape)],
        out_specs=pl.BlockSpec((qb, ATT_WIDTH), row),
        out_shape=jax.ShapeDtypeStruct((t, ATT_WIDTH), BF16),
        scratch_shapes=[pltpu.VMEM((seq_pad, qb), F32), pltpu.VMEM((seq_pad, qb), BF16),
                        pltpu.VMEM((qb, seq_pad), F32),
                        pltpu.VMEM((rows, KV_RANK), BF16),
                        pltpu.VMEM((rows, KEY_CHUNK), F32), pltpu.VMEM((rows, KEY_CHUNK), BF16),
                        pltpu.VMEM((rows, LANES), F32), pltpu.VMEM((rows, LANES), F32),
                        pltpu.VMEM((rows, LANES), F32), pltpu.VMEM((rows, KV_RANK), F32)],
        compiler_params=_params("arbitrary", "arbitrary"),
        name="dsa",
    )(slopes, q, qit, wit, ckv.reshape(bsz, seq_pad, KV_RANK), ckvt, ki.reshape(bsz, seq_pad, IDX_DIM), w_uk, w_uv)


def _merge_ln_body(x_ref, ys_ref, ya_ref, sg_ref, wps_ref, wpa_ref, wo_ref, g_ref, b_ref, o_ref):
    sg = sg_ref[...].astype(F32)
    merged = (sg[:, :D_MODEL] * _dot(ys_ref[...], wps_ref[...])
              + sg[:, D_MODEL:] * _dot(ya_ref[...], wpa_ref[...]))
    out = _dot(merged.astype(BF16), wo_ref[...])
    o_ref[...] = _layer_norm(ALPHA * x_ref[...] + out, g_ref[...], b_ref[...])


def _merge_ln(x, ys, ya, sg, wps, wpa, wo, g, b, tm):
    t = x.shape[0]
    row = lambda i: (i, 0)
    return pl.pallas_call(
        _merge_ln_body,
        grid=(t // tm,),
        in_specs=[pl.BlockSpec((tm, D_MODEL), row), pl.BlockSpec((tm, SSD_INNER), row),
                  pl.BlockSpec((tm, ATT_WIDTH), row), pl.BlockSpec((tm, 2 * D_MODEL), row)]
                 + [_const_spec(w.shape) for w in (wps, wpa, wo, g, b)],
        out_specs=pl.BlockSpec((tm, D_MODEL), row),
        out_shape=jax.ShapeDtypeStruct((t, D_MODEL), F32),
        compiler_params=_params("parallel"),
        name="merge_ln",
    )(x, ys, ya, sg, wps, wpa, wo, g, b)


def _mem_kv_body(m_ref, w_ref, o_ref):
    o_ref[...] = _dot(m_ref[...].astype(BF16), w_ref[...]).astype(BF16)


def _mem_kv(mem, w_mkv, bsz, mem_len):
    row = lambda b: (b, 0)
    return pl.pallas_call(
        _mem_kv_body,
        grid=(bsz,),
        in_specs=[pl.BlockSpec((mem_len, D_MODEL), row), _const_spec(w_mkv.shape)],
        out_specs=pl.BlockSpec((mem_len, 2 * D_MODEL), row),
        out_shape=jax.ShapeDtypeStruct((bsz * mem_len, 2 * D_MODEL), BF16),
        compiler_params=_params("parallel"),
        name="mem_kv",
    )(mem, w_mkv)


def _xattn_ln_body(x_ref, kv_ref, wq_ref, wo_ref, g_ref, b_ref, o_ref):
    x = x_ref[...]
    q = (_dot(x.astype(BF16), wq_ref[...]) * (MEM_HEAD_DIM ** -0.5)).astype(BF16)
    heads = []
    for h in range(MEM_HEADS):
        lo = h * MEM_HEAD_DIM
        s = _dot_nt(q[:, lo:lo + MEM_HEAD_DIM], kv_ref[:, lo:lo + MEM_HEAD_DIM])
        p = jnp.exp(s - jnp.max(s, axis=-1, keepdims=True))
        o = _dot(p.astype(BF16), kv_ref[:, D_MODEL + lo:D_MODEL + lo + MEM_HEAD_DIM])
        heads.append((o / jnp.sum(p, axis=-1, keepdims=True)).astype(BF16))
    out = _dot(jnp.concatenate(heads, axis=1), wo_ref[...])
    o_ref[...] = _layer_norm(ALPHA * x + out, g_ref[...], b_ref[...])


def _xattn_ln(x, kv, wq, wo, g, b, bsz, seqlen, mem_len, tm):
    t = x.shape[0]
    nt = seqlen // tm
    return pl.pallas_call(
        _xattn_ln_body,
        grid=(bsz, nt),
        in_specs=[pl.BlockSpec((tm, D_MODEL), lambda b_, i: (b_ * nt + i, 0)),
                  pl.BlockSpec((mem_len, 2 * D_MODEL), lambda b_, i: (b_, 0))]
                 + [_const_spec(w.shape) for w in (wq, wo, g, b)],
        out_specs=pl.BlockSpec((tm, D_MODEL), lambda b_, i: (b_ * nt + i, 0)),
        out_shape=jax.ShapeDtypeStruct((t, D_MODEL), F32),
        compiler_params=_params("parallel", "parallel"),
        name="xattn_ln",
    )(x, kv, wq, wo, g, b)


def _split_w_in(w_in):
    sizes = (D_MODEL, D_MODEL, SSD_INNER, SSD_XBC, SSD_HEADS, ATT_WIDTH, KV_RANK, IDX_HEADS * IDX_DIM,
             IDX_DIM, IDX_HEADS)
    parts, off = [], 0
    for s in sizes:
        parts.append(w_in[:, off:off + s])
        off += s
    g_ssd, g_att, z, xbc, dt, q, ckv, qi, ki, wi = parts
    misc = jnp.concatenate([dt, ki, jnp.zeros((w_in.shape[0], LANES - SSD_HEADS - IDX_DIM), w_in.dtype)], axis=1)
    return [w.astype(BF16) for w in (jnp.concatenate([g_ssd, g_att], axis=1), z, xbc, q, ckv, qi.T, misc, wi.T)]


def _layer(x, mem, bsz, seqlen, mem_len, p):
    (ffn1_w_in, ffn1_w_out, ln1_g, ln1_b, w_in, conv_w, conv_b, dt_bias, a_log, d_skip, ssd_norm_w,
     kv_norm_w, w_uk, w_uv, w_proj_ssd, w_proj_att, w_out, ln2_g, ln2_b,
     w_mq, w_mkv, w_mo, ln3_g, ln3_b, ffn2_w_in, ffn2_w_out, ln4_g, ln4_b) = p
    tm = min(TOKEN_TILE, seqlen)
    vec = lambda v: v.reshape(1, -1)
    bf = lambda w: w.astype(BF16)

    x = _ffn_ln(x, bf(ffn1_w_in), bf(ffn1_w_out), vec(ln1_g), vec(ln1_b), tm)

    sg, sz, xbc, q, ckv, ckvt, qit, ki, dt, wit = _in_proj(x, *_split_w_in(w_in), vec(kv_norm_w), tm)

    expand = jnp.repeat(jnp.eye(SSD_HEADS, dtype=BF16), SSD_HEAD_DIM, axis=1)
    y_ssd = _ssd(xbc, sz, dt, conv_w, vec(conv_b), vec(dt_bias), vec(a_log),
                 vec(jnp.repeat(d_skip, SSD_HEAD_DIM)), vec(ssd_norm_w), expand, bsz, seqlen)

    slopes = 2.0 ** (-8.0 * jnp.arange(1, ATT_HEADS + 1, dtype=F32) / ATT_HEADS)
    y_att = _dsa(q, qit, wit, ckv, ckvt, ki, bf(w_uk), bf(w_uv), slopes, bsz, seqlen)

    x = _merge_ln(x, y_ssd, y_att, sg, bf(w_proj_ssd), bf(w_proj_att), bf(w_out), vec(ln2_g), vec(ln2_b), tm)

    kv = _mem_kv(mem, bf(w_mkv), bsz, mem_len)
    x = _xattn_ln(x, kv, bf(w_mq), bf(w_mo), vec(ln3_g), vec(ln3_b), bsz, seqlen, mem_len, tm)

    return _ffn_ln(x, bf(ffn2_w_in), bf(ffn2_w_out), vec(ln4_g), vec(ln4_b), tm)


def kernel(x, mem, ffn1_w_in, ffn1_w_out, ln1_g, ln1_b, w_in, conv_w, conv_b, dt_bias, a_log, d_skip,
           ssd_norm_w, kv_norm_w, w_uk, w_uv, w_proj_ssd, w_proj_att, w_out, ln2_g, ln2_b,
           w_mq, w_mkv, w_mo, ln3_g, ln3_b, ffn2_w_in, ffn2_w_out, ln4_g, ln4_b):
    bsz, seqlen, d = x.shape
    mem_len = mem.shape[1]
    params = (ffn1_w_in, ffn1_w_out, ln1_g, ln1_b, w_in, conv_w, conv_b, dt_bias, a_log, d_skip, ssd_norm_w,
              kv_norm_w, w_uk, w_uv, w_proj_ssd, w_proj_att, w_out, ln2_g, ln2_b,
              w_mq, w_mkv, w_mo, ln3_g, ln3_b, ffn2_w_in, ffn2_w_out, ln4_g, ln4_b)
    h = x.reshape(bsz * seqlen, d)
    m = mem.reshape(bsz * mem_len, d)
    for layer in range(ffn1_w_in.shape[0]):
        h = _layer(h, m, bsz, seqlen, mem_len, tuple(w[layer] for w in params))
    return h.reshape(bsz, seqlen, d)
```

```python
import functools
import math

import jax
import jax.numpy as jnp
from jax import lax
from jax.experimental import pallas as pl
from jax.experimental.pallas import tpu as pltpu

F32 = jnp.float32
BF16 = jnp.bfloat16

D_MODEL = 1024
DEPTH = 1
SSD_HEADS = 16
SSD_HEAD_DIM = 64
SSD_INNER = SSD_HEADS * SSD_HEAD_DIM
SSD_GROUPS = 4
SSD_STATE = 128
SSD_CONV = 4
SSD_CHUNK = 128
SSD_STEP_CHUNKS = 2
SSD_GN = SSD_GROUPS * SSD_STATE
SSD_XBC = SSD_INNER + 2 * SSD_GN
ATT_HEADS = 16
ATT_HEAD_DIM = 64
ATT_WIDTH = ATT_HEADS * ATT_HEAD_DIM
KV_RANK = 256
IDX_HEADS = 16
IDX_DIM = 64
TOPK_MAX = 256
Q_BLOCK = 256
MEM_HEADS = 4
MEM_HEAD_DIM = D_MODEL // MEM_HEADS
D_FF = 2816
ALPHA = (2.0 * DEPTH) ** 0.25
LN_EPS = 1e-5
RMS_EPS = 1e-6

INT_MIN = -(2 ** 31)
INT_MAX = 2 ** 31 - 1
LOWEST_FINITE_KEY = INT_MIN + 2 ** 23
NEG_BIG = -1e30
LOG2E = math.log2(math.e)

LANES = 128
SUBLANES = 8
BF16_ROWS = 16
TOKEN_TILE = 512
FFN_CHUNK = 2816
KEY_CHUNK = 512
THR_BITS_ALWAYS = 9
THR_BITS_STEP = 2
MXU_SPLIT = 4
CONV_TAIL = 16
VMEM_LIMIT = 56 * 1024 * 1024

NT_DIMS = (((1,), (1,)), ((), ()))
TN_DIMS = (((0,), (0,)), ((), ()))


def _const_spec(shape):
    return pl.BlockSpec(shape, lambda *_: (0,) * len(shape), pipeline_mode=pl.Buffered(1))


def _params(*sem):
    return pltpu.CompilerParams(dimension_semantics=sem, vmem_limit_bytes=VMEM_LIMIT)


def _layer_norm(y, g, b):
    mu = jnp.mean(y, axis=-1, keepdims=True)
    d = y - mu
    var = jnp.mean(d * d, axis=-1, keepdims=True)
    return d * lax.rsqrt(var + LN_EPS) * g + b


def _dot(a, b):
    return jnp.dot(a, b, preferred_element_type=F32)


def _dot_nt(a, b):
    return lax.dot_general(a, b, NT_DIMS, preferred_element_type=F32)


def _split3(x):
    hi = x.astype(BF16)
    rest = x - hi.astype(F32)
    mid = rest.astype(BF16)
    return hi, mid, (rest - mid.astype(F32)).astype(BF16)


def _select_sum(sel01, pieces, lhs_is_01):
    if lhs_is_01:
        return sum(_dot(sel01, p) for p in pieces)
    return sum(_dot(p, sel01) for p in pieces)


def _ffn_ln_body(x_ref, win_ref, wout_ref, g_ref, b_ref, o_ref):
    x = x_ref[...]
    xb = x.astype(BF16)
    acc = jnp.zeros(x.shape, F32)
    for c in range(D_FF // FFN_CHUNK):
        lo = c * FFN_CHUNK
        gate = _dot(xb, win_ref[:, lo:lo + FFN_CHUNK])
        up = _dot(xb, win_ref[:, D_FF + lo:D_FF + lo + FFN_CHUNK])
        act = (gate * jax.nn.sigmoid(gate) * up).astype(BF16)
        acc = acc + _dot(act, wout_ref[lo:lo + FFN_CHUNK, :])
    o_ref[...] = _layer_norm(ALPHA * x + 0.5 * acc, g_ref[...], b_ref[...])


def _ffn_ln(x, w_in, w_out, g, b, tm):
    t = x.shape[0]
    row = lambda i: (i, 0)
    return pl.pallas_call(
        _ffn_ln_body,
        grid=(t // tm,),
        in_specs=[pl.BlockSpec((tm, D_MODEL), row),
                  _const_spec(w_in.shape), _const_spec(w_out.shape),
                  _const_spec(g.shape), _const_spec(b.shape)],
        out_specs=pl.BlockSpec((tm, D_MODEL), row),
        out_shape=jax.ShapeDtypeStruct((t, D_MODEL), F32),
        compiler_params=_params("parallel"),
        name="ffn_ln",
    )(x, w_in, w_out, g, b)


def _in_proj_body(x_ref, wg_ref, wz_ref, wxbc_ref, wq_ref, wckv_ref, wqit_ref, wmisc_ref, wwit_ref, kvn_ref,
                  sg_ref, sz_ref, xbc_ref, q_ref, ckv_ref, ckvt_ref, qit_ref, ki_ref, dt_ref, wit_ref):
    xb = x_ref[...].astype(BF16)
    sg_ref[...] = jax.nn.sigmoid(_dot(xb, wg_ref[...])).astype(BF16)
    z = _dot(xb, wz_ref[...])
    sz_ref[...] = (z * jax.nn.sigmoid(z)).astype(BF16)
    xbc_ref[...] = _dot(xb, wxbc_ref[...]).astype(BF16)
    q_ref[...] = _dot(xb, wq_ref[...]).astype(BF16)
    c = _dot(xb, wckv_ref[...])
    c = c * lax.rsqrt(jnp.mean(c * c, axis=-1, keepdims=True) + RMS_EPS) * kvn_ref[...]
    ckv_ref[...] = c.astype(BF16)
    ckvt_ref[...] = c.T.astype(BF16)
    qit_ref[...] = _dot_nt(wqit_ref[...], xb).astype(BF16)
    wit_ref[...] = _dot_nt(wwit_ref[...], xb) * (IDX_HEADS ** -0.5)
    misc = _dot(xb, wmisc_ref[...])
    dt_ref[...] = misc[:, 0:SSD_HEADS]
    ki_ref[...] = misc[:, SSD_HEADS:SSD_HEADS + IDX_DIM].astype(BF16)


def _in_proj(x, wg, wz, wxbc, wq, wckv, wqit, wmisc, wwit, kvn, tm):
    t = x.shape[0]
    row = lambda i: (i, 0)
    col = lambda i: (0, i)
    out = [
        ((t, 2 * D_MODEL), BF16, (tm, 2 * D_MODEL), row),
        ((t, SSD_INNER), BF16, (tm, SSD_INNER), row),
        ((t, SSD_XBC), BF16, (tm, SSD_XBC), row),
        ((t, ATT_WIDTH), BF16, (tm, ATT_WIDTH), row),
        ((t, KV_RANK), BF16, (tm, KV_RANK), row),
        ((KV_RANK, t), BF16, (KV_RANK, tm), col),
        ((IDX_HEADS * IDX_DIM, t), BF16, (IDX_HEADS * IDX_DIM, tm), col),
        ((t, IDX_DIM), BF16, (tm, IDX_DIM), row),
        ((t, SSD_HEADS), F32, (tm, SSD_HEADS), row),
        ((IDX_HEADS, t), F32, (IDX_HEADS, tm), col),
    ]
    return pl.pallas_call(
        _in_proj_body,
        grid=(t // tm,),
        in_specs=[pl.BlockSpec((tm, D_MODEL), row)] + [_const_spec(w.shape) for w in
                                                        (wg, wz, wxbc, wq, wckv, wqit, wmisc, wwit, kvn)],
        out_specs=[pl.BlockSpec(blk, im) for _, _, blk, im in out],
        out_shape=[jax.ShapeDtypeStruct(shp, dt) for shp, dt, _, _ in out],
        compiler_params=_params("parallel"),
        name="in_proj",
    )(x, wg, wz, wxbc, wq, wckv, wqit, wmisc, wwit, kvn)


def _ssd_body(xbc_ref, sz_ref, dt_ref, cw_ref, cb_ref, dtb_ref, alog_ref, dskip_ref, nw_ref, expand_ref,
              y_ref, state_ref, ext_ref):
    chunk = SSD_CHUNK
    n_sub = xbc_ref.shape[0] // chunk
    hp = SSD_HEAD_DIM
    gw = SSD_INNER // SSD_GROUPS

    @pl.when(pl.program_id(1) == 0)
    def _():
        state_ref[...] = jnp.zeros(state_ref.shape, F32)
        ext_ref[0:CONV_TAIL, :] = jnp.zeros((CONV_TAIL, SSD_XBC), BF16)

    ext_ref[CONV_TAIL:, :] = xbc_ref[...]
    t_i = lax.broadcasted_iota(jnp.int32, (chunk, CONV_TAIL + chunk), 0)
    j_i = lax.broadcasted_iota(jnp.int32, (chunk, CONV_TAIL + chunk), 1)
    r_i = lax.broadcasted_iota(jnp.int32, (chunk, chunk), 0)
    c_i = lax.broadcasted_iota(jnp.int32, (chunk, chunk), 1)
    causal = r_i >= c_i
    causal_b = causal.astype(BF16)
    eye = (lax.broadcasted_iota(jnp.int32, (SSD_HEADS, SSD_HEADS), 0)
           == lax.broadcasted_iota(jnp.int32, (SSD_HEADS, SSD_HEADS), 1)).astype(BF16)
    expand = expand_ref[...]
    neg_a = -jnp.exp(alog_ref[...])
    state = state_ref[...]

    for sub in range(n_sub):
        rows = slice(sub * chunk, (sub + 1) * chunk)
        ext = ext_ref[sub * chunk:sub * chunk + CONV_TAIL + chunk, :]
        conv = cb_ref[...] + cw_ref[SSD_CONV - 1:SSD_CONV, :] * xbc_ref[rows, :].astype(F32)
        for k in range(SSD_CONV - 1):
            shift = (j_i == t_i + (CONV_TAIL - (SSD_CONV - 1) + k)).astype(BF16)
            conv = conv + cw_ref[k:k + 1, :] * _dot(shift, ext)
        act = conv * jax.nn.sigmoid(conv)
        xs = act[:, :SSD_INNER]

        pre = dt_ref[rows, :] + dtb_ref[...]
        dt = jnp.maximum(pre, 0.0) + jnp.log(1.0 + jnp.exp(-jnp.abs(pre)))
        a = dt * neg_a
        a_cs = _select_sum(causal_b, _split3(a), lhs_is_01=True)
        a_cs_3 = _split3(a_cs)
        a_cs_t = sum(_dot_nt(eye, piece) for piece in a_cs_3)
        dt_e = _select_sum(expand, _split3(dt), lhs_is_01=False)
        a_cs_e = _select_sum(expand, a_cs_3, lhs_is_01=False)
        a_end_e = a_cs_e[chunk - 1:chunk, :]

        xdt = xs * dt_e
        xdt_b = xdt.astype(BF16)
        xdt_end_b = (xdt * jnp.exp(a_end_e - a_cs_e)).astype(BF16)
        state_b = state.astype(BF16)

        y_diag, y_off, new_state = [], [], []
        for g in range(SSD_GROUPS):
            bm = act[:, SSD_INNER + g * SSD_STATE:SSD_INNER + (g + 1) * SSD_STATE].astype(BF16)
            cm = act[:, SSD_INNER + SSD_GN + g * SSD_STATE:SSD_INNER + SSD_GN + (g + 1) * SSD_STATE].astype(BF16)
            cb = _dot_nt(cm, bm)
            y_off.append(_dot(cm, state_b[:, g * gw:(g + 1) * gw]))
            new_state.append(lax.dot_general(bm, xdt_end_b[:, g * gw:(g + 1) * gw], TN_DIMS,
                                             preferred_element_type=F32))
            for r in range(SSD_HEADS // SSD_GROUPS):
                h = g * (SSD_HEADS // SSD_GROUPS) + r
                seg = a_cs[:, h:h + 1] - a_cs_t[h:h + 1, :]
                decay = jnp.exp(jnp.where(causal, seg, NEG_BIG))
                y_diag.append(_dot((cb * decay).astype(BF16), xdt_b[:, h * hp:(h + 1) * hp]))
        state = jnp.exp(a_end_e) * state + jnp.concatenate(new_state, axis=1)

        y = jnp.concatenate(y_diag, axis=1) + jnp.concatenate(y_off, axis=1) * jnp.exp(a_cs_e)
        y = (y + dskip_ref[...] * xs) * sz_ref[rows, :].astype(F32)
        outs = []
        for g in range(SSD_GROUPS):
            yg = y[:, g * gw:(g + 1) * gw]
            outs.append(yg * lax.rsqrt(jnp.mean(yg * yg, axis=-1, keepdims=True) + RMS_EPS))
        y_ref[rows, :] = (jnp.concatenate(outs, axis=1) * nw_ref[...]).astype(BF16)

    state_ref[...] = state
    ext_ref[0:CONV_TAIL, :] = ext_ref[n_sub * chunk:n_sub * chunk + CONV_TAIL, :]


def _ssd(xbc, sz, dt, conv_w, conv_b, dt_bias, a_log, d_skip_e, norm_w, expand, bsz, seqlen):
    t = xbc.shape[0]
    n_sub = SSD_STEP_CHUNKS if (seqlen // SSD_CHUNK) % SSD_STEP_CHUNKS == 0 else 1
    rows = n_sub * SSD_CHUNK
    ns = seqlen // rows
    row = lambda b, c: (b * ns + c, 0)
    return pl.pallas_call(
        _ssd_body,
        grid=(bsz, ns),
        in_specs=[pl.BlockSpec((rows, SSD_XBC), row), pl.BlockSpec((rows, SSD_INNER), row),
                  pl.BlockSpec((rows, SSD_HEADS), row)]
                 + [_const_spec(w.shape) for w in (conv_w, conv_b, dt_bias, a_log, d_skip_e, norm_w, expand)],
        out_specs=pl.BlockSpec((rows, SSD_INNER), row),
        out_shape=jax.ShapeDtypeStruct((t, SSD_INNER), BF16),
        scratch_shapes=[pltpu.VMEM((SSD_STATE, SSD_INNER), F32),
                        pltpu.VMEM((CONV_TAIL + rows, SSD_XBC), BF16)],
        compiler_params=_params("arbitrary", "arbitrary"),
        name="ssd",
    )(xbc, sz, dt, conv_w, conv_b, dt_bias, a_log, d_skip_e, norm_w, expand)


def _score_of_key(key):
    return pltpu.bitcast(jnp.where(key >= 0, key, key ^ INT_MAX), F32)


def _dsa_body(slopes_ref, q_ref, qit_ref, wit_ref, ckv_ref, ckvt_ref, ki_ref, wuk_ref, wuv_ref, y_ref,
              sc_ref, sb_ref, bias_ref, ql_ref, s_ref, p_ref, m_ref, l_ref, alpha_ref, acc_ref, *, topk, pos_bits):
    qb = q_ref.shape[0]
    kc = KEY_CHUNK
    t0 = pl.program_id(1) * qb
    n_chunks = (t0 + qb + kc - 1) // kc
    q_pos = t0 + lax.broadcasted_iota(jnp.int32, (1, qb), 1)

    def key_pos(c):
        return c * kc + lax.broadcasted_iota(jnp.int32, (kc, qb), 0)

    def chunk_ds(c):
        return pl.ds(pl.multiple_of(c * kc, kc), kc)

    def score_chunk(c, carry):
        kblk = ki_ref[chunk_ds(c), :]
        score = jnp.zeros((kc, qb), F32)
        for h in range(0, IDX_HEADS, 2):
            w = jnp.concatenate([qit_ref[h * IDX_DIM:(h + 1) * IDX_DIM, :],
                                 qit_ref[(h + 1) * IDX_DIM:(h + 2) * IDX_DIM, :]], axis=1)
            wi = jnp.concatenate([wit_ref[h:h + 1, :], wit_ref[h + 1:h + 2, :]], axis=1)
            t = jnp.maximum(_dot(kblk, w), 0.0) * wi
            score = score + t[:, :qb] + t[:, qb:]
        score = jnp.where(key_pos(c) <= q_pos, score, -jnp.inf)
        sc_ref[chunk_ds(c), :] = score
        sb_ref[chunk_ds(c), :] = score.astype(BF16)
        return carry

    lax.fori_loop(0, n_chunks, score_chunk, 0)

    def count(pred):
        n_acc = 8

        def body(c, parts):
            hit = pred(sc_ref[chunk_ds(c), :], key_pos(c))
            parts = list(parts)
            for i in range(kc // SUBLANES):
                tile = hit[i * SUBLANES:(i + 1) * SUBLANES, :]
                parts[i % n_acc] = jnp.where(tile, parts[i % n_acc] + 1, parts[i % n_acc])
            return tuple(parts)
        parts = lax.fori_loop(0, n_chunks, body, (jnp.zeros((SUBLANES, qb), jnp.int32),) * n_acc)
        return jnp.sum(sum(parts), axis=0, keepdims=True)

    def count_rounded(cand_b):
        n_acc = 4

        def body(c, parts):
            hit = jnp.where(sb_ref[chunk_ds(c), :] >= cand_b, jnp.ones((kc, qb), BF16), jnp.zeros((kc, qb), BF16))
            parts = list(parts)
            for i in range(kc // BF16_ROWS):
                parts[i % n_acc] = parts[i % n_acc] + hit[i * BF16_ROWS:(i + 1) * BF16_ROWS, :]
            return tuple(parts)
        parts = lax.fori_loop(0, n_chunks, body, (jnp.zeros((BF16_ROWS, qb), BF16),) * n_acc)
        total = sum(p.astype(F32) for p in parts)
        return jnp.sum(total, axis=0, keepdims=True).astype(jnp.int32)

    def coarse_bit(i, key16):
        cand = key16 + (jnp.int32(1) << (15 - i))
        pattern = jnp.where(cand >= 0, cand, cand ^ 0x7FFF)
        cand_b = pltpu.bitcast(pattern << 16, F32).astype(BF16)
        return jnp.where(count_rounded(cand_b) >= topk, cand, key16)

    key16 = lax.fori_loop(0, 16, coarse_bit, jnp.full((1, qb), -(2 ** 15), jnp.int32))
    coarse = (key16 << 16) + jnp.where(key16 < 0, 0xFFFF, 0)
    window_lo = jnp.where(coarse < INT_MIN + 2 ** 16, INT_MIN, coarse - 2 ** 16)

    def thr_bit(i, state):
        thr, n_ge = state
        cand = thr + (jnp.int32(1) << (16 - i))
        cand_f = _score_of_key(cand)
        n = count(lambda s, p: s >= cand_f)
        return jnp.where(n >= topk, cand, thr), jnp.where(n >= topk, n, n_ge)

    def pending(state):
        return jnp.max(jnp.where(state[1] <= topk, 0, 1))

    state = (window_lo, jnp.full((1, qb), INT_MAX, jnp.int32))
    state = lax.fori_loop(0, THR_BITS_ALWAYS, thr_bit, state)

    def more_bits(carry):
        i, state, _ = carry
        for b in range(THR_BITS_STEP):
            state = thr_bit(i + b, state)
        return i + THR_BITS_STEP, state, pending(state)

    _, (thr, _), still_open = lax.while_loop(lambda carry: (carry[0] < 17) & (carry[2] > 0), more_bits,
                                             (jnp.int32(THR_BITS_ALWAYS), state, pending(state)))
    thr_f = jnp.where(thr >= LOWEST_FINITE_KEY, _score_of_key(jnp.maximum(thr, LOWEST_FINITE_KEY)), -jnp.inf)

    eye = (lax.broadcasted_iota(jnp.int32, (qb, qb), 0)
           == lax.broadcasted_iota(jnp.int32, (qb, qb), 1)).astype(BF16)

    def write_bias(selected):
        def bias_chunk(c, carry):
            p = key_pos(c)
            sel = (p <= q_pos) & selected(sc_ref[chunk_ds(c), :], p)
            bias_ref[:, chunk_ds(c)] = _dot_nt(eye, jnp.where(sel, 0.0, NEG_BIG).astype(BF16))
            return carry
        lax.fori_loop(0, n_chunks, bias_chunk, 0)

    def bias_no_ties():
        write_bias(lambda s, p: s >= thr_f)

    def bias_checking_ties():
        need = topk - count(lambda s, p: s > thr_f)
        n_tie = count(lambda s, p: (s == thr_f) & (p <= q_pos))

        def bias_with_ties():
            def cut_bit(i, cut):
                cand = cut + (jnp.int32(1) << (pos_bits - 1 - i))
                n_before = count(lambda s, p: (s == thr_f) & (p <= q_pos) & (p < cand))
                return jnp.where(n_before < need, cand, cut)
            cut = lax.fori_loop(0, pos_bits, cut_bit, jnp.zeros((1, qb), jnp.int32))
            write_bias(lambda s, p: (s > thr_f) | ((s == thr_f) & (p <= cut)))

        lax.cond(jnp.max(jnp.where(n_tie > need, 1, 0)) > 0, bias_with_ties, bias_no_ties)

    lax.cond(still_open > 0, bias_checking_ties, bias_no_ties)

    for h in range(ATT_HEADS):
        rows = slice(h * qb, (h + 1) * qb)
        ql = _dot(q_ref[:, h * ATT_HEAD_DIM:(h + 1) * ATT_HEAD_DIM], wuk_ref[h])
        ql_ref[rows, :] = (ql * (ATT_HEAD_DIM ** -0.5 * LOG2E)).astype(BF16)
        m_ref[rows, :] = jnp.full((qb, LANES), NEG_BIG, F32)
        l_ref[rows, :] = jnp.zeros((qb, LANES), F32)
        acc_ref[rows, :] = jnp.zeros((qb, KV_RANK), F32)
    lane_tiles = kc // LANES
    heads_per_part = ATT_HEADS // MXU_SPLIT

    part_rows = [slice(r * heads_per_part * qb, (r + 1) * heads_per_part * qb) for r in range(MXU_SPLIT)]

    def logits(part, c):
        rel = (c * kc - t0 + lax.broadcasted_iota(jnp.int32, (1, kc), 1)).astype(F32)
        s_part = _dot(ql_ref[part_rows[part], :], ckvt_ref[:, chunk_ds(c)])
        for i_h in range(heads_per_part):
            h = part * heads_per_part + i_h
            rows = slice(h * qb, (h + 1) * qb)
            s = s_part[i_h * qb:(i_h + 1) * qb, :] + (slopes_ref[h] * LOG2E) * rel + bias_ref[:, chunk_ds(c)]
            s_ref[rows, :] = s
            mx = s[:, 0:LANES]
            for i in range(1, lane_tiles):
                mx = jnp.maximum(mx, s[:, i * LANES:(i + 1) * LANES])
            m_old = m_ref[rows, :]
            m_new = jnp.maximum(m_old, jnp.max(mx, axis=-1, keepdims=True))
            m_ref[rows, :] = m_new
            alpha_ref[rows, :] = jnp.exp2(m_old - m_new)

    def values(part, c):
        cblk = ckv_ref[chunk_ds(c), :]
        for i_h in range(heads_per_part):
            rows = slice((part * heads_per_part + i_h) * qb, (part * heads_per_part + i_h + 1) * qb)
            m = m_ref[rows, :]
            l = alpha_ref[rows, :] * l_ref[rows, :]
            for i in range(lane_tiles):
                p = jnp.exp2(s_ref[rows, i * LANES:(i + 1) * LANES] - m)
                l = l + p
                p_ref[rows, i * LANES:(i + 1) * LANES] = p.astype(BF16)
            l_ref[rows, :] = l
        upd = _dot(p_ref[part_rows[part], :], cblk)
        for i_h in range(heads_per_part):
            rows = slice((part * heads_per_part + i_h) * qb, (part * heads_per_part + i_h + 1) * qb)
            alpha = alpha_ref[rows, :]
            for i in range(KV_RANK // LANES):
                cols = slice(i * LANES, (i + 1) * LANES)
                acc_ref[rows, cols] = alpha * acc_ref[rows, cols] + upd[i_h * qb:(i_h + 1) * qb, cols]

    last = n_chunks - 1
    for part in range(MXU_SPLIT):
        logits(part, 0)

    def att_chunk(c, carry):
        for part in range(MXU_SPLIT):
            values(part, c)
            logits(part, c + 1)
        return carry

    lax.fori_loop(0, last, att_chunk, 0)
    for part in range(MXU_SPLIT):
        values(part, last)

    o = (acc_ref[...] / jnp.sum(l_ref[...], axis=-1, keepdims=True)).astype(BF16)
    y_ref[...] = jnp.concatenate([_dot(o[h * qb:(h + 1) * qb, :], wuv_ref[h]) for h in range(ATT_HEADS)],
                                 axis=1).astype(BF16)


def _dsa(q, qit, wit, ckv, ckvt, ki, w_uk, w_uv, slopes, bsz, seqlen):
    t = q.shape[0]
    qb = Q_BLOCK
    nq = seqlen // qb
    rows = ATT_HEADS * qb
    topk = min(TOPK_MAX, seqlen // 4)
    pos_bits = max(1, (seqlen - 1).bit_length())
    seq_pad = -(-seqlen // KEY_CHUNK) * KEY_CHUNK
    row = lambda b, j: (b * nq + j, 0)
    col = lambda b, j: (0, b * nq + j)
    per_batch = lambda b, j: (b, 0, 0)
    if seq_pad != seqlen:
        ckv = jnp.pad(ckv.reshape(bsz, seqlen, KV_RANK), ((0, 0), (0, seq_pad - seqlen), (0, 0)))
        ki = jnp.pad(ki.reshape(bsz, seqlen, IDX_DIM), ((0, 0), (0, seq_pad - seqlen), (0, 0)))
    ckvt = ckvt.reshape(KV_RANK, bsz, seqlen)
    if seq_pad != seqlen:
        ckvt = jnp.pad(ckvt, ((0, 0), (0, 0), (0, seq_pad - seqlen)))
    ckvt = ckvt.reshape(KV_RANK, bsz * seq_pad)
    body = functools.partial(_dsa_body, topk=topk, pos_bits=pos_bits)
    return pl.pallas_call(
        body,
        grid=(bsz, nq),
        in_specs=[pl.BlockSpec(memory_space=pltpu.SMEM),
                  pl.BlockSpec((qb, ATT_WIDTH), row), pl.BlockSpec((IDX_HEADS * IDX_DIM, qb), col),
                  pl.BlockSpec((IDX_HEADS, qb), col),
                  pl.BlockSpec((None, seq_pad, KV_RANK), per_batch, pipeline_mode=pl.Buffered(1)),
                  pl.BlockSpec((KV_RANK, seq_pad), lambda b, j: (0, b), pipeline_mode=pl.Buffered(1)),
                  pl.BlockSpec((None, seq_pad, IDX_DIM), per_batch, pipeline_mode=pl.Buffered(1)),
                  _const_spec(w_uk.shape), _const_spec(w_uv.shape)],
        out_specs=pl.BlockSpec((qb, ATT_WIDTH), row),
        out_shape=jax.ShapeDtypeStruct((t, ATT_WIDTH), BF16),
        scratch_shapes=[pltpu.VMEM((seq_pad, qb), F32), pltpu.VMEM((seq_pad, qb), BF16),
                        pltpu.VMEM((qb, seq_pad), F32),
                        pltpu.VMEM((rows, KV_RANK), BF16),
                        pltpu.VMEM((rows, KEY_CHUNK), F32), pltpu.VMEM((rows, KEY_CHUNK), BF16),
                        pltpu.VMEM((rows, LANES), F32), pltpu.VMEM((rows, LANES), F32),
                        pltpu.VMEM((rows, LANES), F32), pltpu.VMEM((rows, KV_RANK), F32)],
        compiler_params=_params("arbitrary", "arbitrary"),
        name="dsa",
    )(slopes, q, qit, wit, ckv.reshape(bsz, seq_pad, KV_RANK), ckvt, ki.reshape(bsz, seq_pad, IDX_DIM), w_uk, w_uv)


def _merge_ln_body(x_ref, ys_ref, ya_ref, sg_ref, wps_ref, wpa_ref, wo_ref, g_ref, b_ref, o_ref):
    sg = sg_ref[...].astype(F32)
    merged = (sg[:, :D_MODEL] * _dot(ys_ref[...], wps_ref[...])
              + sg[:, D_MODEL:] * _dot(ya_ref[...], wpa_ref[...]))
    out = _dot(merged.astype(BF16), wo_ref[...])
    o_ref[...] = _layer_norm(ALPHA * x_ref[...] + out, g_ref[...], b_ref[...])


def _merge_ln(x, ys, ya, sg, wps, wpa, wo, g, b, tm):
    t = x.shape[0]
    row = lambda i: (i, 0)
    return pl.pallas_call(
        _merge_ln_body,
        grid=(t // tm,),
        in_specs=[pl.BlockSpec((tm, D_MODEL), row), pl.BlockSpec((tm, SSD_INNER), row),
                  pl.BlockSpec((tm, ATT_WIDTH), row), pl.BlockSpec((tm, 2 * D_MODEL), row)]
                 + [_const_spec(w.shape) for w in (wps, wpa, wo, g, b)],
        out_specs=pl.BlockSpec((tm, D_MODEL), row),
        out_shape=jax.ShapeDtypeStruct((t, D_MODEL), F32),
        compiler_params=_params("parallel"),
        name="merge_ln",
    )(x, ys, ya, sg, wps, wpa, wo, g, b)


def _mem_kv_body(m_ref, w_ref, o_ref):
    o_ref[...] = _dot(m_ref[...].astype(BF16), w_ref[...]).astype(BF16)


def _mem_kv(mem, w_mkv, bsz, mem_len):
    row = lambda b: (b, 0)
    return pl.pallas_call(
        _mem_kv_body,
        grid=(bsz,),
        in_specs=[pl.BlockSpec((mem_len, D_MODEL), row), _const_spec(w_mkv.shape)],
        out_specs=pl.BlockSpec((mem_len, 2 * D_MODEL), row),
        out_shape=jax.ShapeDtypeStruct((bsz * mem_len, 2 * D_MODEL), BF16),
        compiler_params=_params("parallel"),
        name="mem_kv",
    )(mem, w_mkv)


def _xattn_ln_body(x_ref, kv_ref, wq_ref, wo_ref, g_ref, b_ref, o_ref):
    x = x_ref[...]
    q = (_dot(x.astype(BF16), wq_ref[...]) * (MEM_HEAD_DIM ** -0.5)).astype(BF16)
    heads = []
    for h in range(MEM_HEADS):
        lo = h * MEM_HEAD_DIM
        s = _dot_nt(q[:, lo:lo + MEM_HEAD_DIM], kv_ref[:, lo:lo + MEM_HEAD_DIM])
        p = jnp.exp(s - jnp.max(s, axis=-1, keepdims=True))
        o = _dot(p.astype(BF16), kv_ref[:, D_MODEL + lo:D_MODEL + lo + MEM_HEAD_DIM])
        heads.append((o / jnp.sum(p, axis=-1, keepdims=True)).astype(BF16))
    out = _dot(jnp.concatenate(heads, axis=1), wo_ref[...])
    o_ref[...] = _layer_norm(ALPHA * x + out, g_ref[...], b_ref[...])


def _xattn_ln(x, kv, wq, wo, g, b, bsz, seqlen, mem_len, tm):
    t = x.shape[0]
    nt = seqlen // tm
    return pl.pallas_call(
        _xattn_ln_body,
        grid=(bsz, nt),
        in_specs=[pl.BlockSpec((tm, D_MODEL), lambda b_, i: (b_ * nt + i, 0)),
                  pl.BlockSpec((mem_len, 2 * D_MODEL), lambda b_, i: (b_, 0))]
                 + [_const_spec(w.shape) for w in (wq, wo, g, b)],
        out_specs=pl.BlockSpec((tm, D_MODEL), lambda b_, i: (b_ * nt + i, 0)),
        out_shape=jax.ShapeDtypeStruct((t, D_MODEL), F32),
        compiler_params=_params("parallel", "parallel"),
        name="xattn_ln",
    )(x, kv, wq, wo, g, b)


def _split_w_in(w_in):
    sizes = (D_MODEL, D_MODEL, SSD_INNER, SSD_XBC, SSD_HEADS, ATT_WIDTH, KV_RANK, IDX_HEADS * IDX_DIM,
             IDX_DIM, IDX_HEADS)
    parts, off = [], 0
    for s in sizes:
        parts.append(w_in[:, off:off + s])
        off += s
    g_ssd, g_att, z, xbc, dt, q, ckv, qi, ki, wi = parts
    misc = jnp.concatenate([dt, ki, jnp.zeros((w_in.shape[0], LANES - SSD_HEADS - IDX_DIM), w_in.dtype)], axis=1)
    return [w.astype(BF16) for w in (jnp.concatenate([g_ssd, g_att], axis=1), z, xbc, q, ckv, qi.T, misc, wi.T)]


def _layer(x, mem, bsz, seqlen, mem_len, p):
    (ffn1_w_in, ffn1_w_out, ln1_g, ln1_b, w_in, conv_w, conv_b, dt_bias, a_log, d_skip, ssd_norm_w,
     kv_norm_w, w_uk, w_uv, w_proj_ssd, w_proj_att, w_out, ln2_g, ln2_b,
     w_mq, w_mkv, w_mo, ln3_g, ln3_b, ffn2_w_in, ffn2_w_out, ln4_g, ln4_b) = p
    tm = min(TOKEN_TILE, seqlen)
    vec = lambda v: v.reshape(1, -1)
    bf = lambda w: w.astype(BF16)

    x = _ffn_ln(x, bf(ffn1_w_in), bf(ffn1_w_out), vec(ln1_g), vec(ln1_b), tm)

    sg, sz, xbc, q, ckv, ckvt, qit, ki, dt, wit = _in_proj(x, *_split_w_in(w_in), vec(kv_norm_w), tm)

    expand = jnp.repeat(jnp.eye(SSD_HEADS, dtype=BF16), SSD_HEAD_DIM, axis=1)
    y_ssd = _ssd(xbc, sz, dt, conv_w, vec(conv_b), vec(dt_bias), vec(a_log),
                 vec(jnp.repeat(d_skip, SSD_HEAD_DIM)), vec(ssd_norm_w), expand, bsz, seqlen)

    slopes = 2.0 ** (-8.0 * jnp.arange(1, ATT_HEADS + 1, dtype=F32) / ATT_HEADS)
    y_att = _dsa(q, qit, wit, ckv, ckvt, ki, bf(w_uk), bf(w_uv), slopes, bsz, seqlen)

    x = _merge_ln(x, y_ssd, y_att, sg, bf(w_proj_ssd), bf(w_proj_att), bf(w_out), vec(ln2_g), vec(ln2_b), tm)

    kv = _mem_kv(mem, bf(w_mkv), bsz, mem_len)
    x = _xattn_ln(x, kv, bf(w_mq), bf(w_mo), vec(ln3_g), vec(ln3_b), bsz, seqlen, mem_len, tm)

    return _ffn_ln(x, bf(ffn2_w_in), bf(ffn2_w_out), vec(ln4_g), vec(ln4_b), tm)


def kernel(x, mem, ffn1_w_in, ffn1_w_out, ln1_g, ln1_b, w_in, conv_w, conv_b, dt_bias, a_log, d_skip,
           ssd_norm_w, kv_norm_w, w_uk, w_uv, w_proj_ssd, w_proj_att, w_out, ln2_g, ln2_b,
           w_mq, w_mkv, w_mo, ln3_g, ln3_b, ffn2_w_in, ffn2_w_out, ln4_g, ln4_b):
    bsz, seqlen, d = x.shape
    mem_len = mem.shape[1]
    params = (ffn1_w_in, ffn1_w_out, ln1_g, ln1_b, w_in, conv_w, conv_b, dt_bias, a_log, d_skip, ssd_norm_w,
              kv_norm_w, w_uk, w_uv, w_proj_ssd, w_proj_att, w_out, ln2_g, ln2_b,
              w_mq, w_mkv, w_mo, ln3_g, ln3_b, ffn2_w_in, ffn2_w_out, ln4_g, ln4_b)
    h = x.reshape(bsz * seqlen, d)
    m = mem.reshape(bsz * mem_len, d)
    for layer in range(ffn1_w_in.shape[0]):
        h = _layer(h, m, bsz, seqlen, mem_len, tuple(w[layer] for w in params))
    return h.reshape(bsz, seqlen, d)
```

```python
import functools
import math

import jax
import jax.numpy as jnp
from jax import lax
from jax.experimental import pallas as pl
from jax.experimental.pallas import tpu as pltpu

F32 = jnp.float32
BF16 = jnp.bfloat16

D_MODEL = 1024
DEPTH = 1
SSD_HEADS = 16
SSD_HEAD_DIM = 64
SSD_INNER = SSD_HEADS * SSD_HEAD_DIM
SSD_GROUPS = 4
SSD_STATE = 128
SSD_CONV = 4
SSD_CHUNK = 128
SSD_STEP_CHUNKS = 2
SSD_GN = SSD_GROUPS * SSD_STATE
SSD_XBC = SSD_INNER + 2 * SSD_GN
ATT_HEADS = 16
ATT_HEAD_DIM = 64
ATT_WIDTH = ATT_HEADS * ATT_HEAD_DIM
KV_RANK = 256
IDX_HEADS = 16
IDX_DIM = 64
TOPK_MAX = 256
Q_BLOCK = 256
MEM_HEADS = 4
MEM_HEAD_DIM = D_MODEL // MEM_HEADS
D_FF = 2816
ALPHA = (2.0 * DEPTH) ** 0.25
LN_EPS = 1e-5
RMS_EPS = 1e-6

INT_MIN = -(2 ** 31)
INT_MAX = 2 ** 31 - 1
LOWEST_FINITE_KEY = INT_MIN + 2 ** 23
NEG_BIG = -1e30
LOG2E = math.log2(math.e)

LANES = 128
SUBLANES = 8
BF16_ROWS = 16
TOKEN_TILE = 512
SUB_TILES = 2
FFN_CHUNK = 2816
KEY_CHUNK = 512
THR_BITS_ALWAYS = 9
THR_BITS_STEP = 2
MXU_SPLIT = 4
CONV_TAIL = 16
VMEM_LIMIT = 56 * 1024 * 1024

NT_DIMS = (((1,), (1,)), ((), ()))
TN_DIMS = (((0,), (0,)), ((), ()))


def _const_spec(shape):
    return pl.BlockSpec(shape, lambda *_: (0,) * len(shape), pipeline_mode=pl.Buffered(1))


def _params(*sem):
    return pltpu.CompilerParams(dimension_semantics=sem, vmem_limit_bytes=VMEM_LIMIT)


def _layer_norm(y, g, b):
    mu = jnp.mean(y, axis=-1, keepdims=True)
    d = y - mu
    var = jnp.mean(d * d, axis=-1, keepdims=True)
    return d * lax.rsqrt(var + LN_EPS) * g + b


def _dot(a, b):
    return jnp.dot(a, b, preferred_element_type=F32)


def _dot_nt(a, b):
    return lax.dot_general(a, b, NT_DIMS, preferred_element_type=F32)


def _split3(x):
    hi = x.astype(BF16)
    rest = x - hi.astype(F32)
    mid = rest.astype(BF16)
    return hi, mid, (rest - mid.astype(F32)).astype(BF16)


def _select_sum(sel01, pieces, lhs_is_01):
    if lhs_is_01:
        return sum(_dot(sel01, p) for p in pieces)
    return sum(_dot(p, sel01) for p in pieces)


def _ffn_ln_body(x_ref, win_ref, wout_ref, g_ref, b_ref, o_ref):
    x = x_ref[...]
    xb = x.astype(BF16)
    acc = jnp.zeros(x.shape, F32)
    for c in range(D_FF // FFN_CHUNK):
        lo = c * FFN_CHUNK
        gate = _dot(xb, win_ref[:, lo:lo + FFN_CHUNK])
        up = _dot(xb, win_ref[:, D_FF + lo:D_FF + lo + FFN_CHUNK])
        act = (gate * jax.nn.sigmoid(gate) * up).astype(BF16)
        acc = acc + _dot(act, wout_ref[lo:lo + FFN_CHUNK, :])
    o_ref[...] = _layer_norm(ALPHA * x + 0.5 * acc, g_ref[...], b_ref[...])


def _ffn_ln(x, w_in, w_out, g, b, tm):
    t = x.shape[0]
    row = lambda i: (i, 0)
    return pl.pallas_call(
        _ffn_ln_body,
        grid=(t // tm,),
        in_specs=[pl.BlockSpec((tm, D_MODEL), row),
                  _const_spec(w_in.shape), _const_spec(w_out.shape),
                  _const_spec(g.shape), _const_spec(b.shape)],
        out_specs=pl.BlockSpec((tm, D_MODEL), row),
        out_shape=jax.ShapeDtypeStruct((t, D_MODEL), F32),
        compiler_params=_params("parallel"),
        name="ffn_ln",
    )(x, w_in, w_out, g, b)


def _in_proj_body(x_ref, wg_ref, wz_ref, wxbc_ref, wq_ref, wckv_ref, wqit_ref, wmisc_ref, wwit_ref, kvn_ref,
                  sg_ref, sz_ref, xbc_ref, q_ref, ckv_ref, ckvt_ref, qit_ref, ki_ref, dt_ref, wit_ref):
    xb = x_ref[...].astype(BF16)
    sg_ref[...] = jax.nn.sigmoid(_dot(xb, wg_ref[...])).astype(BF16)
    z = _dot(xb, wz_ref[...])
    sz_ref[...] = (z * jax.nn.sigmoid(z)).astype(BF16)
    xbc_ref[...] = _dot(xb, wxbc_ref[...]).astype(BF16)
    q_ref[...] = _dot(xb, wq_ref[...]).astype(BF16)
    c = _dot(xb, wckv_ref[...])
    c = c * lax.rsqrt(jnp.mean(c * c, axis=-1, keepdims=True) + RMS_EPS) * kvn_ref[...]
    ckv_ref[...] = c.astype(BF16)
    ckvt_ref[...] = c.T.astype(BF16)
    qit_ref[...] = _dot_nt(wqit_ref[...], xb).astype(BF16)
    wit_ref[...] = _dot_nt(wwit_ref[...], xb) * (IDX_HEADS ** -0.5)
    misc = _dot(xb, wmisc_ref[...])
    dt_ref[...] = misc[:, 0:SSD_HEADS]
    ki_ref[...] = misc[:, SSD_HEADS:SSD_HEADS + IDX_DIM].astype(BF16)


def _in_proj(x, wg, wz, wxbc, wq, wckv, wqit, wmisc, wwit, kvn, tm):
    t = x.shape[0]
    row = lambda i: (i, 0)
    col = lambda i: (0, i)
    out = [
        ((t, 2 * D_MODEL), BF16, (tm, 2 * D_MODEL), row),
        ((t, SSD_INNER), BF16, (tm, SSD_INNER), row),
        ((t, SSD_XBC), BF16, (tm, SSD_XBC), row),
        ((t, ATT_WIDTH), BF16, (tm, ATT_WIDTH), row),
        ((t, KV_RANK), BF16, (tm, KV_RANK), row),
        ((KV_RANK, t), BF16, (KV_RANK, tm), col),
        ((IDX_HEADS * IDX_DIM, t), BF16, (IDX_HEADS * IDX_DIM, tm), col),
        ((t, IDX_DIM), BF16, (tm, IDX_DIM), row),
        ((t, SSD_HEADS), F32, (tm, SSD_HEADS), row),
        ((IDX_HEADS, t), F32, (IDX_HEADS, tm), col),
    ]
    return pl.pallas_call(
        _in_proj_body,
        grid=(t // tm,),
        in_specs=[pl.BlockSpec((tm, D_MODEL), row)] + [_const_spec(w.shape) for w in
                                                        (wg, wz, wxbc, wq, wckv, wqit, wmisc, wwit, kvn)],
        out_specs=[pl.BlockSpec(blk, im) for _, _, blk, im in out],
        out_shape=[jax.ShapeDtypeStruct(shp, dt) for shp, dt, _, _ in out],
        compiler_params=_params("parallel"),
        name="in_proj",
    )(x, wg, wz, wxbc, wq, wckv, wqit, wmisc, wwit, kvn)


def _ssd_body(xbc_ref, sz_ref, dt_ref, cw_ref, cb_ref, dtb_ref, alog_ref, dskip_ref, nw_ref, expand_ref,
              y_ref, state_ref, ext_ref):
    chunk = SSD_CHUNK
    n_sub = xbc_ref.shape[0] // chunk
    hp = SSD_HEAD_DIM
    gw = SSD_INNER // SSD_GROUPS

    @pl.when(pl.program_id(1) == 0)
    def _():
        state_ref[...] = jnp.zeros(state_ref.shape, F32)
        ext_ref[0:CONV_TAIL, :] = jnp.zeros((CONV_TAIL, SSD_XBC), BF16)

    ext_ref[CONV_TAIL:, :] = xbc_ref[...]
    t_i = lax.broadcasted_iota(jnp.int32, (chunk, CONV_TAIL + chunk), 0)
    j_i = lax.broadcasted_iota(jnp.int32, (chunk, CONV_TAIL + chunk), 1)
    r_i = lax.broadcasted_iota(jnp.int32, (chunk, chunk), 0)
    c_i = lax.broadcasted_iota(jnp.int32, (chunk, chunk), 1)
    causal = r_i >= c_i
    causal_b = causal.astype(BF16)
    eye = (lax.broadcasted_iota(jnp.int32, (SSD_HEADS, SSD_HEADS), 0)
           == lax.broadcasted_iota(jnp.int32, (SSD_HEADS, SSD_HEADS), 1)).astype(BF16)
    expand = expand_ref[...]
    neg_a = -jnp.exp(alog_ref[...])
    state = state_ref[...]

    for sub in range(n_sub):
        rows = slice(sub * chunk, (sub + 1) * chunk)
        ext = ext_ref[sub * chunk:sub * chunk + CONV_TAIL + chunk, :]
        conv = cb_ref[...] + cw_ref[SSD_CONV - 1:SSD_CONV, :] * xbc_ref[rows, :].astype(F32)
        for k in range(SSD_CONV - 1):
            shift = (j_i == t_i + (CONV_TAIL - (SSD_CONV - 1) + k)).astype(BF16)
            conv = conv + cw_ref[k:k + 1, :] * _dot(shift, ext)
        act = conv * jax.nn.sigmoid(conv)
        xs = act[:, :SSD_INNER]

        pre = dt_ref[rows, :] + dtb_ref[...]
        dt = jnp.maximum(pre, 0.0) + jnp.log(1.0 + jnp.exp(-jnp.abs(pre)))
        a = dt * neg_a
        a_cs = _select_sum(causal_b, _split3(a), lhs_is_01=True)
        a_cs_3 = _split3(a_cs)
        a_cs_t = sum(_dot_nt(eye, piece) for piece in a_cs_3)
        dt_e = _select_sum(expand, _split3(dt), lhs_is_01=False)
        a_cs_e = _select_sum(expand, a_cs_3, lhs_is_01=False)
        a_end_e = a_cs_e[chunk - 1:chunk, :]

        xdt = xs * dt_e
        xdt_b = xdt.astype(BF16)
        xdt_end_b = (xdt * jnp.exp(a_end_e - a_cs_e)).astype(BF16)
        state_b = state.astype(BF16)

        y_diag, y_off, new_state = [], [], []
        for g in range(SSD_GROUPS):
            bm = act[:, SSD_INNER + g * SSD_STATE:SSD_INNER + (g + 1) * SSD_STATE].astype(BF16)
            cm = act[:, SSD_INNER + SSD_GN + g * SSD_STATE:SSD_INNER + SSD_GN + (g + 1) * SSD_STATE].astype(BF16)
            cb = _dot_nt(cm, bm)
            y_off.append(_dot(cm, state_b[:, g * gw:(g + 1) * gw]))
            new_state.append(lax.dot_general(bm, xdt_end_b[:, g * gw:(g + 1) * gw], TN_DIMS,
                                             preferred_element_type=F32))
            for r in range(SSD_HEADS // SSD_GROUPS):
                h = g * (SSD_HEADS // SSD_GROUPS) + r
                seg = a_cs[:, h:h + 1] - a_cs_t[h:h + 1, :]
                decay = jnp.exp(jnp.where(causal, seg, NEG_BIG))
                y_diag.append(_dot((cb * decay).astype(BF16), xdt_b[:, h * hp:(h + 1) * hp]))
        state = jnp.exp(a_end_e) * state + jnp.concatenate(new_state, axis=1)

        y = jnp.concatenate(y_diag, axis=1) + jnp.concatenate(y_off, axis=1) * jnp.exp(a_cs_e)
        y = (y + dskip_ref[...] * xs) * sz_ref[rows, :].astype(F32)
        outs = []
        for g in range(SSD_GROUPS):
            yg = y[:, g * gw:(g + 1) * gw]
            outs.append(yg * lax.rsqrt(jnp.mean(yg * yg, axis=-1, keepdims=True) + RMS_EPS))
        y_ref[rows, :] = (jnp.concatenate(outs, axis=1) * nw_ref[...]).astype(BF16)

    state_ref[...] = state
    ext_ref[0:CONV_TAIL, :] = ext_ref[n_sub * chunk:n_sub * chunk + CONV_TAIL, :]


def _ssd(xbc, sz, dt, conv_w, conv_b, dt_bias, a_log, d_skip_e, norm_w, expand, bsz, seqlen):
    t = xbc.shape[0]
    n_sub = SSD_STEP_CHUNKS if (seqlen // SSD_CHUNK) % SSD_STEP_CHUNKS == 0 else 1
    rows = n_sub * SSD_CHUNK
    ns = seqlen // rows
    row = lambda b, c: (b * ns + c, 0)
    return pl.pallas_call(
        _ssd_body,
        grid=(bsz, ns),
        in_specs=[pl.BlockSpec((rows, SSD_XBC), row), pl.BlockSpec((rows, SSD_INNER), row),
                  pl.BlockSpec((rows, SSD_HEADS), row)]
                 + [_const_spec(w.shape) for w in (conv_w, conv_b, dt_bias, a_log, d_skip_e, norm_w, expand)],
        out_specs=pl.BlockSpec((rows, SSD_INNER), row),
        out_shape=jax.ShapeDtypeStruct((t, SSD_INNER), BF16),
        scratch_shapes=[pltpu.VMEM((SSD_STATE, SSD_INNER), F32),
                        pltpu.VMEM((CONV_TAIL + rows, SSD_XBC), BF16)],
        compiler_params=_params("arbitrary", "arbitrary"),
        name="ssd",
    )(xbc, sz, dt, conv_w, conv_b, dt_bias, a_log, d_skip_e, norm_w, expand)


def _score_of_key(key):
    return pltpu.bitcast(jnp.where(key >= 0, key, key ^ INT_MAX), F32)


def _dsa_body(slopes_ref, q_ref, qit_ref, wit_ref, ckv_ref, ckvt_ref, ki_ref, wuk_ref, wuv_ref, y_ref,
              sc_ref, sb_ref, bias_ref, ql_ref, s_ref, p_ref, m_ref, l_ref, alpha_ref, acc_ref, *, topk, pos_bits):
    qb = q_ref.shape[0]
    kc = KEY_CHUNK
    t0 = pl.program_id(1) * qb
    n_chunks = (t0 + qb + kc - 1) // kc
    q_pos = t0 + lax.broadcasted_iota(jnp.int32, (1, qb), 1)

    def key_pos(c):
        return c * kc + lax.broadcasted_iota(jnp.int32, (kc, qb), 0)

    def chunk_ds(c):
        return pl.ds(pl.multiple_of(c * kc, kc), kc)

    def score_chunk(c, carry):
        kblk = ki_ref[chunk_ds(c), :]
        score = jnp.zeros((kc, qb), F32)
        for h in range(0, IDX_HEADS, 2):
            w = jnp.concatenate([qit_ref[h * IDX_DIM:(h + 1) * IDX_DIM, :],
                                 qit_ref[(h + 1) * IDX_DIM:(h + 2) * IDX_DIM, :]], axis=1)
            wi = jnp.concatenate([wit_ref[h:h + 1, :], wit_ref[h + 1:h + 2, :]], axis=1)
            t = jnp.maximum(_dot(kblk, w), 0.0) * wi
            score = score + t[:, :qb] + t[:, qb:]
        score = jnp.where(key_pos(c) <= q_pos, score, -jnp.inf)
        sc_ref[chunk_ds(c), :] = score
        sb_ref[chunk_ds(c), :] = score.astype(BF16)
        return carry

    lax.fori_loop(0, n_chunks, score_chunk, 0)

    def count(pred):
        n_acc = 8

        def body(c, parts):
            hit = pred(sc_ref[chunk_ds(c), :], key_pos(c))
            parts = list(parts)
            for i in range(kc // SUBLANES):
                tile = hit[i * SUBLANES:(i + 1) * SUBLANES, :]
                parts[i % n_acc] = jnp.where(tile, parts[i % n_acc] + 1, parts[i % n_acc])
            return tuple(parts)
        parts = lax.fori_loop(0, n_chunks, body, (jnp.zeros((SUBLANES, qb), jnp.int32),) * n_acc)
        return jnp.sum(sum(parts), axis=0, keepdims=True)

    def count_rounded(cand_b):
        n_acc = 4

        def body(c, parts):
            hit = jnp.where(sb_ref[chunk_ds(c), :] >= cand_b, jnp.ones((kc, qb), BF16), jnp.zeros((kc, qb), BF16))
            parts = list(parts)
            for i in range(kc // BF16_ROWS):
                parts[i % n_acc] = parts[i % n_acc] + hit[i * BF16_ROWS:(i + 1) * BF16_ROWS, :]
            return tuple(parts)
        parts = lax.fori_loop(0, n_chunks, body, (jnp.zeros((BF16_ROWS, qb), BF16),) * n_acc)
        total = sum(p.astype(F32) for p in parts)
        return jnp.sum(total, axis=0, keepdims=True).astype(jnp.int32)

    def coarse_bit(i, key16):
        cand = key16 + (jnp.int32(1) << (15 - i))
        pattern = jnp.where(cand >= 0, cand, cand ^ 0x7FFF)
        cand_b = pltpu.bitcast(pattern << 16, F32).astype(BF16)
        return jnp.where(count_rounded(cand_b) >= topk, cand, key16)

    key16 = lax.fori_loop(0, 16, coarse_bit, jnp.full((1, qb), -(2 ** 15), jnp.int32))
    coarse = (key16 << 16) + jnp.where(key16 < 0, 0xFFFF, 0)
    window_lo = jnp.where(coarse < INT_MIN + 2 ** 16, INT_MIN, coarse - 2 ** 16)

    def thr_bit(i, state):
        thr, n_ge = state
        cand = thr + (jnp.int32(1) << (16 - i))
        cand_f = _score_of_key(cand)
        n = count(lambda s, p: s >= cand_f)
        return jnp.where(n >= topk, cand, thr), jnp.where(n >= topk, n, n_ge)

    def pending(state):
        return jnp.max(jnp.where(state[1] <= topk, 0, 1))

    state = (window_lo, jnp.full((1, qb), INT_MAX, jnp.int32))
    state = lax.fori_loop(0, THR_BITS_ALWAYS, thr_bit, state)

    def more_bits(carry):
        i, state, _ = carry
        for b in range(THR_BITS_STEP):
            state = thr_bit(i + b, state)
        return i + THR_BITS_STEP, state, pending(state)

    _, (thr, _), still_open = lax.while_loop(lambda carry: (carry[0] < 17) & (carry[2] > 0), more_bits,
                                             (jnp.int32(THR_BITS_ALWAYS), state, pending(state)))
    thr_f = jnp.where(thr >= LOWEST_FINITE_KEY, _score_of_key(jnp.maximum(thr, LOWEST_FINITE_KEY)), -jnp.inf)

    eye = (lax.broadcasted_iota(jnp.int32, (qb, qb), 0)
           == lax.broadcasted_iota(jnp.int32, (qb, qb), 1)).astype(BF16)

    def write_bias(selected):
        def bias_chunk(c, carry):
            p = key_pos(c)
            sel = (p <= q_pos) & selected(sc_ref[chunk_ds(c), :], p)
            bias_ref[:, chunk_ds(c)] = _dot_nt(eye, jnp.where(sel, 0.0, NEG_BIG).astype(BF16))
            return carry
        return bias_chunk

    def bias_no_ties():
        def before_block(c, carry):
            dropped = sc_ref[chunk_ds(c), :] < thr_f
            bias_ref[:, chunk_ds(c)] = _dot_nt(eye, jnp.where(dropped, NEG_BIG, 0.0).astype(BF16))
            return carry
        n_before = t0 // kc
        lax.fori_loop(0, n_before, before_block, 0)
        lax.fori_loop(n_before, n_chunks, write_bias(lambda s, p: s >= thr_f), 0)

    def bias_checking_ties():
        need = topk - count(lambda s, p: s > thr_f)
        n_tie = count(lambda s, p: (s == thr_f) & (p <= q_pos))

        def bias_with_ties():
            def cut_bit(i, cut):
                cand = cut + (jnp.int32(1) << (pos_bits - 1 - i))
                n_before = count(lambda s, p: (s == thr_f) & (p <= q_pos) & (p < cand))
                return jnp.where(n_before < need, cand, cut)
            cut = lax.fori_loop(0, pos_bits, cut_bit, jnp.zeros((1, qb), jnp.int32))
            lax.fori_loop(0, n_chunks, write_bias(lambda s, p: (s > thr_f) | ((s == thr_f) & (p <= cut))), 0)

        lax.cond(jnp.max(jnp.where(n_tie > need, 1, 0)) > 0, bias_with_ties, bias_no_ties)

    lax.cond(still_open > 0, bias_checking_ties, bias_no_ties)

    for h in range(ATT_HEADS):
        rows = slice(h * qb, (h + 1) * qb)
        ql = _dot(q_ref[:, h * ATT_HEAD_DIM:(h + 1) * ATT_HEAD_DIM], wuk_ref[h])
        ql_ref[rows, :] = (ql * (ATT_HEAD_DIM ** -0.5 * LOG2E)).astype(BF16)
        m_ref[rows, :] = jnp.full((qb, LANES), NEG_BIG, F32)
        l_ref[rows, :] = jnp.zeros((qb, LANES), F32)
        acc_ref[rows, :] = jnp.zeros((qb, KV_RANK), F32)
    lane_tiles = kc // LANES
    heads_per_part = ATT_HEADS // MXU_SPLIT

    part_rows = [slice(r * heads_per_part * qb, (r + 1) * heads_per_part * qb) for r in range(MXU_SPLIT)]

    def logits(part, c):
        rel = (c * kc - t0 + lax.broadcasted_iota(jnp.int32, (1, kc), 1)).astype(F32)
        s_part = _dot(ql_ref[part_rows[part], :], ckvt_ref[:, chunk_ds(c)])
        for i_h in range(heads_per_part):
            h = part * heads_per_part + i_h
            rows = slice(h * qb, (h + 1) * qb)
            s = s_part[i_h * qb:(i_h + 1) * qb, :] + (slopes_ref[h] * LOG2E) * rel + bias_ref[:, chunk_ds(c)]
            s_ref[rows, :] = s
            mx = s[:, 0:LANES]
            for i in range(1, lane_tiles):
                mx = jnp.maximum(mx, s[:, i * LANES:(i + 1) * LANES])
            m_old = m_ref[rows, :]
            m_new = jnp.maximum(m_old, jnp.max(mx, axis=-1, keepdims=True))
            m_ref[rows, :] = m_new
            alpha_ref[rows, :] = jnp.exp2(m_old - m_new)

    def values(part, c):
        cblk = ckv_ref[chunk_ds(c), :]
        for i_h in range(heads_per_part):
            rows = slice((part * heads_per_part + i_h) * qb, (part * heads_per_part + i_h + 1) * qb)
            m = m_ref[rows, :]
            l = alpha_ref[rows, :] * l_ref[rows, :]
            for i in range(lane_tiles):
                p = jnp.exp2(s_ref[rows, i * LANES:(i + 1) * LANES] - m)
                l = l + p
                p_ref[rows, i * LANES:(i + 1) * LANES] = p.astype(BF16)
            l_ref[rows, :] = l
        upd = _dot(p_ref[part_rows[part], :], cblk)
        for i_h in range(heads_per_part):
            rows = slice((part * heads_per_part + i_h) * qb, (part * heads_per_part + i_h + 1) * qb)
            alpha = alpha_ref[rows, :]
            for i in range(KV_RANK // LANES):
                cols = slice(i * LANES, (i + 1) * LANES)
                acc_ref[rows, cols] = alpha * acc_ref[rows, cols] + upd[i_h * qb:(i_h + 1) * qb, cols]

    last = n_chunks - 1
    for part in range(MXU_SPLIT):
        logits(part, 0)

    def att_chunk(c, carry):
        for part in range(MXU_SPLIT):
            values(part, c)
            logits(part, c + 1)
        return carry

    lax.fori_loop(0, last, att_chunk, 0)
    for part in range(MXU_SPLIT):
        values(part, last)

    o = (acc_ref[...] / jnp.sum(l_ref[...], axis=-1, keepdims=True)).astype(BF16)
    y_ref[...] = jnp.concatenate([_dot(o[h * qb:(h + 1) * qb, :], wuv_ref[h]) for h in range(ATT_HEADS)],
                                 axis=1).astype(BF16)


def _dsa(q, qit, wit, ckv, ckvt, ki, w_uk, w_uv, slopes, bsz, seqlen):
    t = q.shape[0]
    qb = Q_BLOCK
    nq = seqlen // qb
    rows = ATT_HEADS * qb
    topk = min(TOPK_MAX, seqlen // 4)
    pos_bits = max(1, (seqlen - 1).bit_length())
    seq_pad = -(-seqlen // KEY_CHUNK) * KEY_CHUNK
    row = lambda b, j: (b * nq + j, 0)
    col = lambda b, j: (0, b * nq + j)
    per_batch = lambda b, j: (b, 0, 0)
    if seq_pad != seqlen:
        ckv = jnp.pad(ckv.reshape(bsz, seqlen, KV_RANK), ((0, 0), (0, seq_pad - seqlen), (0, 0)))
        ki = jnp.pad(ki.reshape(bsz, seqlen, IDX_DIM), ((0, 0), (0, seq_pad - seqlen), (0, 0)))
    ckvt = ckvt.reshape(KV_RANK, bsz, seqlen)
    if seq_pad != seqlen:
        ckvt = jnp.pad(ckvt, ((0, 0), (0, 0), (0, seq_pad - seqlen)))
    ckvt = ckvt.reshape(KV_RANK, bsz * seq_pad)
    body = functools.partial(_dsa_body, topk=topk, pos_bits=pos_bits)
    return pl.pallas_call(
        body,
        grid=(bsz, nq),
        in_specs=[pl.BlockSpec(memory_space=pltpu.SMEM),
                  pl.BlockSpec((qb, ATT_WIDTH), row), pl.BlockSpec((IDX_HEADS * IDX_DIM, qb), col),
                  pl.BlockSpec((IDX_HEADS, qb), col),
                  pl.BlockSpec((None, seq_pad, KV_RANK), per_batch, pipeline_mode=pl.Buffered(1)),
                  pl.BlockSpec((KV_RANK, seq_pad), lambda b, j: (0, b), pipeline_mode=pl.Buffered(1)),
                  pl.BlockSpec((None, seq_pad, IDX_DIM), per_batch, pipeline_mode=pl.Buffered(1)),
                  _const_spec(w_uk.shape), _const_spec(w_uv.shape)],
        out_specs=pl.BlockSpec((qb, ATT_WIDTH), row),
        out_shape=jax.ShapeDtypeStruct((t, ATT_WIDTH), BF16),
        scratch_shapes=[pltpu.VMEM((seq_pad, qb), F32), pltpu.VMEM((seq_pad, qb), BF16),
                        pltpu.VMEM((qb, seq_pad), F32),
                        pltpu.VMEM((rows, KV_RANK), BF16),
                        pltpu.VMEM((rows, KEY_CHUNK), F32), pltpu.VMEM((rows, KEY_CHUNK), BF16),
                        pltpu.VMEM((rows, LANES), F32), pltpu.VMEM((rows, LANES), F32),
                        pltpu.VMEM((rows, LANES), F32), pltpu.VMEM((rows, KV_RANK), F32)],
        compiler_params=_params("arbitrary", "arbitrary"),
        name="dsa",
    )(slopes, q, qit, wit, ckv.reshape(bsz, seq_pad, KV_RANK), ckvt, ki.reshape(bsz, seq_pad, IDX_DIM), w_uk, w_uv)


def _merge_ln_body(x_ref, ys_ref, ya_ref, sg_ref, wps_ref, wpa_ref, wo_ref, g_ref, b_ref, o_ref):
    sub = x_ref.shape[0] // SUB_TILES
    for i in range(SUB_TILES):
        rows = slice(i * sub, (i + 1) * sub)
        sg = sg_ref[rows, :].astype(F32)
        merged = (sg[:, :D_MODEL] * _dot(ys_ref[rows, :], wps_ref[...])
                  + sg[:, D_MODEL:] * _dot(ya_ref[rows, :], wpa_ref[...]))
        out = _dot(merged.astype(BF16), wo_ref[...])
        o_ref[rows, :] = _layer_norm(ALPHA * x_ref[rows, :] + out, g_ref[...], b_ref[...])


def _merge_ln(x, ys, ya, sg, wps, wpa, wo, g, b, tm):
    t = x.shape[0]
    row = lambda i: (i, 0)
    return pl.pallas_call(
        _merge_ln_body,
        grid=(t // tm,),
        in_specs=[pl.BlockSpec((tm, D_MODEL), row), pl.BlockSpec((tm, SSD_INNER), row),
                  pl.BlockSpec((tm, ATT_WIDTH), row), pl.BlockSpec((tm, 2 * D_MODEL), row)]
                 + [_const_spec(w.shape) for w in (wps, wpa, wo, g, b)],
        out_specs=pl.BlockSpec((tm, D_MODEL), row),
        out_shape=jax.ShapeDtypeStruct((t, D_MODEL), F32),
        compiler_params=_params("parallel"),
        name="merge_ln",
    )(x, ys, ya, sg, wps, wpa, wo, g, b)


def _mem_kv_body(m_ref, w_ref, o_ref):
    o_ref[...] = _dot(m_ref[...].astype(BF16), w_ref[...]).astype(BF16)


def _mem_kv(mem, w_mkv, bsz, mem_len):
    row = lambda b: (b, 0)
    return pl.pallas_call(
        _mem_kv_body,
        grid=(bsz,),
        in_specs=[pl.BlockSpec((mem_len, D_MODEL), row), _const_spec(w_mkv.shape)],
        out_specs=pl.BlockSpec((mem_len, 2 * D_MODEL), row),
        out_shape=jax.ShapeDtypeStruct((bsz * mem_len, 2 * D_MODEL), BF16),
        compiler_params=_params("parallel"),
        name="mem_kv",
    )(mem, w_mkv)


def _xattn_ln_body(x_ref, kv_ref, wq_ref, wo_ref, g_ref, b_ref, o_ref):
    sub = x_ref.shape[0] // SUB_TILES
    for i in range(SUB_TILES):
        rows = slice(i * sub, (i + 1) * sub)
        x = x_ref[rows, :]
        q = (_dot(x.astype(BF16), wq_ref[...]) * (MEM_HEAD_DIM ** -0.5)).astype(BF16)
        heads = []
        for h in range(MEM_HEADS):
            lo = h * MEM_HEAD_DIM
            s = _dot_nt(q[:, lo:lo + MEM_HEAD_DIM], kv_ref[:, lo:lo + MEM_HEAD_DIM])
            p = jnp.exp(s - jnp.max(s, axis=-1, keepdims=True))
            o = _dot(p.astype(BF16), kv_ref[:, D_MODEL + lo:D_MODEL + lo + MEM_HEAD_DIM])
            heads.append((o / jnp.sum(p, axis=-1, keepdims=True)).astype(BF16))
        out = _dot(jnp.concatenate(heads, axis=1), wo_ref[...])
        o_ref[rows, :] = _layer_norm(ALPHA * x + out, g_ref[...], b_ref[...])


def _xattn_ln(x, kv, wq, wo, g, b, bsz, seqlen, mem_len, tm):
    t = x.shape[0]
    nt = seqlen // tm
    return pl.pallas_call(
        _xattn_ln_body,
        grid=(bsz, nt),
        in_specs=[pl.BlockSpec((tm, D_MODEL), lambda b_, i: (b_ * nt + i, 0)),
                  pl.BlockSpec((mem_len, 2 * D_MODEL), lambda b_, i: (b_, 0))]
                 + [_const_spec(w.shape) for w in (wq, wo, g, b)],
        out_specs=pl.BlockSpec((tm, D_MODEL), lambda b_, i: (b_ * nt + i, 0)),
        out_shape=jax.ShapeDtypeStruct((t, D_MODEL), F32),
        compiler_params=_params("parallel", "parallel"),
        name="xattn_ln",
    )(x, kv, wq, wo, g, b)


def _split_w_in(w_in):
    sizes = (D_MODEL, D_MODEL, SSD_INNER, SSD_XBC, SSD_HEADS, ATT_WIDTH, KV_RANK, IDX_HEADS * IDX_DIM,
             IDX_DIM, IDX_HEADS)
    parts, off = [], 0
    for s in sizes:
        parts.append(w_in[:, off:off + s])
        off += s
    g_ssd, g_att, z, xbc, dt, q, ckv, qi, ki, wi = parts
    misc = jnp.concatenate([dt, ki, jnp.zeros((w_in.shape[0], LANES - SSD_HEADS - IDX_DIM), w_in.dtype)], axis=1)
    return [w.astype(BF16) for w in (jnp.concatenate([g_ssd, g_att], axis=1), z, xbc, q, ckv, qi.T, misc, wi.T)]


def _layer(x, mem, bsz, seqlen, mem_len, p):
    (ffn1_w_in, ffn1_w_out, ln1_g, ln1_b, w_in, conv_w, conv_b, dt_bias, a_log, d_skip, ssd_norm_w,
     kv_norm_w, w_uk, w_uv, w_proj_ssd, w_proj_att, w_out, ln2_g, ln2_b,
     w_mq, w_mkv, w_mo, ln3_g, ln3_b, ffn2_w_in, ffn2_w_out, ln4_g, ln4_b) = p
    tm = min(TOKEN_TILE, seqlen)
    vec = lambda v: v.reshape(1, -1)
    bf = lambda w: w.astype(BF16)

    x = _ffn_ln(x, bf(ffn1_w_in), bf(ffn1_w_out), vec(ln1_g), vec(ln1_b), tm)

    sg, sz, xbc, q, ckv, ckvt, qit, ki, dt, wit = _in_proj(x, *_split_w_in(w_in), vec(kv_norm_w), tm)

    expand = jnp.repeat(jnp.eye(SSD_HEADS, dtype=BF16), SSD_HEAD_DIM, axis=1)
    y_ssd = _ssd(xbc, sz, dt, conv_w, vec(conv_b), vec(dt_bias), vec(a_log),
                 vec(jnp.repeat(d_skip, SSD_HEAD_DIM)), vec(ssd_norm_w), expand, bsz, seqlen)

    slopes = 2.0 ** (-8.0 * jnp.arange(1, ATT_HEADS + 1, dtype=F32) / ATT_HEADS)
    y_att = _dsa(q, qit, wit, ckv, ckvt, ki, bf(w_uk), bf(w_uv), slopes, bsz, seqlen)

    tm_sub = min(SUB_TILES * TOKEN_TILE, seqlen)
    x = _merge_ln(x, y_ssd, y_att, sg, bf(w_proj_ssd), bf(w_proj_att), bf(w_out), vec(ln2_g), vec(ln2_b), tm_sub)

    kv = _mem_kv(mem, bf(w_mkv), bsz, mem_len)
    x = _xattn_ln(x, kv, bf(w_mq), bf(w_mo), vec(ln3_g), vec(ln3_b), bsz, seqlen, mem_len, tm_sub)

    return _ffn_ln(x, bf(ffn2_w_in), bf(ffn2_w_out), vec(ln4_g), vec(ln4_b), tm)


def kernel(x, mem, ffn1_w_in, ffn1_w_out, ln1_g, ln1_b, w_in, conv_w, conv_b, dt_bias, a_log, d_skip,
           ssd_norm_w, kv_norm_w, w_uk, w_uv, w_proj_ssd, w_proj_att, w_out, ln2_g, ln2_b,
           w_mq, w_mkv, w_mo, ln3_g, ln3_b, ffn2_w_in, ffn2_w_out, ln4_g, ln4_b):
    bsz, seqlen, d = x.shape
    mem_len = mem.shape[1]
    params = (ffn1_w_in, ffn1_w_out, ln1_g, ln1_b, w_in, conv_w, conv_b, dt_bias, a_log, d_skip, ssd_norm_w,
              kv_norm_w, w_uk, w_uv, w_proj_ssd, w_proj_att, w_out, ln2_g, ln2_b,
              w_mq, w_mkv, w_mo, ln3_g, ln3_b, ffn2_w_in, ffn2_w_out, ln4_g, ln4_b)
    h = x.reshape(bsz * seqlen, d)
    m = mem.reshape(bsz * mem_len, d)
    for layer in range(ffn1_w_in.shape[0]):
        h = _layer(h, m, bsz, seqlen, mem_len, tuple(w[layer] for w in params))
    return h.reshape(bsz, seqlen, d)
```

```python
import functools
import math

import jax
import jax.numpy as jnp
from jax import lax
from jax.experimental import pallas as pl
from jax.experimental.pallas import tpu as pltpu

F32 = jnp.float32
BF16 = jnp.bfloat16

D_MODEL = 1024
DEPTH = 1
SSD_HEADS = 16
SSD_HEAD_DIM = 64
SSD_INNER = SSD_HEADS * SSD_HEAD_DIM
SSD_GROUPS = 4
SSD_STATE = 128
SSD_CONV = 4
SSD_CHUNK = 128
SSD_STEP_CHUNKS = 2
SSD_GN = SSD_GROUPS * SSD_STATE
SSD_XBC = SSD_INNER + 2 * SSD_GN
ATT_HEADS = 16
ATT_HEAD_DIM = 64
ATT_WIDTH = ATT_HEADS * ATT_HEAD_DIM
KV_RANK = 256
IDX_HEADS = 16
IDX_DIM = 64
TOPK_MAX = 256
Q_BLOCK = 256
MEM_HEADS = 4
MEM_HEAD_DIM = D_MODEL // MEM_HEADS
D_FF = 2816
ALPHA = (2.0 * DEPTH) ** 0.25
LN_EPS = 1e-5
RMS_EPS = 1e-6

INT_MIN = -(2 ** 31)
INT_MAX = 2 ** 31 - 1
LOWEST_FINITE_KEY = INT_MIN + 2 ** 23
NEG_BIG = -1e30
LOG2E = math.log2(math.e)

LANES = 128
SUBLANES = 8
BF16_ROWS = 16
TOKEN_TILE = 512
SUB_TILES = 2
FFN_CHUNK = 2816
KEY_CHUNK = 512
THR_BITS_ALWAYS = 9
THR_BITS_STEP = 2
MXU_SPLIT = 4
CONV_TAIL = 16
VMEM_LIMIT = 56 * 1024 * 1024

NT_DIMS = (((1,), (1,)), ((), ()))
TN_DIMS = (((0,), (0,)), ((), ()))


def _const_spec(shape):
    return pl.BlockSpec(shape, lambda *_: (0,) * len(shape), pipeline_mode=pl.Buffered(1))


def _params(*sem):
    return pltpu.CompilerParams(dimension_semantics=sem, vmem_limit_bytes=VMEM_LIMIT)


def _layer_norm(y, g, b):
    mu = jnp.mean(y, axis=-1, keepdims=True)
    d = y - mu
    var = jnp.mean(d * d, axis=-1, keepdims=True)
    return d * lax.rsqrt(var + LN_EPS) * g + b


def _dot(a, b):
    return jnp.dot(a, b, preferred_element_type=F32)


def _dot_nt(a, b):
    return lax.dot_general(a, b, NT_DIMS, preferred_element_type=F32)


def _split3(x):
    hi = x.astype(BF16)
    rest = x - hi.astype(F32)
    mid = rest.astype(BF16)
    return hi, mid, (rest - mid.astype(F32)).astype(BF16)


def _select_sum(sel01, pieces, lhs_is_01):
    if lhs_is_01:
        return sum(_dot(sel01, p) for p in pieces)
    return sum(_dot(p, sel01) for p in pieces)


def _ffn_ln_body(x_ref, win_ref, wout_ref, g_ref, b_ref, o_ref):
    x = x_ref[...]
    xb = x.astype(BF16)
    acc = jnp.zeros(x.shape, F32)
    for c in range(D_FF // FFN_CHUNK):
        lo = c * FFN_CHUNK
        gate = _dot(xb, win_ref[:, lo:lo + FFN_CHUNK])
        up = _dot(xb, win_ref[:, D_FF + lo:D_FF + lo + FFN_CHUNK])
        act = (gate * jax.nn.sigmoid(gate) * up).astype(BF16)
        acc = acc + _dot(act, wout_ref[lo:lo + FFN_CHUNK, :])
    o_ref[...] = _layer_norm(ALPHA * x + 0.5 * acc, g_ref[...], b_ref[...])


def _ffn_ln(x, w_in, w_out, g, b, tm):
    t = x.shape[0]
    row = lambda i: (i, 0)
    return pl.pallas_call(
        _ffn_ln_body,
        grid=(t // tm,),
        in_specs=[pl.BlockSpec((tm, D_MODEL), row),
                  _const_spec(w_in.shape), _const_spec(w_out.shape),
                  _const_spec(g.shape), _const_spec(b.shape)],
        out_specs=pl.BlockSpec((tm, D_MODEL), row),
        out_shape=jax.ShapeDtypeStruct((t, D_MODEL), F32),
        compiler_params=_params("parallel"),
        name="ffn_ln",
    )(x, w_in, w_out, g, b)


def _in_proj_body(x_ref, wg_ref, wz_ref, wxbc_ref, wq_ref, wckv_ref, wqi_ref, wmisc_ref, kvn_ref,
                  sg_ref, sz_ref, xbc_ref, q_ref, ckv_ref, ckvt_ref, qit_ref, ki_ref, dt_ref, wit_ref):
    xb = x_ref[...].astype(BF16)
    sg_ref[...] = jax.nn.sigmoid(_dot(xb, wg_ref[...])).astype(BF16)
    z = _dot(xb, wz_ref[...])
    sz_ref[...] = (z * jax.nn.sigmoid(z)).astype(BF16)
    xbc_ref[...] = _dot(xb, wxbc_ref[...]).astype(BF16)
    q_ref[...] = _dot(xb, wq_ref[...]).astype(BF16)
    c = _dot(xb, wckv_ref[...])
    c = c * lax.rsqrt(jnp.mean(c * c, axis=-1, keepdims=True) + RMS_EPS) * kvn_ref[...]
    ckv_ref[...] = c.astype(BF16)
    ckvt_ref[...] = c.T.astype(BF16)
    qit_ref[...] = _dot(xb, wqi_ref[...]).astype(BF16).T
    misc = _dot(xb, wmisc_ref[...])
    dt_ref[...] = misc[:, 0:SSD_HEADS]
    ki_ref[...] = misc[:, SSD_HEADS:SSD_HEADS + IDX_DIM].astype(BF16)
    wi_lo = SSD_HEADS + IDX_DIM
    wit_ref[...] = misc.T[wi_lo:wi_lo + IDX_HEADS, :] * (IDX_HEADS ** -0.5)


def _in_proj(x, wg, wz, wxbc, wq, wckv, wqi, wmisc, kvn, tm):
    t = x.shape[0]
    row = lambda i: (i, 0)
    col = lambda i: (0, i)
    out = [
        ((t, 2 * D_MODEL), BF16, (tm, 2 * D_MODEL), row),
        ((t, SSD_INNER), BF16, (tm, SSD_INNER), row),
        ((t, SSD_XBC), BF16, (tm, SSD_XBC), row),
        ((t, ATT_WIDTH), BF16, (tm, ATT_WIDTH), row),
        ((t, KV_RANK), BF16, (tm, KV_RANK), row),
        ((KV_RANK, t), BF16, (KV_RANK, tm), col),
        ((IDX_HEADS * IDX_DIM, t), BF16, (IDX_HEADS * IDX_DIM, tm), col),
        ((t, IDX_DIM), BF16, (tm, IDX_DIM), row),
        ((t, SSD_HEADS), F32, (tm, SSD_HEADS), row),
        ((IDX_HEADS, t), F32, (IDX_HEADS, tm), col),
    ]
    return pl.pallas_call(
        _in_proj_body,
        grid=(t // tm,),
        in_specs=[pl.BlockSpec((tm, D_MODEL), row)] + [_const_spec(w.shape) for w in
                                                        (wg, wz, wxbc, wq, wckv, wqi, wmisc, kvn)],
        out_specs=[pl.BlockSpec(blk, im) for _, _, blk, im in out],
        out_shape=[jax.ShapeDtypeStruct(shp, dt) for shp, dt, _, _ in out],
        compiler_params=_params("parallel"),
        name="in_proj",
    )(x, wg, wz, wxbc, wq, wckv, wqi, wmisc, kvn)


def _ssd_body(xbc_ref, sz_ref, dt_ref, cw_ref, cb_ref, dtb_ref, alog_ref, dskip_ref, nw_ref, expand_ref,
              y_ref, state_ref, ext_ref):
    chunk = SSD_CHUNK
    n_sub = xbc_ref.shape[0] // chunk
    hp = SSD_HEAD_DIM
    gw = SSD_INNER // SSD_GROUPS

    @pl.when(pl.program_id(1) == 0)
    def _():
        state_ref[...] = jnp.zeros(state_ref.shape, F32)
        ext_ref[0:CONV_TAIL, :] = jnp.zeros((CONV_TAIL, SSD_XBC), BF16)

    ext_ref[CONV_TAIL:, :] = xbc_ref[...]
    t_i = lax.broadcasted_iota(jnp.int32, (chunk, CONV_TAIL + chunk), 0)
    j_i = lax.broadcasted_iota(jnp.int32, (chunk, CONV_TAIL + chunk), 1)
    r_i = lax.broadcasted_iota(jnp.int32, (chunk, chunk), 0)
    c_i = lax.broadcasted_iota(jnp.int32, (chunk, chunk), 1)
    causal = r_i >= c_i
    causal_b = causal.astype(BF16)
    eye = (lax.broadcasted_iota(jnp.int32, (SSD_HEADS, SSD_HEADS), 0)
           == lax.broadcasted_iota(jnp.int32, (SSD_HEADS, SSD_HEADS), 1)).astype(BF16)
    expand = expand_ref[...]
    neg_a = -jnp.exp(alog_ref[...])
    state = state_ref[...]

    for sub in range(n_sub):
        rows = slice(sub * chunk, (sub + 1) * chunk)
        ext = ext_ref[sub * chunk:sub * chunk + CONV_TAIL + chunk, :]
        conv = cb_ref[...] + cw_ref[SSD_CONV - 1:SSD_CONV, :] * xbc_ref[rows, :].astype(F32)
        for k in range(SSD_CONV - 1):
            shift = (j_i == t_i + (CONV_TAIL - (SSD_CONV - 1) + k)).astype(BF16)
            conv = conv + cw_ref[k:k + 1, :] * _dot(shift, ext)
        act = conv * jax.nn.sigmoid(conv)
        xs = act[:, :SSD_INNER]

        pre = dt_ref[rows, :] + dtb_ref[...]
        dt = jnp.maximum(pre, 0.0) + jnp.log(1.0 + jnp.exp(-jnp.abs(pre)))
        a = dt * neg_a
        a_cs = _select_sum(causal_b, _split3(a), lhs_is_01=True)
        a_cs_3 = _split3(a_cs)
        a_cs_t = sum(_dot_nt(eye, piece) for piece in a_cs_3)
        dt_e = _select_sum(expand, _split3(dt), lhs_is_01=False)
        a_cs_e = _select_sum(expand, a_cs_3, lhs_is_01=False)
        a_end_e = a_cs_e[chunk - 1:chunk, :]

        xdt = xs * dt_e
        xdt_b = xdt.astype(BF16)
        xdt_end_b = (xdt * jnp.exp(a_end_e - a_cs_e)).astype(BF16)
        state_b = state.astype(BF16)

        y_diag, y_off, new_state = [], [], []
        for g in range(SSD_GROUPS):
            bm = act[:, SSD_INNER + g * SSD_STATE:SSD_INNER + (g + 1) * SSD_STATE].astype(BF16)
            cm = act[:, SSD_INNER + SSD_GN + g * SSD_STATE:SSD_INNER + SSD_GN + (g + 1) * SSD_STATE].astype(BF16)
            cb = _dot_nt(cm, bm)
            y_off.append(_dot(cm, state_b[:, g * gw:(g + 1) * gw]))
            new_state.append(lax.dot_general(bm, xdt_end_b[:, g * gw:(g + 1) * gw], TN_DIMS,
                                             preferred_element_type=F32))
            for r in range(SSD_HEADS // SSD_GROUPS):
                h = g * (SSD_HEADS // SSD_GROUPS) + r
                seg = a_cs[:, h:h + 1] - a_cs_t[h:h + 1, :]
                decay = jnp.exp(jnp.where(causal, seg, NEG_BIG))
                y_diag.append(_dot((cb * decay).astype(BF16), xdt_b[:, h * hp:(h + 1) * hp]))
        state = jnp.exp(a_end_e) * state + jnp.concatenate(new_state, axis=1)

        y = jnp.concatenate(y_diag, axis=1) + jnp.concatenate(y_off, axis=1) * jnp.exp(a_cs_e)
        y = (y + dskip_ref[...] * xs) * sz_ref[rows, :].astype(F32)
        outs = []
        for g in range(SSD_GROUPS):
            yg = y[:, g * gw:(g + 1) * gw]
            outs.append(yg * lax.rsqrt(jnp.mean(yg * yg, axis=-1, keepdims=True) + RMS_EPS))
        y_ref[rows, :] = (jnp.concatenate(outs, axis=1) * nw_ref[...]).astype(BF16)

    state_ref[...] = state
    ext_ref[0:CONV_TAIL, :] = ext_ref[n_sub * chunk:n_sub * chunk + CONV_TAIL, :]


def _ssd(xbc, sz, dt, conv_w, conv_b, dt_bias, a_log, d_skip_e, norm_w, expand, bsz, seqlen):
    t = xbc.shape[0]
    n_sub = SSD_STEP_CHUNKS if (seqlen // SSD_CHUNK) % SSD_STEP_CHUNKS == 0 else 1
    rows = n_sub * SSD_CHUNK
    ns = seqlen // rows
    row = lambda b, c: (b * ns + c, 0)
    return pl.pallas_call(
        _ssd_body,
        grid=(bsz, ns),
        in_specs=[pl.BlockSpec((rows, SSD_XBC), row), pl.BlockSpec((rows, SSD_INNER), row),
                  pl.BlockSpec((rows, SSD_HEADS), row)]
                 + [_const_spec(w.shape) for w in (conv_w, conv_b, dt_bias, a_log, d_skip_e, norm_w, expand)],
        out_specs=pl.BlockSpec((rows, SSD_INNER), row),
        out_shape=jax.ShapeDtypeStruct((t, SSD_INNER), BF16),
        scratch_shapes=[pltpu.VMEM((SSD_STATE, SSD_INNER), F32),
                        pltpu.VMEM((CONV_TAIL + rows, SSD_XBC), BF16)],
        compiler_params=_params("arbitrary", "arbitrary"),
        name="ssd",
    )(xbc, sz, dt, conv_w, conv_b, dt_bias, a_log, d_skip_e, norm_w, expand)


def _score_of_key(key):
    return pltpu.bitcast(jnp.where(key >= 0, key, key ^ INT_MAX), F32)


def _dsa_body(slopes_ref, q_ref, qit_ref, wit_ref, ckv_ref, ckvt_ref, ki_ref, wuk_ref, wuv_ref, y_ref,
              sc_ref, sb_ref, bias_ref, ql_ref, s_ref, p_ref, m_ref, l_ref, alpha_ref, acc_ref, *, topk, pos_bits):
    qb = q_ref.shape[0]
    kc = KEY_CHUNK
    t0 = pl.program_id(1) * qb
    n_chunks = (t0 + qb + kc - 1) // kc
    q_pos = t0 + lax.broadcasted_iota(jnp.int32, (1, qb), 1)

    def key_pos(c):
        return c * kc + lax.broadcasted_iota(jnp.int32, (kc, qb), 0)

    def chunk_ds(c):
        return pl.ds(pl.multiple_of(c * kc, kc), kc)

    def score_chunk(c, carry):
        kblk = ki_ref[chunk_ds(c), :]
        score = jnp.zeros((kc, qb), F32)
        for h in range(0, IDX_HEADS, 2):
            w = jnp.concatenate([qit_ref[h * IDX_DIM:(h + 1) * IDX_DIM, :],
                                 qit_ref[(h + 1) * IDX_DIM:(h + 2) * IDX_DIM, :]], axis=1)
            wi = jnp.concatenate([wit_ref[h:h + 1, :], wit_ref[h + 1:h + 2, :]], axis=1)
            t = jnp.maximum(_dot(kblk, w), 0.0) * wi
            score = score + t[:, :qb] + t[:, qb:]
        score = jnp.where(key_pos(c) <= q_pos, score, -jnp.inf)
        sc_ref[chunk_ds(c), :] = score
        sb_ref[chunk_ds(c), :] = score.astype(BF16)
        return carry

    lax.fori_loop(0, n_chunks, score_chunk, 0)

    def count(pred):
        n_acc = 8

        def body(c, parts):
            hit = pred(sc_ref[chunk_ds(c), :], key_pos(c))
            parts = list(parts)
            for i in range(kc // SUBLANES):
                tile = hit[i * SUBLANES:(i + 1) * SUBLANES, :]
                parts[i % n_acc] = jnp.where(tile, parts[i % n_acc] + 1, parts[i % n_acc])
            return tuple(parts)
        parts = lax.fori_loop(0, n_chunks, body, (jnp.zeros((SUBLANES, qb), jnp.int32),) * n_acc)
        return jnp.sum(sum(parts), axis=0, keepdims=True)

    def count_rounded(cand_b):
        n_acc = 4

        def body(c, parts):
            hit = jnp.where(sb_ref[chunk_ds(c), :] >= cand_b, jnp.ones((kc, qb), BF16), jnp.zeros((kc, qb), BF16))
            parts = list(parts)
            for i in range(kc // BF16_ROWS):
                parts[i % n_acc] = parts[i % n_acc] + hit[i * BF16_ROWS:(i + 1) * BF16_ROWS, :]
            return tuple(parts)
        parts = lax.fori_loop(0, n_chunks, body, (jnp.zeros((BF16_ROWS, qb), BF16),) * n_acc)
        total = sum(p.astype(F32) for p in parts)
        return jnp.sum(total, axis=0, keepdims=True).astype(jnp.int32)

    def coarse_bit(i, key16):
        cand = key16 + (jnp.int32(1) << (15 - i))
        pattern = jnp.where(cand >= 0, cand, cand ^ 0x7FFF)
        cand_b = pltpu.bitcast(pattern << 16, F32).astype(BF16)
        return jnp.where(count_rounded(cand_b) >= topk, cand, key16)

    key16 = lax.fori_loop(0, 16, coarse_bit, jnp.full((1, qb), -(2 ** 15), jnp.int32))
    coarse = (key16 << 16) + jnp.where(key16 < 0, 0xFFFF, 0)
    window_lo = jnp.where(coarse < INT_MIN + 2 ** 16, INT_MIN, coarse - 2 ** 16)

    def thr_bit(i, state):
        thr, n_ge = state
        cand = thr + (jnp.int32(1) << (16 - i))
        cand_f = _score_of_key(cand)
        n = count(lambda s, p: s >= cand_f)
        return jnp.where(n >= topk, cand, thr), jnp.where(n >= topk, n, n_ge)

    def pending(state):
        return jnp.max(jnp.where(state[1] <= topk, 0, 1))

    state = (window_lo, jnp.full((1, qb), INT_MAX, jnp.int32))
    state = lax.fori_loop(0, THR_BITS_ALWAYS, thr_bit, state)

    def more_bits(carry):
        i, state, _ = carry
        for b in range(THR_BITS_STEP):
            state = thr_bit(i + b, state)
        return i + THR_BITS_STEP, state, pending(state)

    _, (thr, _), still_open = lax.while_loop(lambda carry: (carry[0] < 17) & (carry[2] > 0), more_bits,
                                             (jnp.int32(THR_BITS_ALWAYS), state, pending(state)))
    thr_f = jnp.where(thr >= LOWEST_FINITE_KEY, _score_of_key(jnp.maximum(thr, LOWEST_FINITE_KEY)), -jnp.inf)

    eye = (lax.broadcasted_iota(jnp.int32, (qb, qb), 0)
           == lax.broadcasted_iota(jnp.int32, (qb, qb), 1)).astype(BF16)

    def write_bias(selected):
        def bias_chunk(c, carry):
            p = key_pos(c)
            sel = (p <= q_pos) & selected(sc_ref[chunk_ds(c), :], p)
            bias_ref[:, chunk_ds(c)] = _dot_nt(eye, jnp.where(sel, 0.0, NEG_BIG).astype(BF16))
            return carry
        return bias_chunk

    def bias_no_ties():
        def before_block(c, carry):
            dropped = sc_ref[chunk_ds(c), :] < thr_f
            bias_ref[:, chunk_ds(c)] = _dot_nt(eye, jnp.where(dropped, NEG_BIG, 0.0).astype(BF16))
            return carry
        n_before = t0 // kc
        lax.fori_loop(0, n_before, before_block, 0)
        lax.fori_loop(n_before, n_chunks, write_bias(lambda s, p: s >= thr_f), 0)

    def bias_checking_ties():
        need = topk - count(lambda s, p: s > thr_f)
        n_tie = count(lambda s, p: (s == thr_f) & (p <= q_pos))

        def bias_with_ties():
            def cut_bit(i, cut):
                cand = cut + (jnp.int32(1) << (pos_bits - 1 - i))
                n_before = count(lambda s, p: (s == thr_f) & (p <= q_pos) & (p < cand))
                return jnp.where(n_before < need, cand, cut)
            cut = lax.fori_loop(0, pos_bits, cut_bit, jnp.zeros((1, qb), jnp.int32))
            lax.fori_loop(0, n_chunks, write_bias(lambda s, p: (s > thr_f) | ((s == thr_f) & (p <= cut))), 0)

        lax.cond(jnp.max(jnp.where(n_tie > need, 1, 0)) > 0, bias_with_ties, bias_no_ties)

    lax.cond(still_open > 0, bias_checking_ties, bias_no_ties)

    for h in range(ATT_HEADS):
        rows = slice(h * qb, (h + 1) * qb)
        ql = _dot(q_ref[:, h * ATT_HEAD_DIM:(h + 1) * ATT_HEAD_DIM], wuk_ref[h])
        ql_ref[rows, :] = (ql * (ATT_HEAD_DIM ** -0.5 * LOG2E)).astype(BF16)
        m_ref[rows, :] = jnp.full((qb, LANES), NEG_BIG, F32)
        l_ref[rows, :] = jnp.zeros((qb, LANES), F32)
        acc_ref[rows, :] = jnp.zeros((qb, KV_RANK), F32)
    lane_tiles = kc // LANES
    heads_per_part = ATT_HEADS // MXU_SPLIT

    part_rows = [slice(r * heads_per_part * qb, (r + 1) * heads_per_part * qb) for r in range(MXU_SPLIT)]

    def logits(part, c):
        rel = (c * kc - t0 + lax.broadcasted_iota(jnp.int32, (1, kc), 1)).astype(F32)
        s_part = _dot(ql_ref[part_rows[part], :], ckvt_ref[:, chunk_ds(c)])
        for i_h in range(heads_per_part):
            h = part * heads_per_part + i_h
            rows = slice(h * qb, (h + 1) * qb)
            s = s_part[i_h * qb:(i_h + 1) * qb, :] + (slopes_ref[h] * LOG2E) * rel + bias_ref[:, chunk_ds(c)]
            s_ref[rows, :] = s
            mx = s[:, 0:LANES]
            for i in range(1, lane_tiles):
                mx = jnp.maximum(mx, s[:, i * LANES:(i + 1) * LANES])
            m_old = m_ref[rows, :]
            m_new = jnp.maximum(m_old, jnp.max(mx, axis=-1, keepdims=True))
            m_ref[rows, :] = m_new
            alpha_ref[rows, :] = jnp.exp2(m_old - m_new)

    def values(part, c):
        cblk = ckv_ref[chunk_ds(c), :]
        for i_h in range(heads_per_part):
            rows = slice((part * heads_per_part + i_h) * qb, (part * heads_per_part + i_h + 1) * qb)
            m = m_ref[rows, :]
            l = alpha_ref[rows, :] * l_ref[rows, :]
            for i in range(lane_tiles):
                p = jnp.exp2(s_ref[rows, i * LANES:(i + 1) * LANES] - m)
                l = l + p
                p_ref[rows, i * LANES:(i + 1) * LANES] = p.astype(BF16)
            l_ref[rows, :] = l
        upd = _dot(p_ref[part_rows[part], :], cblk)
        for i_h in range(heads_per_part):
            rows = slice((part * heads_per_part + i_h) * qb, (part * heads_per_part + i_h + 1) * qb)
            alpha = alpha_ref[rows, :]
            for i in range(KV_RANK // LANES):
                cols = slice(i * LANES, (i + 1) * LANES)
                acc_ref[rows, cols] = alpha * acc_ref[rows, cols] + upd[i_h * qb:(i_h + 1) * qb, cols]

    last = n_chunks - 1
    for part in range(MXU_SPLIT):
        logits(part, 0)

    def att_chunk(c, carry):
        for part in range(MXU_SPLIT):
            values(part, c)
            logits(part, c + 1)
        return carry

    lax.fori_loop(0, last, att_chunk, 0)
    for part in range(MXU_SPLIT):
        values(part, last)

    o = (acc_ref[...] / jnp.sum(l_ref[...], axis=-1, keepdims=True)).astype(BF16)
    y_ref[...] = jnp.concatenate([_dot(o[h * qb:(h + 1) * qb, :], wuv_ref[h]) for h in range(ATT_HEADS)],
                                 axis=1).astype(BF16)


def _dsa(q, qit, wit, ckv, ckvt, ki, w_uk, w_uv, slopes, bsz, seqlen):
    t = q.shape[0]
    qb = Q_BLOCK
    nq = seqlen // qb
    rows = ATT_HEADS * qb
    topk = min(TOPK_MAX, seqlen // 4)
    pos_bits = max(1, (seqlen - 1).bit_length())
    seq_pad = -(-seqlen // KEY_CHUNK) * KEY_CHUNK
    row = lambda b, j: (b * nq + j, 0)
    col = lambda b, j: (0, b * nq + j)
    per_batch = lambda b, j: (b, 0, 0)
    if seq_pad != seqlen:
        ckv = jnp.pad(ckv.reshape(bsz, seqlen, KV_RANK), ((0, 0), (0, seq_pad - seqlen), (0, 0)))
        ki = jnp.pad(ki.reshape(bsz, seqlen, IDX_DIM), ((0, 0), (0, seq_pad - seqlen), (0, 0)))
    ckvt = ckvt.reshape(KV_RANK, bsz, seqlen)
    if seq_pad != seqlen:
        ckvt = jnp.pad(ckvt, ((0, 0), (0, 0), (0, seq_pad - seqlen)))
    ckvt = ckvt.reshape(KV_RANK, bsz * seq_pad)
    body = functools.partial(_dsa_body, topk=topk, pos_bits=pos_bits)
    return pl.pallas_call(
        body,
        grid=(bsz, nq),
        in_specs=[pl.BlockSpec(memory_space=pltpu.SMEM),
                  pl.BlockSpec((qb, ATT_WIDTH), row), pl.BlockSpec((IDX_HEADS * IDX_DIM, qb), col),
                  pl.BlockSpec((IDX_HEADS, qb), col),
                  pl.BlockSpec((None, seq_pad, KV_RANK), per_batch, pipeline_mode=pl.Buffered(1)),
                  pl.BlockSpec((KV_RANK, seq_pad), lambda b, j: (0, b), pipeline_mode=pl.Buffered(1)),
                  pl.BlockSpec((None, seq_pad, IDX_DIM), per_batch, pipeline_mode=pl.Buffered(1)),
                  _const_spec(w_uk.shape), _const_spec(w_uv.shape)],
        out_specs=pl.BlockSpec((qb, ATT_WIDTH), row),
        out_shape=jax.ShapeDtypeStruct((t, ATT_WIDTH), BF16),
        scratch_shapes=[pltpu.VMEM((seq_pad, qb), F32), pltpu.VMEM((seq_pad, qb), BF16),
                        pltpu.VMEM((qb, seq_pad), F32),
                        pltpu.VMEM((rows, KV_RANK), BF16),
                        pltpu.VMEM((rows, KEY_CHUNK), F32), pltpu.VMEM((rows, KEY_CHUNK), BF16),
                        pltpu.VMEM((rows, LANES), F32), pltpu.VMEM((rows, LANES), F32),
                        pltpu.VMEM((rows, LANES), F32), pltpu.VMEM((rows, KV_RANK), F32)],
        compiler_params=_params("arbitrary", "arbitrary"),
        name="dsa",
    )(slopes, q, qit, wit, ckv.reshape(bsz, seq_pad, KV_RANK), ckvt, ki.reshape(bsz, seq_pad, IDX_DIM), w_uk, w_uv)


def _merge_ln_body(x_ref, ys_ref, ya_ref, sg_ref, wps_ref, wpa_ref, wo_ref, g_ref, b_ref, o_ref):
    sub = x_ref.shape[0] // SUB_TILES
    for i in range(SUB_TILES):
        rows = slice(i * sub, (i + 1) * sub)
        sg = sg_ref[rows, :].astype(F32)
        merged = (sg[:, :D_MODEL] * _dot(ys_ref[rows, :], wps_ref[...])
                  + sg[:, D_MODEL:] * _dot(ya_ref[rows, :], wpa_ref[...]))
        out = _dot(merged.astype(BF16), wo_ref[...])
        o_ref[rows, :] = _layer_norm(ALPHA * x_ref[rows, :] + out, g_ref[...], b_ref[...])


def _merge_ln(x, ys, ya, sg, wps, wpa, wo, g, b, tm):
    t = x.shape[0]
    row = lambda i: (i, 0)
    return pl.pallas_call(
        _merge_ln_body,
        grid=(t // tm,),
        in_specs=[pl.BlockSpec((tm, D_MODEL), row), pl.BlockSpec((tm, SSD_INNER), row),
                  pl.BlockSpec((tm, ATT_WIDTH), row), pl.BlockSpec((tm, 2 * D_MODEL), row)]
                 + [_const_spec(w.shape) for w in (wps, wpa, wo, g, b)],
        out_specs=pl.BlockSpec((tm, D_MODEL), row),
        out_shape=jax.ShapeDtypeStruct((t, D_MODEL), F32),
        compiler_params=_params("parallel"),
        name="merge_ln",
    )(x, ys, ya, sg, wps, wpa, wo, g, b)


def _mem_kv_body(m_ref, w_ref, o_ref):
    o_ref[...] = _dot(m_ref[...].astype(BF16), w_ref[...]).astype(BF16)


def _mem_kv(mem, w_mkv, bsz, mem_len):
    row = lambda b: (b, 0)
    return pl.pallas_call(
        _mem_kv_body,
        grid=(bsz,),
        in_specs=[pl.BlockSpec((mem_len, D_MODEL), row), _const_spec(w_mkv.shape)],
        out_specs=pl.BlockSpec((mem_len, 2 * D_MODEL), row),
        out_shape=jax.ShapeDtypeStruct((bsz * mem_len, 2 * D_MODEL), BF16),
        compiler_params=_params("parallel"),
        name="mem_kv",
    )(mem, w_mkv)


def _xattn_ln_body(x_ref, kv_ref, wq_ref, wo_ref, g_ref, b_ref, o_ref):
    sub = x_ref.shape[0] // SUB_TILES
    for i in range(SUB_TILES):
        rows = slice(i * sub, (i + 1) * sub)
        x = x_ref[rows, :]
        q = (_dot(x.astype(BF16), wq_ref[...]) * (MEM_HEAD_DIM ** -0.5)).astype(BF16)
        heads = []
        for h in range(MEM_HEADS):
            lo = h * MEM_HEAD_DIM
            s = _dot_nt(q[:, lo:lo + MEM_HEAD_DIM], kv_ref[:, lo:lo + MEM_HEAD_DIM])
            p = jnp.exp(s - jnp.max(s, axis=-1, keepdims=True))
            o = _dot(p.astype(BF16), kv_ref[:, D_MODEL + lo:D_MODEL + lo + MEM_HEAD_DIM])
            heads.append((o / jnp.sum(p, axis=-1, keepdims=True)).astype(BF16))
        out = _dot(jnp.concatenate(heads, axis=1), wo_ref[...])
        o_ref[rows, :] = _layer_norm(ALPHA * x + out, g_ref[...], b_ref[...])


def _xattn_ln(x, kv, wq, wo, g, b, bsz, seqlen, mem_len, tm):
    t = x.shape[0]
    nt = seqlen // tm
    return pl.pallas_call(
        _xattn_ln_body,
        grid=(bsz, nt),
        in_specs=[pl.BlockSpec((tm, D_MODEL), lambda b_, i: (b_ * nt + i, 0)),
                  pl.BlockSpec((mem_len, 2 * D_MODEL), lambda b_, i: (b_, 0))]
                 + [_const_spec(w.shape) for w in (wq, wo, g, b)],
        out_specs=pl.BlockSpec((tm, D_MODEL), lambda b_, i: (b_ * nt + i, 0)),
        out_shape=jax.ShapeDtypeStruct((t, D_MODEL), F32),
        compiler_params=_params("parallel", "parallel"),
        name="xattn_ln",
    )(x, kv, wq, wo, g, b)


def _split_w_in(w_in):
    sizes = (D_MODEL, D_MODEL, SSD_INNER, SSD_XBC, SSD_HEADS, ATT_WIDTH, KV_RANK, IDX_HEADS * IDX_DIM,
             IDX_DIM, IDX_HEADS)
    parts, off = [], 0
    for s in sizes:
        parts.append(w_in[:, off:off + s])
        off += s
    g_ssd, g_att, z, xbc, dt, q, ckv, qi, ki, wi = parts
    pad = jnp.zeros((w_in.shape[0], LANES - SSD_HEADS - IDX_DIM - IDX_HEADS), w_in.dtype)
    misc = jnp.concatenate([dt, ki, wi, pad], axis=1)
    return [w.astype(BF16) for w in (jnp.concatenate([g_ssd, g_att], axis=1), z, xbc, q, ckv, qi, misc)]


def _layer(x, mem, bsz, seqlen, mem_len, p):
    (ffn1_w_in, ffn1_w_out, ln1_g, ln1_b, w_in, conv_w, conv_b, dt_bias, a_log, d_skip, ssd_norm_w,
     kv_norm_w, w_uk, w_uv, w_proj_ssd, w_proj_att, w_out, ln2_g, ln2_b,
     w_mq, w_mkv, w_mo, ln3_g, ln3_b, ffn2_w_in, ffn2_w_out, ln4_g, ln4_b) = p
    tm = min(TOKEN_TILE, seqlen)
    vec = lambda v: v.reshape(1, -1)
    bf = lambda w: w.astype(BF16)

    x = _ffn_ln(x, bf(ffn1_w_in), bf(ffn1_w_out), vec(ln1_g), vec(ln1_b), tm)

    sg, sz, xbc, q, ckv, ckvt, qit, ki, dt, wit = _in_proj(x, *_split_w_in(w_in), vec(kv_norm_w), tm)

    expand = jnp.repeat(jnp.eye(SSD_HEADS, dtype=BF16), SSD_HEAD_DIM, axis=1)
    y_ssd = _ssd(xbc, sz, dt, conv_w, vec(conv_b), vec(dt_bias), vec(a_log),
                 vec(jnp.repeat(d_skip, SSD_HEAD_DIM)), vec(ssd_norm_w), expand, bsz, seqlen)

    slopes = 2.0 ** (-8.0 * jnp.arange(1, ATT_HEADS + 1, dtype=F32) / ATT_HEADS)
    y_att = _dsa(q, qit, wit, ckv, ckvt, ki, bf(w_uk), bf(w_uv), slopes, bsz, seqlen)

    tm_sub = min(SUB_TILES * TOKEN_TILE, seqlen)
    x = _merge_ln(x, y_ssd, y_att, sg, bf(w_proj_ssd), bf(w_proj_att), bf(w_out), vec(ln2_g), vec(ln2_b), tm_sub)

    kv = _mem_kv(mem, bf(w_mkv), bsz, mem_len)
    x = _xattn_ln(x, kv, bf(w_mq), bf(w_mo), vec(ln3_g), vec(ln3_b), bsz, seqlen, mem_len, tm_sub)

    return _ffn_ln(x, bf(ffn2_w_in), bf(ffn2_w_out), vec(ln4_g), vec(ln4_b), tm)


def kernel(x, mem, ffn1_w_in, ffn1_w_out, ln1_g, ln1_b, w_in, conv_w, conv_b, dt_bias, a_log, d_skip,
           ssd_norm_w, kv_norm_w, w_uk, w_uv, w_proj_ssd, w_proj_att, w_out, ln2_g, ln2_b,
           w_mq, w_mkv, w_mo, ln3_g, ln3_b, ffn2_w_in, ffn2_w_out, ln4_g, ln4_b):
    bsz, seqlen, d = x.shape
    mem_len = mem.shape[1]
    params = (ffn1_w_in, ffn1_w_out, ln1_g, ln1_b, w_in, conv_w, conv_b, dt_bias, a_log, d_skip, ssd_norm_w,
              kv_norm_w, w_uk, w_uv, w_proj_ssd, w_proj_att, w_out, ln2_g, ln2_b,
              w_mq, w_mkv, w_mo, ln3_g, ln3_b, ffn2_w_in, ffn2_w_out, ln4_g, ln4_b)
    h = x.reshape(bsz * seqlen, d)
    m = mem.reshape(bsz * mem_len, d)
    for layer in range(ffn1_w_in.shape[0]):
        h = _layer(h, m, bsz, seqlen, mem_len, tuple(w[layer] for w in params))
    return h.reshape(bsz, seqlen, d)
```

```python
import functools
import math

import jax
import jax.numpy as jnp
from jax import lax
from jax.experimental import pallas as pl
from jax.experimental.pallas import tpu as pltpu

F32 = jnp.float32
BF16 = jnp.bfloat16

D_MODEL = 1024
DEPTH = 1
SSD_HEADS = 16
SSD_HEAD_DIM = 64
SSD_INNER = SSD_HEADS * SSD_HEAD_DIM
SSD_GROUPS = 4
SSD_STATE = 128
SSD_CONV = 4
SSD_CHUNK = 128
SSD_STEP_CHUNKS = 2
SSD_GN = SSD_GROUPS * SSD_STATE
SSD_XBC = SSD_INNER + 2 * SSD_GN
ATT_HEADS = 16
ATT_HEAD_DIM = 64
ATT_WIDTH = ATT_HEADS * ATT_HEAD_DIM
KV_RANK = 256
IDX_HEADS = 16
IDX_DIM = 64
TOPK_MAX = 256
Q_BLOCK = 256
MEM_HEADS = 4
MEM_HEAD_DIM = D_MODEL // MEM_HEADS
D_FF = 2816
ALPHA = (2.0 * DEPTH) ** 0.25
LN_EPS = 1e-5
RMS_EPS = 1e-6

INT_MIN = -(2 ** 31)
INT_MAX = 2 ** 31 - 1
LOWEST_FINITE_KEY = INT_MIN + 2 ** 23
NEG_BIG = -1e30
LOG2E = math.log2(math.e)

LANES = 128
SUBLANES = 8
BF16_ROWS = 16
TOKEN_TILE = 512
SUB_TILES = 2
FFN_CHUNK = 2816
KEY_CHUNK = 512
THR_BITS_ALWAYS = 9
THR_BITS_STEP = 2
MXU_SPLIT = 4
CONV_TAIL = 16
VMEM_LIMIT = 56 * 1024 * 1024

NT_DIMS = (((1,), (1,)), ((), ()))
TN_DIMS = (((0,), (0,)), ((), ()))


def _const_spec(shape):
    return pl.BlockSpec(shape, lambda *_: (0,) * len(shape), pipeline_mode=pl.Buffered(1))


def _params(*sem):
    return pltpu.CompilerParams(dimension_semantics=sem, vmem_limit_bytes=VMEM_LIMIT)


def _layer_norm(y, g, b):
    mu = jnp.mean(y, axis=-1, keepdims=True)
    d = y - mu
    var = jnp.mean(d * d, axis=-1, keepdims=True)
    return d * lax.rsqrt(var + LN_EPS) * g + b


def _dot(a, b):
    return jnp.dot(a, b, preferred_element_type=F32)


def _dot_nt(a, b):
    return lax.dot_general(a, b, NT_DIMS, preferred_element_type=F32)


def _split3(x):
    hi = x.astype(BF16)
    rest = x - hi.astype(F32)
    mid = rest.astype(BF16)
    return hi, mid, (rest - mid.astype(F32)).astype(BF16)


def _select_sum(sel01, pieces, lhs_is_01):
    if lhs_is_01:
        return sum(_dot(sel01, p) for p in pieces)
    return sum(_dot(p, sel01) for p in pieces)


def _ffn_ln_body(x_ref, win_ref, wout_ref, g_ref, b_ref, o_ref):
    x = x_ref[...]
    xb = x.astype(BF16)
    acc = jnp.zeros(x.shape, F32)
    for c in range(D_FF // FFN_CHUNK):
        lo = c * FFN_CHUNK
        gate = _dot(xb, win_ref[:, lo:lo + FFN_CHUNK])
        up = _dot(xb, win_ref[:, D_FF + lo:D_FF + lo + FFN_CHUNK])
        act = (gate * jax.nn.sigmoid(gate) * up).astype(BF16)
        acc = acc + _dot(act, wout_ref[lo:lo + FFN_CHUNK, :])
    o_ref[...] = _layer_norm(ALPHA * x + 0.5 * acc, g_ref[...], b_ref[...])


def _ffn_ln(x, w_in, w_out, g, b, tm):
    t = x.shape[0]
    row = lambda i: (i, 0)
    return pl.pallas_call(
        _ffn_ln_body,
        grid=(t // tm,),
        in_specs=[pl.BlockSpec((tm, D_MODEL), row),
                  _const_spec(w_in.shape), _const_spec(w_out.shape),
                  _const_spec(g.shape), _const_spec(b.shape)],
        out_specs=pl.BlockSpec((tm, D_MODEL), row),
        out_shape=jax.ShapeDtypeStruct((t, D_MODEL), F32),
        compiler_params=_params("parallel"),
        name="ffn_ln",
    )(x, w_in, w_out, g, b)


def _in_proj_body(x_ref, wg_ref, wz_ref, wxbc_ref, wq_ref, wckv_ref, wqi_ref, wmisc_ref, kvn_ref,
                  sg_ref, sz_ref, xbc_ref, q_ref, ckv_ref, ckvt_ref, qit_ref, ki_ref, dt_ref, wit_ref):
    xb = x_ref[...].astype(BF16)
    sg_ref[...] = jax.nn.sigmoid(_dot_nt(xb, wg_ref[...])).astype(BF16)
    z = _dot_nt(xb, wz_ref[...])
    sz_ref[...] = (z * jax.nn.sigmoid(z)).astype(BF16)
    xbc_ref[...] = _dot_nt(xb, wxbc_ref[...]).astype(BF16)
    q_ref[...] = _dot_nt(xb, wq_ref[...]).astype(BF16)
    c = _dot_nt(xb, wckv_ref[...])
    c = c * lax.rsqrt(jnp.mean(c * c, axis=-1, keepdims=True) + RMS_EPS) * kvn_ref[...]
    ckv_ref[...] = c.astype(BF16)
    ckvt_ref[...] = c.T.astype(BF16)
    qit_ref[...] = _dot_nt(wqi_ref[...], xb).astype(BF16)
    misc = _dot_nt(xb, wmisc_ref[...])
    dt_ref[...] = misc[:, 0:SSD_HEADS]
    ki_ref[...] = misc[:, SSD_HEADS:SSD_HEADS + IDX_DIM].astype(BF16)
    wi_lo = SSD_HEADS + IDX_DIM
    wit_ref[...] = _dot_nt(wmisc_ref[wi_lo:wi_lo + IDX_HEADS, :], xb) * (IDX_HEADS ** -0.5)


def _in_proj(x, wg, wz, wxbc, wq, wckv, wqi, wmisc, kvn, tm):
    t = x.shape[0]
    row = lambda i: (i, 0)
    col = lambda i: (0, i)
    out = [
        ((t, 2 * D_MODEL), BF16, (tm, 2 * D_MODEL), row),
        ((t, SSD_INNER), BF16, (tm, SSD_INNER), row),
        ((t, SSD_XBC), BF16, (tm, SSD_XBC), row),
        ((t, ATT_WIDTH), BF16, (tm, ATT_WIDTH), row),
        ((t, KV_RANK), BF16, (tm, KV_RANK), row),
        ((KV_RANK, t), BF16, (KV_RANK, tm), col),
        ((IDX_HEADS * IDX_DIM, t), BF16, (IDX_HEADS * IDX_DIM, tm), col),
        ((t, IDX_DIM), BF16, (tm, IDX_DIM), row),
        ((t, SSD_HEADS), F32, (tm, SSD_HEADS), row),
        ((IDX_HEADS, t), F32, (IDX_HEADS, tm), col),
    ]
    return pl.pallas_call(
        _in_proj_body,
        grid=(t // tm,),
        in_specs=[pl.BlockSpec((tm, D_MODEL), row)] + [_const_spec(w.shape) for w in
                                                        (wg, wz, wxbc, wq, wckv, wqi, wmisc, kvn)],
        out_specs=[pl.BlockSpec(blk, im) for _, _, blk, im in out],
        out_shape=[jax.ShapeDtypeStruct(shp, dt) for shp, dt, _, _ in out],
        compiler_params=_params("parallel"),
        name="in_proj",
    )(x, wg, wz, wxbc, wq, wckv, wqi, wmisc, kvn)


def _ssd_body(xbc_ref, sz_ref, dt_ref, cw_ref, cb_ref, dtb_ref, alog_ref, dskip_ref, nw_ref, expand_ref,
              y_ref, state_ref, ext_ref):
    chunk = SSD_CHUNK
    n_sub = xbc_ref.shape[0] // chunk
    hp = SSD_HEAD_DIM
    gw = SSD_INNER // SSD_GROUPS

    @pl.when(pl.program_id(1) == 0)
    def _():
        state_ref[...] = jnp.zeros(state_ref.shape, F32)
        ext_ref[0:CONV_TAIL, :] = jnp.zeros((CONV_TAIL, SSD_XBC), BF16)

    ext_ref[CONV_TAIL:, :] = xbc_ref[...]
    t_i = lax.broadcasted_iota(jnp.int32, (chunk, CONV_TAIL + chunk), 0)
    j_i = lax.broadcasted_iota(jnp.int32, (chunk, CONV_TAIL + chunk), 1)
    r_i = lax.broadcasted_iota(jnp.int32, (chunk, chunk), 0)
    c_i = lax.broadcasted_iota(jnp.int32, (chunk, chunk), 1)
    causal = r_i >= c_i
    causal_b = causal.astype(BF16)
    eye = (lax.broadcasted_iota(jnp.int32, (SSD_HEADS, SSD_HEADS), 0)
           == lax.broadcasted_iota(jnp.int32, (SSD_HEADS, SSD_HEADS), 1)).astype(BF16)
    expand = expand_ref[...]
    neg_a = -jnp.exp(alog_ref[...])
    state = state_ref[...]

    for sub in range(n_sub):
        rows = slice(sub * chunk, (sub + 1) * chunk)
        ext = ext_ref[sub * chunk:sub * chunk + CONV_TAIL + chunk, :]
        conv = cb_ref[...] + cw_ref[SSD_CONV - 1:SSD_CONV, :] * xbc_ref[rows, :].astype(F32)
        for k in range(SSD_CONV - 1):
            shift = (j_i == t_i + (CONV_TAIL - (SSD_CONV - 1) + k)).astype(BF16)
            conv = conv + cw_ref[k:k + 1, :] * _dot(shift, ext)
        act = conv * jax.nn.sigmoid(conv)
        xs = act[:, :SSD_INNER]

        pre = dt_ref[rows, :] + dtb_ref[...]
        dt = jnp.maximum(pre, 0.0) + jnp.log(1.0 + jnp.exp(-jnp.abs(pre)))
        a = dt * neg_a
        a_cs = _select_sum(causal_b, _split3(a), lhs_is_01=True)
        a_cs_3 = _split3(a_cs)
        a_cs_t = sum(_dot_nt(eye, piece) for piece in a_cs_3)
        dt_e = _select_sum(expand, _split3(dt), lhs_is_01=False)
        a_cs_e = _select_sum(expand, a_cs_3, lhs_is_01=False)
        a_end_e = a_cs_e[chunk - 1:chunk, :]

        xdt = xs * dt_e
        xdt_b = xdt.astype(BF16)
        xdt_end_b = (xdt * jnp.exp(a_end_e - a_cs_e)).astype(BF16)
        state_b = state.astype(BF16)

        y_diag, y_off, new_state = [], [], []
        for g in range(SSD_GROUPS):
            bm = act[:, SSD_INNER + g * SSD_STATE:SSD_INNER + (g + 1) * SSD_STATE].astype(BF16)
            cm = act[:, SSD_INNER + SSD_GN + g * SSD_STATE:SSD_INNER + SSD_GN + (g + 1) * SSD_STATE].astype(BF16)
            cb = _dot_nt(cm, bm)
            y_off.append(_dot(cm, state_b[:, g * gw:(g + 1) * gw]))
            new_state.append(lax.dot_general(bm, xdt_end_b[:, g * gw:(g + 1) * gw], TN_DIMS,
                                             preferred_element_type=F32))
            for r in range(SSD_HEADS // SSD_GROUPS):
                h = g * (SSD_HEADS // SSD_GROUPS) + r
                seg = a_cs[:, h:h + 1] - a_cs_t[h:h + 1, :]
                decay = jnp.exp(jnp.where(causal, seg, NEG_BIG))
                y_diag.append(_dot((cb * decay).astype(BF16), xdt_b[:, h * hp:(h + 1) * hp]))
        state = jnp.exp(a_end_e) * state + jnp.concatenate(new_state, axis=1)

        y = jnp.concatenate(y_diag, axis=1) + jnp.concatenate(y_off, axis=1) * jnp.exp(a_cs_e)
        y = (y + dskip_ref[...] * xs) * sz_ref[rows, :].astype(F32)
        outs = []
        for g in range(SSD_GROUPS):
            yg = y[:, g * gw:(g + 1) * gw]
            outs.append(yg * lax.rsqrt(jnp.mean(yg * yg, axis=-1, keepdims=True) + RMS_EPS))
        y_ref[rows, :] = (jnp.concatenate(outs, axis=1) * nw_ref[...]).astype(BF16)

    state_ref[...] = state
    ext_ref[0:CONV_TAIL, :] = ext_ref[n_sub * chunk:n_sub * chunk + CONV_TAIL, :]


def _ssd(xbc, sz, dt, conv_w, conv_b, dt_bias, a_log, d_skip_e, norm_w, expand, bsz, seqlen):
    t = xbc.shape[0]
    n_sub = SSD_STEP_CHUNKS if (seqlen // SSD_CHUNK) % SSD_STEP_CHUNKS == 0 else 1
    rows = n_sub * SSD_CHUNK
    ns = seqlen // rows
    row = lambda b, c: (b * ns + c, 0)
    return pl.pallas_call(
        _ssd_body,
        grid=(bsz, ns),
        in_specs=[pl.BlockSpec((rows, SSD_XBC), row), pl.BlockSpec((rows, SSD_INNER), row),
                  pl.BlockSpec((rows, SSD_HEADS), row)]
                 + [_const_spec(w.shape) for w in (conv_w, conv_b, dt_bias, a_log, d_skip_e, norm_w, expand)],
        out_specs=pl.BlockSpec((rows, SSD_INNER), row),
        out_shape=jax.ShapeDtypeStruct((t, SSD_INNER), BF16),
        scratch_shapes=[pltpu.VMEM((SSD_STATE, SSD_INNER), F32),
                        pltpu.VMEM((CONV_TAIL + rows, SSD_XBC), BF16)],
        compiler_params=_params("arbitrary", "arbitrary"),
        name="ssd",
    )(xbc, sz, dt, conv_w, conv_b, dt_bias, a_log, d_skip_e, norm_w, expand)


def _score_of_key(key):
    return pltpu.bitcast(jnp.where(key >= 0, key, key ^ INT_MAX), F32)


def _dsa_body(slopes_ref, q_ref, qit_ref, wit_ref, ckv_ref, ckvt_ref, ki_ref, wuk_ref, wuv_ref, y_ref,
              sc_ref, sb_ref, bias_ref, ql_ref, s_ref, p_ref, m_ref, l_ref, alpha_ref, acc_ref, *, topk, pos_bits):
    qb = q_ref.shape[0]
    kc = KEY_CHUNK
    t0 = pl.program_id(1) * qb
    n_chunks = (t0 + qb + kc - 1) // kc
    q_pos = t0 + lax.broadcasted_iota(jnp.int32, (1, qb), 1)

    def key_pos(c):
        return c * kc + lax.broadcasted_iota(jnp.int32, (kc, qb), 0)

    def chunk_ds(c):
        return pl.ds(pl.multiple_of(c * kc, kc), kc)

    def score_chunk(c, carry):
        kblk = ki_ref[chunk_ds(c), :]
        score = jnp.zeros((kc, qb), F32)
        for h in range(0, IDX_HEADS, 2):
            w = jnp.concatenate([qit_ref[h * IDX_DIM:(h + 1) * IDX_DIM, :],
                                 qit_ref[(h + 1) * IDX_DIM:(h + 2) * IDX_DIM, :]], axis=1)
            wi = jnp.concatenate([wit_ref[h:h + 1, :], wit_ref[h + 1:h + 2, :]], axis=1)
            t = jnp.maximum(_dot(kblk, w), 0.0) * wi
            score = score + t[:, :qb] + t[:, qb:]
        score = jnp.where(key_pos(c) <= q_pos, score, -jnp.inf)
        sc_ref[chunk_ds(c), :] = score
        sb_ref[chunk_ds(c), :] = score.astype(BF16)
        return carry

    lax.fori_loop(0, n_chunks, score_chunk, 0)

    def count(pred):
        n_acc = 8

        def body(c, parts):
            hit = pred(sc_ref[chunk_ds(c), :], key_pos(c))
            parts = list(parts)
            for i in range(kc // SUBLANES):
                tile = hit[i * SUBLANES:(i + 1) * SUBLANES, :]
                parts[i % n_acc] = jnp.where(tile, parts[i % n_acc] + 1, parts[i % n_acc])
            return tuple(parts)
        parts = lax.fori_loop(0, n_chunks, body, (jnp.zeros((SUBLANES, qb), jnp.int32),) * n_acc)
        return jnp.sum(sum(parts), axis=0, keepdims=True)

    def count_rounded(cand_b):
        n_acc = 4

        def body(c, parts):
            hit = jnp.where(sb_ref[chunk_ds(c), :] >= cand_b, jnp.ones((kc, qb), BF16), jnp.zeros((kc, qb), BF16))
            parts = list(parts)
            for i in range(kc // BF16_ROWS):
                parts[i % n_acc] = parts[i % n_acc] + hit[i * BF16_ROWS:(i + 1) * BF16_ROWS, :]
            return tuple(parts)
        parts = lax.fori_loop(0, n_chunks, body, (jnp.zeros((BF16_ROWS, qb), BF16),) * n_acc)
        total = sum(p.astype(F32) for p in parts)
        return jnp.sum(total, axis=0, keepdims=True).astype(jnp.int32)

    def coarse_bit(i, key16):
        cand = key16 + (jnp.int32(1) << (15 - i))
        pattern = jnp.where(cand >= 0, cand, cand ^ 0x7FFF)
        cand_b = pltpu.bitcast(pattern << 16, F32).astype(BF16)
        return jnp.where(count_rounded(cand_b) >= topk, cand, key16)

    key16 = lax.fori_loop(0, 16, coarse_bit, jnp.full((1, qb), -(2 ** 15), jnp.int32))
    coarse = (key16 << 16) + jnp.where(key16 < 0, 0xFFFF, 0)
    window_lo = jnp.where(coarse < INT_MIN + 2 ** 16, INT_MIN, coarse - 2 ** 16)

    def thr_bit(i, state):
        thr, n_ge = state
        cand = thr + (jnp.int32(1) << (16 - i))
        cand_f = _score_of_key(cand)
        n = count(lambda s, p: s >= cand_f)
        return jnp.where(n >= topk, cand, thr), jnp.where(n >= topk, n, n_ge)

    def pending(state):
        return jnp.max(jnp.where(state[1] <= topk, 0, 1))

    state = (window_lo, jnp.full((1, qb), INT_MAX, jnp.int32))
    state = lax.fori_loop(0, THR_BITS_ALWAYS, thr_bit, state)

    def more_bits(carry):
        i, state, _ = carry
        for b in range(THR_BITS_STEP):
            state = thr_bit(i + b, state)
        return i + THR_BITS_STEP, state, pending(state)

    _, (thr, _), still_open = lax.while_loop(lambda carry: (carry[0] < 17) & (carry[2] > 0), more_bits,
                                             (jnp.int32(THR_BITS_ALWAYS), state, pending(state)))
    thr_f = jnp.where(thr >= LOWEST_FINITE_KEY, _score_of_key(jnp.maximum(thr, LOWEST_FINITE_KEY)), -jnp.inf)

    eye = (lax.broadcasted_iota(jnp.int32, (qb, qb), 0)
           == lax.broadcasted_iota(jnp.int32, (qb, qb), 1)).astype(BF16)

    def write_bias(selected):
        def bias_chunk(c, carry):
            p = key_pos(c)
            sel = (p <= q_pos) & selected(sc_ref[chunk_ds(c), :], p)
            bias_ref[:, chunk_ds(c)] = _dot_nt(eye, jnp.where(sel, 0.0, NEG_BIG).astype(BF16))
            return carry
        return bias_chunk

    def bias_no_ties():
        def before_block(c, carry):
            dropped = sc_ref[chunk_ds(c), :] < thr_f
            bias_ref[:, chunk_ds(c)] = _dot_nt(eye, jnp.where(dropped, NEG_BIG, 0.0).astype(BF16))
            return carry
        n_before = t0 // kc
        lax.fori_loop(0, n_before, before_block, 0)
        lax.fori_loop(n_before, n_chunks, write_bias(lambda s, p: s >= thr_f), 0)

    def bias_checking_ties():
        need = topk - count(lambda s, p: s > thr_f)
        n_tie = count(lambda s, p: (s == thr_f) & (p <= q_pos))

        def bias_with_ties():
            def cut_bit(i, cut):
                cand = cut + (jnp.int32(1) << (pos_bits - 1 - i))
                n_before = count(lambda s, p: (s == thr_f) & (p <= q_pos) & (p < cand))
                return jnp.where(n_before < need, cand, cut)
            cut = lax.fori_loop(0, pos_bits, cut_bit, jnp.zeros((1, qb), jnp.int32))
            lax.fori_loop(0, n_chunks, write_bias(lambda s, p: (s > thr_f) | ((s == thr_f) & (p <= cut))), 0)

        lax.cond(jnp.max(jnp.where(n_tie > need, 1, 0)) > 0, bias_with_ties, bias_no_ties)

    lax.cond(still_open > 0, bias_checking_ties, bias_no_ties)

    for h in range(ATT_HEADS):
        rows = slice(h * qb, (h + 1) * qb)
        ql = _dot(q_ref[:, h * ATT_HEAD_DIM:(h + 1) * ATT_HEAD_DIM], wuk_ref[h])
        ql_ref[rows, :] = (ql * (ATT_HEAD_DIM ** -0.5 * LOG2E)).astype(BF16)
        m_ref[rows, :] = jnp.full((qb, LANES), NEG_BIG, F32)
        l_ref[rows, :] = jnp.zeros((qb, LANES), F32)
        acc_ref[rows, :] = jnp.zeros((qb, KV_RANK), F32)
    lane_tiles = kc // LANES
    heads_per_part = ATT_HEADS // MXU_SPLIT

    part_rows = [slice(r * heads_per_part * qb, (r + 1) * heads_per_part * qb) for r in range(MXU_SPLIT)]

    def logits(part, c):
        rel = (c * kc - t0 + lax.broadcasted_iota(jnp.int32, (1, kc), 1)).astype(F32)
        s_part = _dot(ql_ref[part_rows[part], :], ckvt_ref[:, chunk_ds(c)])
        for i_h in range(heads_per_part):
            h = part * heads_per_part + i_h
            rows = slice(h * qb, (h + 1) * qb)
            s = s_part[i_h * qb:(i_h + 1) * qb, :] + (slopes_ref[h] * LOG2E) * rel + bias_ref[:, chunk_ds(c)]
            s_ref[rows, :] = s
            mx = s[:, 0:LANES]
            for i in range(1, lane_tiles):
                mx = jnp.maximum(mx, s[:, i * LANES:(i + 1) * LANES])
            m_old = m_ref[rows, :]
            m_new = jnp.maximum(m_old, jnp.max(mx, axis=-1, keepdims=True))
            m_ref[rows, :] = m_new
            alpha_ref[rows, :] = jnp.exp2(m_old - m_new)

    def values(part, c):
        cblk = ckv_ref[chunk_ds(c), :]
        for i_h in range(heads_per_part):
            rows = slice((part * heads_per_part + i_h) * qb, (part * heads_per_part + i_h + 1) * qb)
            m = m_ref[rows, :]
            l = alpha_ref[rows, :] * l_ref[rows, :]
            for i in range(lane_tiles):
                p = jnp.exp2(s_ref[rows, i * LANES:(i + 1) * LANES] - m)
                l = l + p
                p_ref[rows, i * LANES:(i + 1) * LANES] = p.astype(BF16)
            l_ref[rows, :] = l
        upd = _dot(p_ref[part_rows[part], :], cblk)
        for i_h in range(heads_per_part):
            rows = slice((part * heads_per_part + i_h) * qb, (part * heads_per_part + i_h + 1) * qb)
            alpha = alpha_ref[rows, :]
            for i in range(KV_RANK // LANES):
                cols = slice(i * LANES, (i + 1) * LANES)
                acc_ref[rows, cols] = alpha * acc_ref[rows, cols] + upd[i_h * qb:(i_h + 1) * qb, cols]

    last = n_chunks - 1
    for part in range(MXU_SPLIT):
        logits(part, 0)

    def att_chunk(c, carry):
        for part in range(MXU_SPLIT):
            values(part, c)
            logits(part, c + 1)
        return carry

    lax.fori_loop(0, last, att_chunk, 0)
    for part in range(MXU_SPLIT):
        values(part, last)

    o = (acc_ref[...] / jnp.sum(l_ref[...], axis=-1, keepdims=True)).astype(BF16)
    y_ref[...] = jnp.concatenate([_dot(o[h * qb:(h + 1) * qb, :], wuv_ref[h]) for h in range(ATT_HEADS)],
                                 axis=1).astype(BF16)


def _dsa(q, qit, wit, ckv, ckvt, ki, w_uk, w_uv, slopes, bsz, seqlen):
    t = q.shape[0]
    qb = Q_BLOCK
    nq = seqlen // qb
    rows = ATT_HEADS * qb
    topk = min(TOPK_MAX, seqlen // 4)
    pos_bits = max(1, (seqlen - 1).bit_length())
    seq_pad = -(-seqlen // KEY_CHUNK) * KEY_CHUNK
    row = lambda b, j: (b * nq + j, 0)
    col = lambda b, j: (0, b * nq + j)
    per_batch = lambda b, j: (b, 0, 0)
    if seq_pad != seqlen:
        ckv = jnp.pad(ckv.reshape(bsz, seqlen, KV_RANK), ((0, 0), (0, seq_pad - seqlen), (0, 0)))
        ki = jnp.pad(ki.reshape(bsz, seqlen, IDX_DIM), ((0, 0), (0, seq_pad - seqlen), (0, 0)))
    ckvt = ckvt.reshape(KV_RANK, bsz, seqlen)
    if seq_pad != seqlen:
        ckvt = jnp.pad(ckvt, ((0, 0), (0, 0), (0, seq_pad - seqlen)))
    ckvt = ckvt.reshape(KV_RANK, bsz * seq_pad)
    body = functools.partial(_dsa_body, topk=topk, pos_bits=pos_bits)
    return pl.pallas_call(
        body,
        grid=(bsz, nq),
        in_specs=[pl.BlockSpec(memory_space=pltpu.SMEM),
                  pl.BlockSpec((qb, ATT_WIDTH), row), pl.BlockSpec((IDX_HEADS * IDX_DIM, qb), col),
                  pl.BlockSpec((IDX_HEADS, qb), col),
                  pl.BlockSpec((None, seq_pad, KV_RANK), per_batch, pipeline_mode=pl.Buffered(1)),
                  pl.BlockSpec((KV_RANK, seq_pad), lambda b, j: (0, b), pipeline_mode=pl.Buffered(1)),
                  pl.BlockSpec((None, seq_pad, IDX_DIM), per_batch, pipeline_mode=pl.Buffered(1)),
                  _const_spec(w_uk.shape), _const_spec(w_uv.shape)],
        out_specs=pl.BlockSpec((qb, ATT_WIDTH), row),
        out_shape=jax.ShapeDtypeStruct((t, ATT_WIDTH), BF16),
        scratch_shapes=[pltpu.VMEM((seq_pad, qb), F32), pltpu.VMEM((seq_pad, qb), BF16),
                        pltpu.VMEM((qb, seq_pad), F32),
                        pltpu.VMEM((rows, KV_RANK), BF16),
                        pltpu.VMEM((rows, KEY_CHUNK), F32), pltpu.VMEM((rows, KEY_CHUNK), BF16),
                        pltpu.VMEM((rows, LANES), F32), pltpu.VMEM((rows, LANES), F32),
                        pltpu.VMEM((rows, LANES), F32), pltpu.VMEM((rows, KV_RANK), F32)],
        compiler_params=_params("arbitrary", "arbitrary"),
        name="dsa",
    )(slopes, q, qit, wit, ckv.reshape(bsz, seq_pad, KV_RANK), ckvt, ki.reshape(bsz, seq_pad, IDX_DIM), w_uk, w_uv)


def _merge_ln_body(x_ref, ys_ref, ya_ref, sg_ref, wps_ref, wpa_ref, wo_ref, g_ref, b_ref, o_ref):
    sub = x_ref.shape[0] // SUB_TILES
    for i in range(SUB_TILES):
        rows = slice(i * sub, (i + 1) * sub)
        sg = sg_ref[rows, :].astype(F32)
        merged = (sg[:, :D_MODEL] * _dot(ys_ref[rows, :], wps_ref[...])
                  + sg[:, D_MODEL:] * _dot(ya_ref[rows, :], wpa_ref[...]))
        out = _dot(merged.astype(BF16), wo_ref[...])
        o_ref[rows, :] = _layer_norm(ALPHA * x_ref[rows, :] + out, g_ref[...], b_ref[...])


def _merge_ln(x, ys, ya, sg, wps, wpa, wo, g, b, tm):
    t = x.shape[0]
    row = lambda i: (i, 0)
    return pl.pallas_call(
        _merge_ln_body,
        grid=(t // tm,),
        in_specs=[pl.BlockSpec((tm, D_MODEL), row), pl.BlockSpec((tm, SSD_INNER), row),
                  pl.BlockSpec((tm, ATT_WIDTH), row), pl.BlockSpec((tm, 2 * D_MODEL), row)]
                 + [_const_spec(w.shape) for w in (wps, wpa, wo, g, b)],
        out_specs=pl.BlockSpec((tm, D_MODEL), row),
        out_shape=jax.ShapeDtypeStruct((t, D_MODEL), F32),
        compiler_params=_params("parallel"),
        name="merge_ln",
    )(x, ys, ya, sg, wps, wpa, wo, g, b)


def _mem_kv_body(m_ref, w_ref, o_ref):
    o_ref[...] = _dot(m_ref[...].astype(BF16), w_ref[...]).astype(BF16)


def _mem_kv(mem, w_mkv, bsz, mem_len):
    row = lambda b: (b, 0)
    return pl.pallas_call(
        _mem_kv_body,
        grid=(bsz,),
        in_specs=[pl.BlockSpec((mem_len, D_MODEL), row), _const_spec(w_mkv.shape)],
        out_specs=pl.BlockSpec((mem_len, 2 * D_MODEL), row),
        out_shape=jax.ShapeDtypeStruct((bsz * mem_len, 2 * D_MODEL), BF16),
        compiler_params=_params("parallel"),
        name="mem_kv",
    )(mem, w_mkv)


def _xattn_ln_body(x_ref, kv_ref, wq_ref, wo_ref, g_ref, b_ref, o_ref):
    sub = x_ref.shape[0] // SUB_TILES
    for i in range(SUB_TILES):
        rows = slice(i * sub, (i + 1) * sub)
        x = x_ref[rows, :]
        q = (_dot(x.astype(BF16), wq_ref[...]) * (MEM_HEAD_DIM ** -0.5)).astype(BF16)
        heads = []
        for h in range(MEM_HEADS):
            lo = h * MEM_HEAD_DIM
            s = _dot_nt(q[:, lo:lo + MEM_HEAD_DIM], kv_ref[:, lo:lo + MEM_HEAD_DIM])
            p = jnp.exp(s - jnp.max(s, axis=-1, keepdims=True))
            o = _dot(p.astype(BF16), kv_ref[:, D_MODEL + lo:D_MODEL + lo + MEM_HEAD_DIM])
            heads.append((o / jnp.sum(p, axis=-1, keepdims=True)).astype(BF16))
        out = _dot(jnp.concatenate(heads, axis=1), wo_ref[...])
        o_ref[rows, :] = _layer_norm(ALPHA * x + out, g_ref[...], b_ref[...])


def _xattn_ln(x, kv, wq, wo, g, b, bsz, seqlen, mem_len, tm):
    t = x.shape[0]
    nt = seqlen // tm
    return pl.pallas_call(
        _xattn_ln_body,
        grid=(bsz, nt),
        in_specs=[pl.BlockSpec((tm, D_MODEL), lambda b_, i: (b_ * nt + i, 0)),
                  pl.BlockSpec((mem_len, 2 * D_MODEL), lambda b_, i: (b_, 0))]
                 + [_const_spec(w.shape) for w in (wq, wo, g, b)],
        out_specs=pl.BlockSpec((tm, D_MODEL), lambda b_, i: (b_ * nt + i, 0)),
        out_shape=jax.ShapeDtypeStruct((t, D_MODEL), F32),
        compiler_params=_params("parallel", "parallel"),
        name="xattn_ln",
    )(x, kv, wq, wo, g, b)


def _split_w_in(w_in):
    w_t = w_in.T
    sizes = (D_MODEL, D_MODEL, SSD_INNER, SSD_XBC, SSD_HEADS, ATT_WIDTH, KV_RANK, IDX_HEADS * IDX_DIM,
             IDX_DIM, IDX_HEADS)
    parts, off = [], 0
    for s in sizes:
        parts.append(w_t[off:off + s, :])
        off += s
    g_ssd, g_att, z, xbc, dt, q, ckv, qi, ki, wi = parts
    pad = jnp.zeros((LANES - SSD_HEADS - IDX_DIM - IDX_HEADS, w_t.shape[1]), w_in.dtype)
    misc = jnp.concatenate([dt, ki, wi, pad], axis=0)
    return [w.astype(BF16) for w in (jnp.concatenate([g_ssd, g_att], axis=0), z, xbc, q, ckv, qi, misc)]


def _layer(x, mem, bsz, seqlen, mem_len, p):
    (ffn1_w_in, ffn1_w_out, ln1_g, ln1_b, w_in, conv_w, conv_b, dt_bias, a_log, d_skip, ssd_norm_w,
     kv_norm_w, w_uk, w_uv, w_proj_ssd, w_proj_att, w_out, ln2_g, ln2_b,
     w_mq, w_mkv, w_mo, ln3_g, ln3_b, ffn2_w_in, ffn2_w_out, ln4_g, ln4_b) = p
    tm = min(TOKEN_TILE, seqlen)
    vec = lambda v: v.reshape(1, -1)
    bf = lambda w: w.astype(BF16)

    x = _ffn_ln(x, bf(ffn1_w_in), bf(ffn1_w_out), vec(ln1_g), vec(ln1_b), tm)

    sg, sz, xbc, q, ckv, ckvt, qit, ki, dt, wit = _in_proj(x, *_split_w_in(w_in), vec(kv_norm_w), tm)

    expand = jnp.repeat(jnp.eye(SSD_HEADS, dtype=BF16), SSD_HEAD_DIM, axis=1)
    y_ssd = _ssd(xbc, sz, dt, conv_w, vec(conv_b), vec(dt_bias), vec(a_log),
                 vec(jnp.repeat(d_skip, SSD_HEAD_DIM)), vec(ssd_norm_w), expand, bsz, seqlen)

    slopes = 2.0 ** (-8.0 * jnp.arange(1, ATT_HEADS + 1, dtype=F32) / ATT_HEADS)
    y_att = _dsa(q, qit, wit, ckv, ckvt, ki, bf(w_uk), bf(w_uv), slopes, bsz, seqlen)

    tm_sub = min(SUB_TILES * TOKEN_TILE, seqlen)
    x = _merge_ln(x, y_ssd, y_att, sg, bf(w_proj_ssd), bf(w_proj_att), bf(w_out), vec(ln2_g), vec(ln2_b), tm_sub)

    kv = _mem_kv(mem, bf(w_mkv), bsz, mem_len)
    x = _xattn_ln(x, kv, bf(w_mq), bf(w_mo), vec(ln3_g), vec(ln3_b), bsz, seqlen, mem_len, tm_sub)

    return _ffn_ln(x, bf(ffn2_w_in), bf(ffn2_w_out), vec(ln4_g), vec(ln4_b), tm)


def kernel(x, mem, ffn1_w_in, ffn1_w_out, ln1_g, ln1_b, w_in, conv_w, conv_b, dt_bias, a_log, d_skip,
           ssd_norm_w, kv_norm_w, w_uk, w_uv, w_proj_ssd, w_proj_att, w_out, ln2_g, ln2_b,
           w_mq, w_mkv, w_mo, ln3_g, ln3_b, ffn2_w_in, ffn2_w_out, ln4_g, ln4_b):
    bsz, seqlen, d = x.shape
    mem_len = mem.shape[1]
    params = (ffn1_w_in, ffn1_w_out, ln1_g, ln1_b, w_in, conv_w, conv_b, dt_bias, a_log, d_skip, ssd_norm_w,
              kv_norm_w, w_uk, w_uv, w_proj_ssd, w_proj_att, w_out, ln2_g, ln2_b,
              w_mq, w_mkv, w_mo, ln3_g, ln3_b, ffn2_w_in, ffn2_w_out, ln4_g, ln4_b)
    h = x.reshape(bsz * seqlen, d)
    m = mem.reshape(bsz * mem_len, d)
    for layer in range(ffn1_w_in.shape[0]):
        h = _layer(h, m, bsz, seqlen, mem_len, tuple(w[layer] for w in params))
    return h.reshape(bsz, seqlen, d)
```

```python
import functools
import math

import jax
import jax.numpy as jnp
from jax import lax
from jax.experimental import pallas as pl
from jax.experimental.pallas import tpu as pltpu

F32 = jnp.float32
BF16 = jnp.bfloat16

D_MODEL = 1024
DEPTH = 1
SSD_HEADS = 16
SSD_HEAD_DIM = 64
SSD_INNER = SSD_HEADS * SSD_HEAD_DIM
SSD_GROUPS = 4
SSD_STATE = 128
SSD_CONV = 4
SSD_CHUNK = 128
SSD_STEP_CHUNKS = 2
SSD_GN = SSD_GROUPS * SSD_STATE
SSD_XBC = SSD_INNER + 2 * SSD_GN
ATT_HEADS = 16
ATT_HEAD_DIM = 64
ATT_WIDTH = ATT_HEADS * ATT_HEAD_DIM
KV_RANK = 256
IDX_HEADS = 16
IDX_DIM = 64
TOPK_MAX = 256
Q_BLOCK = 256
MEM_HEADS = 4
MEM_HEAD_DIM = D_MODEL // MEM_HEADS
D_FF = 2816
ALPHA = (2.0 * DEPTH) ** 0.25
LN_EPS = 1e-5
RMS_EPS = 1e-6

INT_MIN = -(2 ** 31)
INT_MAX = 2 ** 31 - 1
LOWEST_FINITE_KEY = INT_MIN + 2 ** 23
NEG_BIG = -1e30
LOG2E = math.log2(math.e)

LANES = 128
SUBLANES = 8
BF16_ROWS = 16
TOKEN_TILE = 512
SUB_TILES = 2
FFN_CHUNK = 2816
KEY_CHUNK = 512
THR_BITS_ALWAYS = 9
THR_BITS_STEP = 2
MXU_SPLIT = 4
CONV_TAIL = 16
VMEM_LIMIT = 56 * 1024 * 1024

NT_DIMS = (((1,), (1,)), ((), ()))
TN_DIMS = (((0,), (0,)), ((), ()))


def _const_spec(shape):
    return pl.BlockSpec(shape, lambda *_: (0,) * len(shape), pipeline_mode=pl.Buffered(1))


def _params(*sem):
    return pltpu.CompilerParams(dimension_semantics=sem, vmem_limit_bytes=VMEM_LIMIT)


def _layer_norm(y, g, b):
    mu = jnp.mean(y, axis=-1, keepdims=True)
    d = y - mu
    var = jnp.mean(d * d, axis=-1, keepdims=True)
    return d * lax.rsqrt(var + LN_EPS) * g + b


def _dot(a, b):
    return jnp.dot(a, b, preferred_element_type=F32)


def _dot_nt(a, b):
    return lax.dot_general(a, b, NT_DIMS, preferred_element_type=F32)


def _split3(x):
    hi = x.astype(BF16)
    rest = x - hi.astype(F32)
    mid = rest.astype(BF16)
    return hi, mid, (rest - mid.astype(F32)).astype(BF16)


def _select_sum(sel01, pieces, lhs_is_01):
    if lhs_is_01:
        return sum(_dot(sel01, p) for p in pieces)
    return sum(_dot(p, sel01) for p in pieces)


def _ffn_ln_body(x_ref, win_ref, wout_ref, g_ref, b_ref, o_ref):
    x = x_ref[...]
    xb = x.astype(BF16)
    acc = jnp.zeros(x.shape, F32)
    for c in range(D_FF // FFN_CHUNK):
        lo = c * FFN_CHUNK
        gate = _dot(xb, win_ref[:, lo:lo + FFN_CHUNK])
        up = _dot(xb, win_ref[:, D_FF + lo:D_FF + lo + FFN_CHUNK])
        act = (gate * jax.nn.sigmoid(gate) * up).astype(BF16)
        acc = acc + _dot(act, wout_ref[lo:lo + FFN_CHUNK, :])
    o_ref[...] = _layer_norm(ALPHA * x + 0.5 * acc, g_ref[...], b_ref[...])


def _ffn_ln(x, w_in, w_out, g, b, tm):
    t = x.shape[0]
    row = lambda i: (i, 0)
    return pl.pallas_call(
        _ffn_ln_body,
        grid=(t // tm,),
        in_specs=[pl.BlockSpec((tm, D_MODEL), row),
                  _const_spec(w_in.shape), _const_spec(w_out.shape),
                  _const_spec(g.shape), _const_spec(b.shape)],
        out_specs=pl.BlockSpec((tm, D_MODEL), row),
        out_shape=jax.ShapeDtypeStruct((t, D_MODEL), F32),
        compiler_params=_params("parallel"),
        name="ffn_ln",
    )(x, w_in, w_out, g, b)


def _in_proj_body(x_ref, wg_ref, wz_ref, wxbc_ref, wq_ref, wckv_ref, wqi_ref, wmisc_ref, kvn_ref,
                  sg_ref, sz_ref, xbc_ref, q_ref, ckv_ref, ckvt_ref, qit_ref, ki_ref, dt_ref, wit_ref):
    xb = x_ref[...].astype(BF16)
    sg_ref[...] = jax.nn.sigmoid(_dot_nt(xb, wg_ref[...])).astype(BF16)
    z = _dot_nt(xb, wz_ref[...])
    sz_ref[...] = (z * jax.nn.sigmoid(z)).astype(BF16)
    xbc_ref[...] = _dot_nt(xb, wxbc_ref[...]).astype(BF16)
    q_ref[...] = _dot_nt(xb, wq_ref[...]).astype(BF16)
    c = _dot_nt(xb, wckv_ref[...])
    c = c * lax.rsqrt(jnp.mean(c * c, axis=-1, keepdims=True) + RMS_EPS) * kvn_ref[...]
    ckv_ref[...] = c.astype(BF16)
    ckvt_ref[...] = c.T.astype(BF16)
    qit_ref[...] = _dot_nt(wqi_ref[...], xb).astype(BF16)
    misc = _dot_nt(xb, wmisc_ref[...])
    dt_ref[...] = misc[:, 0:SSD_HEADS]
    ki_ref[...] = misc[:, SSD_HEADS:SSD_HEADS + IDX_DIM].astype(BF16)
    wi_lo = SSD_HEADS + IDX_DIM
    wit_ref[...] = _dot_nt(wmisc_ref[wi_lo:wi_lo + IDX_HEADS, :], xb) * (IDX_HEADS ** -0.5)


def _in_proj(x, wg, wz, wxbc, wq, wckv, wqi, wmisc, kvn, tm):
    t = x.shape[0]
    row = lambda i: (i, 0)
    col = lambda i: (0, i)
    out = [
        ((t, 2 * D_MODEL), BF16, (tm, 2 * D_MODEL), row),
        ((t, SSD_INNER), BF16, (tm, SSD_INNER), row),
        ((t, SSD_XBC), BF16, (tm, SSD_XBC), row),
        ((t, ATT_WIDTH), BF16, (tm, ATT_WIDTH), row),
        ((t, KV_RANK), BF16, (tm, KV_RANK), row),
        ((KV_RANK, t), BF16, (KV_RANK, tm), col),
        ((IDX_HEADS * IDX_DIM, t), BF16, (IDX_HEADS * IDX_DIM, tm), col),
        ((t, IDX_DIM), BF16, (tm, IDX_DIM), row),
        ((t, SSD_HEADS), F32, (tm, SSD_HEADS), row),
        ((IDX_HEADS, t), F32, (IDX_HEADS, tm), col),
    ]
    return pl.pallas_call(
        _in_proj_body,
        grid=(t // tm,),
        in_specs=[pl.BlockSpec((tm, D_MODEL), row)] + [_const_spec(w.shape) for w in
                                                        (wg, wz, wxbc, wq, wckv, wqi, wmisc, kvn)],
        out_specs=[pl.BlockSpec(blk, im) for _, _, blk, im in out],
        out_shape=[jax.ShapeDtypeStruct(shp, dt) for shp, dt, _, _ in out],
        compiler_params=_params("parallel"),
        name="in_proj",
    )(x, wg, wz, wxbc, wq, wckv, wqi, wmisc, kvn)


def _ssd_body(xbc_ref, sz_ref, dt_ref, cw_ref, cb_ref, dtb_ref, alog_ref, dskip_ref, nw_ref, expand_ref,
              y_ref, state_ref, ext_ref):
    chunk = SSD_CHUNK
    n_sub = xbc_ref.shape[0] // chunk
    hp = SSD_HEAD_DIM
    gw = SSD_INNER // SSD_GROUPS

    @pl.when(pl.program_id(1) == 0)
    def _():
        state_ref[...] = jnp.zeros(state_ref.shape, F32)
        ext_ref[0:CONV_TAIL, :] = jnp.zeros((CONV_TAIL, SSD_XBC), BF16)

    ext_ref[CONV_TAIL:, :] = xbc_ref[...]
    t_i = lax.broadcasted_iota(jnp.int32, (chunk, CONV_TAIL + chunk), 0)
    j_i = lax.broadcasted_iota(jnp.int32, (chunk, CONV_TAIL + chunk), 1)
    r_i = lax.broadcasted_iota(jnp.int32, (chunk, chunk), 0)
    c_i = lax.broadcasted_iota(jnp.int32, (chunk, chunk), 1)
    causal = r_i >= c_i
    causal_b = causal.astype(BF16)
    eye = (lax.broadcasted_iota(jnp.int32, (SSD_HEADS, SSD_HEADS), 0)
           == lax.broadcasted_iota(jnp.int32, (SSD_HEADS, SSD_HEADS), 1)).astype(BF16)
    expand = expand_ref[...]
    neg_a = -jnp.exp(alog_ref[...])
    state = state_ref[...]

    for sub in range(n_sub):
        rows = slice(sub * chunk, (sub + 1) * chunk)
        ext = ext_ref[sub * chunk:sub * chunk + CONV_TAIL + chunk, :]
        conv = cb_ref[...] + cw_ref[SSD_CONV - 1:SSD_CONV, :] * xbc_ref[rows, :].astype(F32)
        for k in range(SSD_CONV - 1):
            shift = (j_i == t_i + (CONV_TAIL - (SSD_CONV - 1) + k)).astype(BF16)
            conv = conv + cw_ref[k:k + 1, :] * _dot(shift, ext)
        act = conv * jax.nn.sigmoid(conv)
        xs = act[:, :SSD_INNER]

        pre = dt_ref[rows, :] + dtb_ref[...]
        dt = jnp.maximum(pre, 0.0) + jnp.log(1.0 + jnp.exp(-jnp.abs(pre)))
        a = dt * neg_a
        a_cs = _select_sum(causal_b, _split3(a), lhs_is_01=True)
        a_cs_3 = _split3(a_cs)
        a_cs_t = sum(_dot_nt(eye, piece) for piece in a_cs_3)
        dt_e = _select_sum(expand, _split3(dt), lhs_is_01=False)
        a_cs_e = _select_sum(expand, a_cs_3, lhs_is_01=False)
        a_end_e = a_cs_e[chunk - 1:chunk, :]

        xdt = xs * dt_e
        xdt_b = xdt.astype(BF16)
        xdt_end_b = (xdt * jnp.exp(a_end_e - a_cs_e)).astype(BF16)
        state_b = state.astype(BF16)

        y_diag, y_off, new_state = [], [], []
        for g in range(SSD_GROUPS):
            bm = act[:, SSD_INNER + g * SSD_STATE:SSD_INNER + (g + 1) * SSD_STATE].astype(BF16)
            cm = act[:, SSD_INNER + SSD_GN + g * SSD_STATE:SSD_INNER + SSD_GN + (g + 1) * SSD_STATE].astype(BF16)
            cb = _dot_nt(cm, bm)
            y_off.append(_dot(cm, state_b[:, g * gw:(g + 1) * gw]))
            new_state.append(lax.dot_general(bm, xdt_end_b[:, g * gw:(g + 1) * gw], TN_DIMS,
                                             preferred_element_type=F32))
            for r in range(SSD_HEADS // SSD_GROUPS):
                h = g * (SSD_HEADS // SSD_GROUPS) + r
                seg = a_cs[:, h:h + 1] - a_cs_t[h:h + 1, :]
                decay = jnp.exp(jnp.where(causal, seg, NEG_BIG))
                y_diag.append(_dot((cb * decay).astype(BF16), xdt_b[:, h * hp:(h + 1) * hp]))
        state = jnp.exp(a_end_e) * state + jnp.concatenate(new_state, axis=1)

        y = jnp.concatenate(y_diag, axis=1) + jnp.concatenate(y_off, axis=1) * jnp.exp(a_cs_e)
        y = (y + dskip_ref[...] * xs) * sz_ref[rows, :].astype(F32)
        outs = []
        for g in range(SSD_GROUPS):
            yg = y[:, g * gw:(g + 1) * gw]
            outs.append(yg * lax.rsqrt(jnp.mean(yg * yg, axis=-1, keepdims=True) + RMS_EPS))
        y_ref[rows, :] = (jnp.concatenate(outs, axis=1) * nw_ref[...]).astype(BF16)

    state_ref[...] = state
    ext_ref[0:CONV_TAIL, :] = ext_ref[n_sub * chunk:n_sub * chunk + CONV_TAIL, :]


def _ssd(xbc, sz, dt, conv_w, conv_b, dt_bias, a_log, d_skip_e, norm_w, expand, bsz, seqlen):
    t = xbc.shape[0]
    n_sub = SSD_STEP_CHUNKS if (seqlen // SSD_CHUNK) % SSD_STEP_CHUNKS == 0 else 1
    rows = n_sub * SSD_CHUNK
    ns = seqlen // rows
    row = lambda b, c: (b * ns + c, 0)
    return pl.pallas_call(
        _ssd_body,
        grid=(bsz, ns),
        in_specs=[pl.BlockSpec((rows, SSD_XBC), row), pl.BlockSpec((rows, SSD_INNER), row),
                  pl.BlockSpec((rows, SSD_HEADS), row)]
                 + [_const_spec(w.shape) for w in (conv_w, conv_b, dt_bias, a_log, d_skip_e, norm_w, expand)],
        out_specs=pl.BlockSpec((rows, SSD_INNER), row),
        out_shape=jax.ShapeDtypeStruct((t, SSD_INNER), BF16),
        scratch_shapes=[pltpu.VMEM((SSD_STATE, SSD_INNER), F32),
                        pltpu.VMEM((CONV_TAIL + rows, SSD_XBC), BF16)],
        compiler_params=_params("arbitrary", "arbitrary"),
        name="ssd",
    )(xbc, sz, dt, conv_w, conv_b, dt_bias, a_log, d_skip_e, norm_w, expand)


def _score_of_key(key):
    return pltpu.bitcast(jnp.where(key >= 0, key, key ^ INT_MAX), F32)


def _dsa_body(slopes_ref, q_ref, qit_ref, wit_ref, ckv_ref, ckvt_ref, ki_ref, wuk_ref, wuv_ref, y_ref,
              sc_ref, sb_ref, bias_ref, ql_ref, s_ref, p_ref, m_ref, l_ref, alpha_ref, acc_ref, *, topk, pos_bits):
    qb = q_ref.shape[0]
    kc = KEY_CHUNK
    t0 = pl.program_id(1) * qb
    n_chunks = (t0 + qb + kc - 1) // kc
    q_pos = t0 + lax.broadcasted_iota(jnp.int32, (1, qb), 1)

    def key_pos(c):
        return c * kc + lax.broadcasted_iota(jnp.int32, (kc, qb), 0)

    def chunk_ds(c):
        return pl.ds(pl.multiple_of(c * kc, kc), kc)

    def score_chunk(c, carry):
        kblk = ki_ref[chunk_ds(c), :]
        score = jnp.zeros((kc, qb), F32)
        for h in range(0, IDX_HEADS, 2):
            w = jnp.concatenate([qit_ref[h * IDX_DIM:(h + 1) * IDX_DIM, :],
                                 qit_ref[(h + 1) * IDX_DIM:(h + 2) * IDX_DIM, :]], axis=1)
            wi = jnp.concatenate([wit_ref[h:h + 1, :], wit_ref[h + 1:h + 2, :]], axis=1)
            t = jnp.maximum(_dot(kblk, w), 0.0) * wi
            score = score + t[:, :qb] + t[:, qb:]
        score = jnp.where(key_pos(c) <= q_pos, score, -jnp.inf)
        sc_ref[chunk_ds(c), :] = score
        sb_ref[chunk_ds(c), :] = score.astype(BF16)
        return carry

    lax.fori_loop(0, n_chunks, score_chunk, 0)

    def count(pred):
        n_acc = 8

        def body(c, parts):
            hit = pred(sc_ref[chunk_ds(c), :], key_pos(c))
            parts = list(parts)
            for i in range(kc // SUBLANES):
                tile = hit[i * SUBLANES:(i + 1) * SUBLANES, :]
                parts[i % n_acc] = jnp.where(tile, parts[i % n_acc] + 1, parts[i % n_acc])
            return tuple(parts)
        parts = lax.fori_loop(0, n_chunks, body, (jnp.zeros((SUBLANES, qb), jnp.int32),) * n_acc)
        return jnp.sum(sum(parts), axis=0, keepdims=True)

    def count_rounded(cand_b):
        n_acc = 4

        def body(c, parts):
            hit = jnp.where(sb_ref[chunk_ds(c), :] >= cand_b, jnp.ones((kc, qb), BF16), jnp.zeros((kc, qb), BF16))
            parts = list(parts)
            for i in range(kc // BF16_ROWS):
                parts[i % n_acc] = parts[i % n_acc] + hit[i * BF16_ROWS:(i + 1) * BF16_ROWS, :]
            return tuple(parts)
        parts = lax.fori_loop(0, n_chunks, body, (jnp.zeros((BF16_ROWS, qb), BF16),) * n_acc)
        total = sum(p.astype(F32) for p in parts)
        return jnp.sum(total, axis=0, keepdims=True).astype(jnp.int32)

    def coarse_bit(i, key16):
        cand = key16 + (jnp.int32(1) << (15 - i))
        pattern = jnp.where(cand >= 0, cand, cand ^ 0x7FFF)
        cand_b = pltpu.bitcast(pattern << 16, F32).astype(BF16)
        return jnp.where(count_rounded(cand_b) >= topk, cand, key16)

    key16 = lax.fori_loop(0, 16, coarse_bit, jnp.full((1, qb), -(2 ** 15), jnp.int32))
    coarse = (key16 << 16) + jnp.where(key16 < 0, 0xFFFF, 0)
    window_lo = jnp.where(coarse < INT_MIN + 2 ** 16, INT_MIN, coarse - 2 ** 16)

    def thr_bit(i, state):
        thr, n_ge = state
        cand = thr + (jnp.int32(1) << (16 - i))
        cand_f = _score_of_key(cand)
        n = count(lambda s, p: s >= cand_f)
        return jnp.where(n >= topk, cand, thr), jnp.where(n >= topk, n, n_ge)

    def pending(state):
        return jnp.max(jnp.where(state[1] <= topk, 0, 1))

    state = (window_lo, jnp.full((1, qb), INT_MAX, jnp.int32))
    state = lax.fori_loop(0, THR_BITS_ALWAYS, thr_bit, state)

    def more_bits(carry):
        i, state, _ = carry
        for b in range(THR_BITS_STEP):
            state = thr_bit(i + b, state)
        return i + THR_BITS_STEP, state, pending(state)

    _, (thr, _), still_open = lax.while_loop(lambda carry: (carry[0] < 17) & (carry[2] > 0), more_bits,
                                             (jnp.int32(THR_BITS_ALWAYS), state, pending(state)))
    thr_f = jnp.where(thr >= LOWEST_FINITE_KEY, _score_of_key(jnp.maximum(thr, LOWEST_FINITE_KEY)), -jnp.inf)

    eye = (lax.broadcasted_iota(jnp.int32, (qb, qb), 0)
           == lax.broadcasted_iota(jnp.int32, (qb, qb), 1)).astype(BF16)

    def write_bias(selected):
        def bias_chunk(c, carry):
            p = key_pos(c)
            sel = (p <= q_pos) & selected(sc_ref[chunk_ds(c), :], p)
            bias_ref[:, chunk_ds(c)] = _dot_nt(eye, jnp.where(sel, 0.0, NEG_BIG).astype(BF16))
            return carry
        return bias_chunk

    def bias_no_ties():
        def before_block(c, carry):
            dropped = sc_ref[chunk_ds(c), :] < thr_f
            bias_ref[:, chunk_ds(c)] = _dot_nt(eye, jnp.where(dropped, NEG_BIG, 0.0).astype(BF16))
            return carry
        n_before = t0 // kc
        lax.fori_loop(0, n_before, before_block, 0)
        lax.fori_loop(n_before, n_chunks, write_bias(lambda s, p: s >= thr_f), 0)

    def bias_checking_ties():
        need = topk - count(lambda s, p: s > thr_f)
        n_tie = count(lambda s, p: (s == thr_f) & (p <= q_pos))

        def bias_with_ties():
            def cut_bit(i, cut):
                cand = cut + (jnp.int32(1) << (pos_bits - 1 - i))
                n_before = count(lambda s, p: (s == thr_f) & (p <= q_pos) & (p < cand))
                return jnp.where(n_before < need, cand, cut)
            cut = lax.fori_loop(0, pos_bits, cut_bit, jnp.zeros((1, qb), jnp.int32))
            lax.fori_loop(0, n_chunks, write_bias(lambda s, p: (s > thr_f) | ((s == thr_f) & (p <= cut))), 0)

        lax.cond(jnp.max(jnp.where(n_tie > need, 1, 0)) > 0, bias_with_ties, bias_no_ties)

    lax.cond(still_open > 0, bias_checking_ties, bias_no_ties)

    for h in range(ATT_HEADS):
        rows = slice(h * qb, (h + 1) * qb)
        ql = _dot(q_ref[:, h * ATT_HEAD_DIM:(h + 1) * ATT_HEAD_DIM], wuk_ref[h])
        ql_ref[rows, :] = (ql * (ATT_HEAD_DIM ** -0.5 * LOG2E)).astype(BF16)
        m_ref[rows, :] = jnp.full((qb, LANES), NEG_BIG, F32)
        l_ref[rows, :] = jnp.zeros((qb, LANES), F32)
        acc_ref[rows, :] = jnp.zeros((qb, KV_RANK), F32)
    lane_tiles = kc // LANES
    heads_per_part = ATT_HEADS // MXU_SPLIT

    part_rows = [slice(r * heads_per_part * qb, (r + 1) * heads_per_part * qb) for r in range(MXU_SPLIT)]

    def logits(part, c):
        rel = (c * kc - t0 + lax.broadcasted_iota(jnp.int32, (1, kc), 1)).astype(F32)
        s_part = _dot(ql_ref[part_rows[part], :], ckvt_ref[:, chunk_ds(c)])
        for i_h in range(heads_per_part):
            h = part * heads_per_part + i_h
            rows = slice(h * qb, (h + 1) * qb)
            s = s_part[i_h * qb:(i_h + 1) * qb, :] + (slopes_ref[h] * LOG2E) * rel + bias_ref[:, chunk_ds(c)]
            s_ref[rows, :] = s
            mx = s[:, 0:LANES]
            for i in range(1, lane_tiles):
                mx = jnp.maximum(mx, s[:, i * LANES:(i + 1) * LANES])
            m_old = m_ref[rows, :]
            m_new = jnp.maximum(m_old, jnp.max(mx, axis=-1, keepdims=True))
            m_ref[rows, :] = m_new
            alpha_ref[rows, :] = jnp.exp2(m_old - m_new)

    def values(part, c):
        cblk = ckv_ref[chunk_ds(c), :]
        for i_h in range(heads_per_part):
            rows = slice((part * heads_per_part + i_h) * qb, (part * heads_per_part + i_h + 1) * qb)
            m = m_ref[rows, :]
            l = alpha_ref[rows, :] * l_ref[rows, :]
            for i in range(lane_tiles):
                p = jnp.exp2(s_ref[rows, i * LANES:(i + 1) * LANES] - m)
                l = l + p
                p_ref[rows, i * LANES:(i + 1) * LANES] = p.astype(BF16)
            l_ref[rows, :] = l
        upd = _dot(p_ref[part_rows[part], :], cblk)
        for i_h in range(heads_per_part):
            rows = slice((part * heads_per_part + i_h) * qb, (part * heads_per_part + i_h + 1) * qb)
            alpha = alpha_ref[rows, :]
            for i in range(KV_RANK // LANES):
                cols = slice(i * LANES, (i + 1) * LANES)
                acc_ref[rows, cols] = alpha * acc_ref[rows, cols] + upd[i_h * qb:(i_h + 1) * qb, cols]

    last = n_chunks - 1
    for part in range(MXU_SPLIT):
        logits(part, 0)

    def att_chunk(c, carry):
        for part in range(MXU_SPLIT):
            values(part, c)
            logits(part, c + 1)
        return carry

    lax.fori_loop(0, last, att_chunk, 0)
    for part in range(MXU_SPLIT):
        values(part, last)

    o = (acc_ref[...] / jnp.sum(l_ref[...], axis=-1, keepdims=True)).astype(BF16)
    y_ref[...] = jnp.concatenate([_dot(o[h * qb:(h + 1) * qb, :], wuv_ref[h]) for h in range(ATT_HEADS)],
                                 axis=1).astype(BF16)


def _dsa(q, qit, wit, ckv, ckvt, ki, w_uk, w_uv, slopes, bsz, seqlen):
    t = q.shape[0]
    qb = Q_BLOCK
    nq = seqlen // qb
    rows = ATT_HEADS * qb
    topk = min(TOPK_MAX, seqlen // 4)
    pos_bits = max(1, (seqlen - 1).bit_length())
    seq_pad = -(-seqlen // KEY_CHUNK) * KEY_CHUNK
    row = lambda b, j: (b * nq + j, 0)
    col = lambda b, j: (0, b * nq + j)
    per_batch = lambda b, j: (b, 0, 0)
    if seq_pad != seqlen:
        ckv = jnp.pad(ckv.reshape(bsz, seqlen, KV_RANK), ((0, 0), (0, seq_pad - seqlen), (0, 0)))
        ki = jnp.pad(ki.reshape(bsz, seqlen, IDX_DIM), ((0, 0), (0, seq_pad - seqlen), (0, 0)))
    ckvt = ckvt.reshape(KV_RANK, bsz, seqlen)
    if seq_pad != seqlen:
        ckvt = jnp.pad(ckvt, ((0, 0), (0, 0), (0, seq_pad - seqlen)))
    ckvt = ckvt.reshape(KV_RANK, bsz * seq_pad)
    body = functools.partial(_dsa_body, topk=topk, pos_bits=pos_bits)
    return pl.pallas_call(
        body,
        grid=(bsz, nq),
        in_specs=[pl.BlockSpec(memory_space=pltpu.SMEM),
                  pl.BlockSpec((qb, ATT_WIDTH), row), pl.BlockSpec((IDX_HEADS * IDX_DIM, qb), col),
                  pl.BlockSpec((IDX_HEADS, qb), col),
                  pl.BlockSpec((None, seq_pad, KV_RANK), per_batch, pipeline_mode=pl.Buffered(1)),
                  pl.BlockSpec((KV_RANK, seq_pad), lambda b, j: (0, b), pipeline_mode=pl.Buffered(1)),
                  pl.BlockSpec((None, seq_pad, IDX_DIM), per_batch, pipeline_mode=pl.Buffered(1)),
                  _const_spec(w_uk.shape), _const_spec(w_uv.shape)],
        out_specs=pl.BlockSpec((qb, ATT_WIDTH), row),
        out_shape=jax.ShapeDtypeStruct((t, ATT_WIDTH), BF16),
        scratch_shapes=[pltpu.VMEM((seq_pad, qb), F32), pltpu.VMEM((seq_pad, qb), BF16),
                        pltpu.VMEM((qb, seq_pad), F32),
                        pltpu.VMEM((rows, KV_RANK), BF16),
                        pltpu.VMEM((rows, KEY_CHUNK), F32), pltpu.VMEM((rows, KEY_CHUNK), BF16),
                        pltpu.VMEM((rows, LANES), F32), pltpu.VMEM((rows, LANES), F32),
                        pltpu.VMEM((rows, LANES), F32), pltpu.VMEM((rows, KV_RANK), F32)],
        compiler_params=_params("arbitrary", "arbitrary"),
        name="dsa",
    )(slopes, q, qit, wit, ckv.reshape(bsz, seq_pad, KV_RANK), ckvt, ki.reshape(bsz, seq_pad, IDX_DIM), w_uk, w_uv)


def _merge_ln_body(x_ref, ys_ref, ya_ref, sg_ref, wps_ref, wpa_ref, wo_ref, g_ref, b_ref, o_ref):
    sub = x_ref.shape[0] // SUB_TILES
    for i in range(SUB_TILES):
        rows = slice(i * sub, (i + 1) * sub)
        sg = sg_ref[rows, :].astype(F32)
        merged = (sg[:, :D_MODEL] * _dot(ys_ref[rows, :], wps_ref[...])
                  + sg[:, D_MODEL:] * _dot(ya_ref[rows, :], wpa_ref[...]))
        out = _dot(merged.astype(BF16), wo_ref[...])
        o_ref[rows, :] = _layer_norm(ALPHA * x_ref[rows, :] + out, g_ref[...], b_ref[...])


def _merge_ln(x, ys, ya, sg, wps, wpa, wo, g, b, tm):
    t = x.shape[0]
    row = lambda i: (i, 0)
    return pl.pallas_call(
        _merge_ln_body,
        grid=(t // tm,),
        in_specs=[pl.BlockSpec((tm, D_MODEL), row), pl.BlockSpec((tm, SSD_INNER), row),
                  pl.BlockSpec((tm, ATT_WIDTH), row), pl.BlockSpec((tm, 2 * D_MODEL), row)]
                 + [_const_spec(w.shape) for w in (wps, wpa, wo, g, b)],
        out_specs=pl.BlockSpec((tm, D_MODEL), row),
        out_shape=jax.ShapeDtypeStruct((t, D_MODEL), F32),
        compiler_params=_params("parallel"),
        name="merge_ln",
    )(x, ys, ya, sg, wps, wpa, wo, g, b)


def _mem_kv_body(m_ref, w_ref, o_ref):
    o_ref[...] = _dot(m_ref[...].astype(BF16), w_ref[...]).astype(BF16)


def _mem_kv(mem, w_mkv, bsz, mem_len):
    row = lambda b: (b, 0)
    return pl.pallas_call(
        _mem_kv_body,
        grid=(bsz,),
        in_specs=[pl.BlockSpec((mem_len, D_MODEL), row), _const_spec(w_mkv.shape)],
        out_specs=pl.BlockSpec((mem_len, 2 * D_MODEL), row),
        out_shape=jax.ShapeDtypeStruct((bsz * mem_len, 2 * D_MODEL), BF16),
        compiler_params=_params("parallel"),
        name="mem_kv",
    )(mem, w_mkv)


def _xattn_ln_body(x_ref, kv_ref, wq_ref, wo_ref, g_ref, b_ref, o_ref):
    sub = x_ref.shape[0] // SUB_TILES
    for i in range(SUB_TILES):
        rows = slice(i * sub, (i + 1) * sub)
        x = x_ref[rows, :]
        q = (_dot(x.astype(BF16), wq_ref[...]) * (MEM_HEAD_DIM ** -0.5)).astype(BF16)
        heads = []
        for h in range(MEM_HEADS):
            lo = h * MEM_HEAD_DIM
            s = _dot_nt(q[:, lo:lo + MEM_HEAD_DIM], kv_ref[:, lo:lo + MEM_HEAD_DIM])
            p = jnp.exp(s - jnp.max(s, axis=-1, keepdims=True))
            o = _dot(p.astype(BF16), kv_ref[:, D_MODEL + lo:D_MODEL + lo + MEM_HEAD_DIM])
            heads.append((o / jnp.sum(p, axis=-1, keepdims=True)).astype(BF16))
        out = _dot(jnp.concatenate(heads, axis=1), wo_ref[...])
        o_ref[rows, :] = _layer_norm(ALPHA * x + out, g_ref[...], b_ref[...])


def _xattn_ln(x, kv, wq, wo, g, b, bsz, seqlen, mem_len, tm):
    t = x.shape[0]
    nt = seqlen // tm
    return pl.pallas_call(
        _xattn_ln_body,
        grid=(bsz, nt),
        in_specs=[pl.BlockSpec((tm, D_MODEL), lambda b_, i: (b_ * nt + i, 0)),
                  pl.BlockSpec((mem_len, 2 * D_MODEL), lambda b_, i: (b_, 0))]
                 + [_const_spec(w.shape) for w in (wq, wo, g, b)],
        out_specs=pl.BlockSpec((tm, D_MODEL), lambda b_, i: (b_ * nt + i, 0)),
        out_shape=jax.ShapeDtypeStruct((t, D_MODEL), F32),
        compiler_params=_params("parallel", "parallel"),
        name="xattn_ln",
    )(x, kv, wq, wo, g, b)


def _split_w_in(w_in):
    w_t = w_in.T
    sizes = (D_MODEL, D_MODEL, SSD_INNER, SSD_XBC, SSD_HEADS, ATT_WIDTH, KV_RANK, IDX_HEADS * IDX_DIM,
             IDX_DIM, IDX_HEADS)
    parts, off = [], 0
    for s in sizes:
        parts.append(w_t[off:off + s, :])
        off += s
    g_ssd, g_att, z, xbc, dt, q, ckv, qi, ki, wi = parts
    pad = jnp.zeros((LANES - SSD_HEADS - IDX_DIM - IDX_HEADS, w_t.shape[1]), w_in.dtype)
    misc = jnp.concatenate([dt, ki, wi, pad], axis=0)
    return [w.astype(BF16) for w in (jnp.concatenate([g_ssd, g_att], axis=0), z, xbc, q, ckv, qi, misc)]


def _layer(x, mem, bsz, seqlen, mem_len, p):
    (ffn1_w_in, ffn1_w_out, ln1_g, ln1_b, w_in, conv_w, conv_b, dt_bias, a_log, d_skip, ssd_norm_w,
     kv_norm_w, w_uk, w_uv, w_proj_ssd, w_proj_att, w_out, ln2_g, ln2_b,
     w_mq, w_mkv, w_mo, ln3_g, ln3_b, ffn2_w_in, ffn2_w_out, ln4_g, ln4_b) = p
    tm = min(TOKEN_TILE, seqlen)
    tm_sub = min(SUB_TILES * TOKEN_TILE, seqlen)
    assert (bsz * seqlen) % tm == 0 and seqlen % tm_sub == 0 and tm_sub % (SUB_TILES * SUBLANES) == 0, (bsz, seqlen)
    assert seqlen % Q_BLOCK == 0 and seqlen % SSD_CHUNK == 0, seqlen
    assert seqlen <= 256 * 4 * BF16_ROWS, seqlen
    vec = lambda v: v.reshape(1, -1)
    bf = lambda w: w.astype(BF16)

    x = _ffn_ln(x, bf(ffn1_w_in), bf(ffn1_w_out), vec(ln1_g), vec(ln1_b), tm)

    sg, sz, xbc, q, ckv, ckvt, qit, ki, dt, wit = _in_proj(x, *_split_w_in(w_in), vec(kv_norm_w), tm)

    expand = jnp.repeat(jnp.eye(SSD_HEADS, dtype=BF16), SSD_HEAD_DIM, axis=1)
    y_ssd = _ssd(xbc, sz, dt, conv_w, vec(conv_b), vec(dt_bias), vec(a_log),
                 vec(jnp.repeat(d_skip, SSD_HEAD_DIM)), vec(ssd_norm_w), expand, bsz, seqlen)

    slopes = 2.0 ** (-8.0 * jnp.arange(1, ATT_HEADS + 1, dtype=F32) / ATT_HEADS)
    y_att = _dsa(q, qit, wit, ckv, ckvt, ki, bf(w_uk), bf(w_uv), slopes, bsz, seqlen)

    x = _merge_ln(x, y_ssd, y_att, sg, bf(w_proj_ssd), bf(w_proj_att), bf(w_out), vec(ln2_g), vec(ln2_b), tm_sub)

    kv = _mem_kv(mem, bf(w_mkv), bsz, mem_len)
    x = _xattn_ln(x, kv, bf(w_mq), bf(w_mo), vec(ln3_g), vec(ln3_b), bsz, seqlen, mem_len, tm_sub)

    return _ffn_ln(x, bf(ffn2_w_in), bf(ffn2_w_out), vec(ln4_g), vec(ln4_b), tm)


def kernel(x, mem, ffn1_w_in, ffn1_w_out, ln1_g, ln1_b, w_in, conv_w, conv_b, dt_bias, a_log, d_skip,
           ssd_norm_w, kv_norm_w, w_uk, w_uv, w_proj_ssd, w_proj_att, w_out, ln2_g, ln2_b,
           w_mq, w_mkv, w_mo, ln3_g, ln3_b, ffn2_w_in, ffn2_w_out, ln4_g, ln4_b):
    bsz, seqlen, d = x.shape
    mem_len = mem.shape[1]
    params = (ffn1_w_in, ffn1_w_out, ln1_g, ln1_b, w_in, conv_w, conv_b, dt_bias, a_log, d_skip, ssd_norm_w,
              kv_norm_w, w_uk, w_uv, w_proj_ssd, w_proj_att, w_out, ln2_g, ln2_b,
              w_mq, w_mkv, w_mo, ln3_g, ln3_b, ffn2_w_in, ffn2_w_out, ln4_g, ln4_b)
    h = x.reshape(bsz * seqlen, d)
    m = mem.reshape(bsz * mem_len, d)
    for layer in range(ffn1_w_in.shape[0]):
        h = _layer(h, m, bsz, seqlen, mem_len, tuple(w[layer] for w in params))
    return h.reshape(bsz, seqlen, d)
```
